```python
import math
import jax, jax.numpy as jnp
from jax import lax
import numpy as np

D_MODEL = 1024
BATCH = 8
SEQ = 4096
DEPTH = 2
DEC_BATCH = 8
DEC_SEQ = 8192
PAST_LEN = 128

HEAD_DIM = 64
GRID_W = 64
EPS = 1e-6
ROPE_THETA = 10000.0
A_HEADS = 4
A_QK = A_HEADS * 2 * HEAD_DIM
A_VDIM = 2 * HEAD_DIM
A_WIDTH = A_HEADS * A_VDIM
Q_BLOCK = 128
B_HEADS = 4
B_WIDTH = B_HEADS * HEAD_DIM
NA_KH = 8
NA_KW = 16
NA_COL_BLOCK = 16
NA_KEY_COLS = 32
C_HEADS = 4
C_KV_HEADS = 2
C_WIDTH = C_HEADS * HEAD_DIM
C_KV_WIDTH = C_KV_HEADS * HEAD_DIM
WINDOW = 128
BAND_BLOCK = 128
MIX_WIDTH = A_WIDTH + B_WIDTH + C_WIDTH
IN_SPLITS = (A_QK, A_QK, A_WIDTH, B_WIDTH, B_WIDTH, B_WIDTH, C_WIDTH, C_KV_WIDTH, C_KV_WIDTH)
IN_WIDTH = sum(IN_SPLITS)
N_GROUPS = 4
EXPERTS_PER_GROUP = 4
N_EXPERTS = N_GROUPS * EXPERTS_PER_GROUP
TOP_K = 2
D_EXPERT = 512

kernel_name = "hybrid_parallel_heads_encoder"


def rmsnorm(x, g):
    xf = x.astype(jnp.float32)
    y = xf * lax.rsqrt(jnp.mean(xf * xf, axis=-1, keepdims=True) + EPS)
    return (y * g.astype(jnp.float32)).astype(x.dtype)


def rope_tables(seq_len, dtype):
    pos = jnp.arange(seq_len, dtype=jnp.float32)
    inv = ROPE_THETA ** (-jnp.arange(0, HEAD_DIM, 2, dtype=jnp.float32) / HEAD_DIM)
    ang = pos[:, None] * inv[None, :]
    ang = jnp.concatenate([ang, ang], axis=-1)
    return jnp.cos(ang).astype(dtype), jnp.sin(ang).astype(dtype)


def apply_rope(x, cos, sin):
    h = x.shape[-1] // 2
    rot = jnp.concatenate([-x[..., h:], x[..., :h]], axis=-1)
    return x * cos + rot * sin


def diff_attention(q, k, v, qn, kn, lq1, lk1, lq2, lk2, subln, lambda_init, cos, sin):
    B, S, _ = q.shape
    q = q.reshape(B, S, A_HEADS, 2, HEAD_DIM).transpose(0, 2, 3, 1, 4)
    k = k.reshape(B, S, A_HEADS, 2, HEAD_DIM).transpose(0, 2, 3, 1, 4)
    v = v.reshape(B, S, A_HEADS, A_VDIM).transpose(0, 2, 1, 3)
    q = apply_rope(rmsnorm(q, qn), cos, sin)
    k = apply_rope(rmsnorm(k, kn), cos, sin)
    f32 = jnp.float32
    lam = (jnp.exp(jnp.sum(lq1.astype(f32) * lk1.astype(f32)))
           - jnp.exp(jnp.sum(lq2.astype(f32) * lk2.astype(f32))) + lambda_init)
    nb = S // Q_BLOCK
    qb = jnp.moveaxis(q.reshape(B, A_HEADS, 2, nb, Q_BLOCK, HEAD_DIM), 3, 0)
    scale = HEAD_DIM ** -0.5

    def block(qi):
        s = jnp.einsum('bhcqd,bhckd->bhcqk', qi, k).astype(f32) * scale
        p = jax.nn.softmax(s, axis=-1)
        w = p[:, :, 0] - lam * p[:, :, 1]
        return jnp.einsum('bhqk,bhkv->bhqv', w.astype(v.dtype), v)

    o = lax.map(block, qb)
    o = jnp.moveaxis(o, 0, 2).reshape(B, A_HEADS, S, A_VDIM)
    o = rmsnorm(o, subln) * (1.0 - lambda_init)
    return o.transpose(0, 2, 1, 3).reshape(B, S, A_WIDTH)


def na_column_tables():
    n_cb = GRID_W // NA_COL_BLOCK
    starts = np.clip(np.arange(n_cb) * NA_COL_BLOCK - NA_KW // 2, 0, GRID_W - NA_KEY_COLS)
    key_col = starts[:, None] + np.arange(NA_KEY_COLS)[None, :]
    q_col = np.arange(n_cb)[:, None] * NA_COL_BLOCK + np.arange(NA_COL_BLOCK)[None, :]
    c_start = np.clip(q_col - NA_KW // 2, 0, GRID_W - NA_KW)
    kc = key_col[:, None, :]
    valid = (kc >= c_start[..., None]) & (kc < c_start[..., None] + NA_KW)
    bias_idx = np.clip(kc - q_col[..., None] + NA_KW - 1, 0, 2 * NA_KW - 2)
    return key_col, valid, bias_idx


def neighborhood_attention(q, k, v, qn, kn, rpb):
    B, S, _ = q.shape
    rows = S // GRID_W
    kh = min(NA_KH, rows)
    n_cb = GRID_W // NA_COL_BLOCK
    key_col, col_valid, col_bias_idx = na_column_tables()

    def grid(t):
        return t.reshape(B, rows, GRID_W, B_HEADS, HEAD_DIM).transpose(0, 3, 1, 2, 4)

    qg = grid(rmsnorm(q.reshape(B, S, B_HEADS, HEAD_DIM), qn))
    kg = grid(rmsnorm(k.reshape(B, S, B_HEADS, HEAD_DIM), kn))
    vg = grid(v.reshape(B, S, B_HEADS, HEAD_DIM))
    flat_cols = key_col.reshape(-1)
    kcol = jnp.take(kg, flat_cols, axis=3).reshape(B, B_HEADS, rows, n_cb, NA_KEY_COLS, HEAD_DIM)
    vcol = jnp.take(vg, flat_cols, axis=3).reshape(B, B_HEADS, rows, n_cb, NA_KEY_COLS, HEAD_DIM)
    scale = HEAD_DIM ** -0.5
    f32 = jnp.float32

    def row(r):
        rs = jnp.clip(r - kh // 2, 0, rows - kh)
        kr = lax.dynamic_slice_in_dim(kcol, rs, kh, axis=2)
        vr = lax.dynamic_slice_in_dim(vcol, rs, kh, axis=2)
        qr = lax.dynamic_index_in_dim(qg, r, axis=2, keepdims=False)
        qr = qr.reshape(B, B_HEADS, n_cb, NA_COL_BLOCK, HEAD_DIM)
        s = jnp.einsum('bhcqd,bhrcjd->bhcqrj', qr, kr).astype(f32) * scale
        row_idx = rs + jnp.arange(kh) - r + NA_KH - 1
        rb = jnp.take(rpb, row_idx, axis=1)
        bias = jnp.take(rb, col_bias_idx, axis=2)
        s = s + bias.transpose(0, 2, 3, 1, 4).astype(f32)
        s = jnp.where(col_valid[:, :, None, :], s, -jnp.inf)
        p = jax.nn.softmax(s.reshape(B, B_HEADS, n_cb, NA_COL_BLOCK, kh * NA_KEY_COLS), axis=-1)
        p = p.reshape(B, B_HEADS, n_cb, NA_COL_BLOCK, kh, NA_KEY_COLS)
        o = jnp.einsum('bhcqrj,bhrcjd->bhcqd', p.astype(vr.dtype), vr)
        return o.reshape(B, B_HEADS, GRID_W, HEAD_DIM)

    o = lax.map(row, jnp.arange(rows))
    return o.transpose(1, 0, 3, 2, 4).reshape(B, S, B_WIDTH)


def window_gqa(q, k, v, qn, kn, sink, cos, sin):
    B, S, _ = q.shape
    G = C_HEADS // C_KV_HEADS
    q = q.reshape(B, S, C_KV_HEADS, G, HEAD_DIM).transpose(0, 2, 3, 1, 4)
    k = k.reshape(B, S, C_KV_HEADS, HEAD_DIM).transpose(0, 2, 1, 3)
    v = v.reshape(B, S, C_KV_HEADS, HEAD_DIM).transpose(0, 2, 1, 3)
    q = apply_rope(rmsnorm(q, qn), cos, sin)
    k = apply_rope(rmsnorm(k, kn), cos, sin)
    nb = S // BAND_BLOCK
    pad = ((0, 0), (0, 0), (BAND_BLOCK, BAND_BLOCK), (0, 0))
    kb = jnp.pad(k, pad).reshape(B, C_KV_HEADS, nb + 2, BAND_BLOCK, HEAD_DIM)
    vb = jnp.pad(v, pad).reshape(B, C_KV_HEADS, nb + 2, BAND_BLOCK, HEAD_DIM)
    kw = jnp.concatenate([kb[:, :, 0:nb], kb[:, :, 1:nb + 1], kb[:, :, 2:nb + 2]], axis=3)
    vw = jnp.concatenate([vb[:, :, 0:nb], vb[:, :, 1:nb + 1], vb[:, :, 2:nb + 2]], axis=3)
    qb = q.reshape(B, C_KV_HEADS, G, nb, BAND_BLOCK, HEAD_DIM)
    f32 = jnp.float32
    s = jnp.einsum('bkgnqd,bknjd->bkgnqj', qb, kw).astype(f32) * (HEAD_DIM ** -0.5)
    qpos = jnp.arange(nb)[:, None] * BAND_BLOCK + jnp.arange(BAND_BLOCK)[None, :]
    kpos = (jnp.arange(nb)[:, None] - 1) * BAND_BLOCK + jnp.arange(3 * BAND_BLOCK)[None, :]
    valid = ((jnp.abs(qpos[:, :, None] - kpos[:, None, :]) <= WINDOW)
             & (kpos[:, None, :] >= 0) & (kpos[:, None, :] < S))
    s = jnp.where(valid, s, -jnp.inf)
    sk = sink.astype(f32).reshape(1, C_KV_HEADS, G, 1, 1, 1)
    m = jnp.maximum(jnp.max(s, axis=-1, keepdims=True), sk)
    e = jnp.exp(s - m)
    p = e / (jnp.sum(e, axis=-1, keepdims=True) + jnp.exp(sk - m))
    o = jnp.einsum('bkgnqj,bknjd->bkgnqd', p.astype(vw.dtype), vw)
    return o.reshape(B, C_HEADS, S, HEAD_DIM).transpose(0, 2, 1, 3).reshape(B, S, C_WIDTH)


def hier_moe(x, w_group, b_group, w_router, b_router, w_gate, w_up, w_down):
    B, S, _ = x.shape
    f32 = jnp.float32
    glog = jnp.einsum('bsd,dg->bsg', x, w_group).astype(f32) + b_group.astype(f32)
    gprob = jax.nn.softmax(glog, axis=-1)
    gp, gidx = lax.top_k(gprob, 1)
    g_onehot = jax.nn.one_hot(gidx[..., 0], N_GROUPS, dtype=f32)
    elog = (jnp.einsum('bsd,de->bse', x, w_router).astype(f32) + b_router.astype(f32))
    elog = elog.reshape(B, S, N_GROUPS, EXPERTS_PER_GROUP)
    sel = jnp.einsum('bsge,bsg->bse', elog, g_onehot)
    ev, eidx = lax.top_k(sel, TOP_K)
    ew = jax.nn.softmax(ev, axis=-1)
    within = jnp.sum(ew[..., None] * jax.nn.one_hot(eidx, EXPERTS_PER_GROUP, dtype=f32), axis=2)
    comb = (gp[..., None] * g_onehot[..., None] * within[:, :, None, :])
    comb = comb.reshape(B, S, N_EXPERTS).astype(x.dtype)

    def per_seq(args):
        xs, cs = args
        hg = jnp.einsum('sd,edf->sef', xs, w_gate)
        hu = jnp.einsum('sd,edf->sef', xs, w_up)
        h = jax.nn.silu(hg) * hu * cs[:, :, None]
        return jnp.einsum('sef,efd->sd', h, w_down)

    return lax.map(per_seq, (x, comb))


def setup_inputs(seed: int = 0) -> dict:
    key = jax.random.key(seed)
    ks = jax.random.split(key, 32)
    f32 = jnp.float32
    L = DEPTH

    def nrm(k, shape, scale):
        return jax.random.normal(k, shape, f32) * scale

    def gain(k, shape):
        return 1.0 + 0.01 * jax.random.normal(k, shape, f32)

    return {
        "x_prompt": jax.random.normal(ks[0], (BATCH, SEQ, D_MODEL), f32),
        "x_sample": jax.random.normal(ks[1], (DEC_BATCH, DEC_SEQ, D_MODEL), f32),
        "norm1_g": gain(ks[2], (L, D_MODEL)),
        "w_in": nrm(ks[3], (L, D_MODEL, IN_WIDTH), D_MODEL ** -0.5),
        "w_out": nrm(ks[4], (L, MIX_WIDTH, D_MODEL), MIX_WIDTH ** -0.5),
        "a_q_norm": gain(ks[5], (L, HEAD_DIM)),
        "a_k_norm": gain(ks[6], (L, HEAD_DIM)),
        "a_lambda_q1": nrm(ks[7], (L, HEAD_DIM), 0.1),
        "a_lambda_k1": nrm(ks[8], (L, HEAD_DIM), 0.1),
        "a_lambda_q2": nrm(ks[9], (L, HEAD_DIM), 0.1),
        "a_lambda_k2": nrm(ks[10], (L, HEAD_DIM), 0.1),
        "a_subln": gain(ks[11], (L, A_VDIM)),
        "b_q_norm": gain(ks[12], (L, HEAD_DIM)),
        "b_k_norm": gain(ks[13], (L, HEAD_DIM)),
        "b_rpb": nrm(ks[14], (L, B_HEADS, 2 * NA_KH - 1, 2 * NA_KW - 1), 0.02),
        "c_q_norm": gain(ks[15], (L, HEAD_DIM)),
        "c_k_norm": gain(ks[16], (L, HEAD_DIM)),
        "c_sink": nrm(ks[17], (L, C_HEADS), 0.5),
        "norm2_g": gain(ks[18], (L, D_MODEL)),
        "w_group": nrm(ks[19], (L, D_MODEL, N_GROUPS), D_MODEL ** -0.5),
        "b_group": nrm(ks[20], (L, N_GROUPS), 0.01),
        "w_router": nrm(ks[21], (L, D_MODEL, N_EXPERTS), D_MODEL ** -0.5),
        "b_router": nrm(ks[22], (L, N_EXPERTS), 0.01),
        "w_gate": nrm(ks[23], (L, N_EXPERTS, D_MODEL, D_EXPERT), D_MODEL ** -0.5),
        "w_up": nrm(ks[24], (L, N_EXPERTS, D_MODEL, D_EXPERT), D_MODEL ** -0.5),
        "w_down": nrm(ks[25], (L, N_EXPERTS, D_EXPERT, D_MODEL), D_EXPERT ** -0.5),
    }


def reference(x_prompt, x_sample, norm1_g, w_in, w_out, a_q_norm, a_k_norm, a_lambda_q1,
              a_lambda_k1, a_lambda_q2, a_lambda_k2, a_subln, b_q_norm, b_k_norm, b_rpb,
              c_q_norm, c_k_norm, c_sink, norm2_g, w_group, b_group, w_router, b_router,
              w_gate, w_up, w_down):
    split_at = [int(v) for v in np.cumsum(IN_SPLITS)[:-1]]

    def run(x):
        cos, sin = rope_tables(x.shape[1], x.dtype)
        for l in range(DEPTH):
            lambda_init = 0.8 - 0.6 * math.exp(-0.3 * l)
            n = rmsnorm(x, norm1_g[l])
            proj = jnp.einsum('bsd,de->bse', n, w_in[l])
            aq, ak, av, bq, bk, bv, cq, ck, cv = jnp.split(proj, split_at, axis=-1)
            ya = diff_attention(aq, ak, av, a_q_norm[l], a_k_norm[l], a_lambda_q1[l], a_lambda_k1[l],
                                a_lambda_q2[l], a_lambda_k2[l], a_subln[l], lambda_init, cos, sin)
            yb = neighborhood_attention(bq, bk, bv, b_q_norm[l], b_k_norm[l], b_rpb[l])
            yc = window_gqa(cq, ck, cv, c_q_norm[l], c_k_norm[l], c_sink[l], cos, sin)
            mix = jnp.concatenate([ya, yb, yc], axis=-1)
            h = x + jnp.einsum('bse,ed->bsd', mix, w_out[l])
            x = h + hier_moe(rmsnorm(h, norm2_g[l]), w_group[l], b_group[l], w_router[l],
                             b_router[l], w_gate[l], w_up[l], w_down[l])
        return x

    y_prompt = run(x_prompt)
    y_sample = run(x_sample)
    return (y_prompt, y_sample)
```

```python
import functools
import math

import numpy as np
import jax
import jax.numpy as jnp
from jax import lax
from jax.experimental import pallas as pl
from jax.experimental.pallas import tpu as pltpu

F32 = jnp.float32
BF16 = jnp.bfloat16

D_MODEL = 1024
DEPTH = 2
HEAD_DIM = 64
GRID_W = 64
EPS = 1e-6
ROPE_THETA = 10000.0
A_HEADS = 4
A_VDIM = 2 * HEAD_DIM
B_HEADS = 4
NA_KH = 8
NA_KW = 16
C_HEADS = 4
C_KV_HEADS = 2
WINDOW = 128
N_GROUPS = 4
EXPERTS_PER_GROUP = 4
N_EXPERTS = 16
D_EXPERT = 512
QK_SCALE = HEAD_DIM ** -0.5

_COLS = dict(aq=(0, 512), ak=(512, 1024), av=(1024, 1536), bq=(1536, 1792), bk=(1792, 2048),
             bv=(2048, 2304), cq=(2304, 2560), ck=(2560, 2688), cv=(2688, 2816))
_PT_ORDER = ("aq", "av", "bq", "bv", "cq", "cv")
_KT_ORDER = ("ak", "bk", "ck")
PT_ROWS = 1920
KC_COLS = 896
IN_WIDTH = 2816

LANES = 128
NEG = -1e30

TM_PROJ = 512
TQ_A = 512
TK_A = 512
NA_Q = 512
NA_K = 1024
TQ_C = 512
TK_C = TQ_C + 2 * WINDOW
T_MOE = 1024
FAST_SOFTMAX_BOUND = 30.0


def _cparams(sem, vmem_mb):
    return pltpu.CompilerParams(dimension_semantics=sem, vmem_limit_bytes=vmem_mb * 1024 * 1024)


def _nt_dot(a, b):
    return lax.dot_general(a, b, (((1,), (1,)), ((), ())), preferred_element_type=F32)


def _dot(a, b):
    return jnp.dot(a, b, preferred_element_type=F32)


def _inproj_kernel(x_ref, g1_ref, w_ref, gains_ref, cos_ref, sin_ref, pt_ref, kc_ref):
    tm = x_ref.shape[1]
    x = x_ref[0]
    ms = jnp.mean(x * x, axis=-1, keepdims=True)
    n = (x * lax.rsqrt(ms + EPS) * g1_ref[...]).astype(BF16)
    proj_t = _nt_dot(w_ref[...], n)
    cos = cos_ref[...]
    sin = sin_ref[...]
    reps = tm // LANES

    def head(r0, gi, rope):
        xh = proj_t[r0:r0 + HEAD_DIM, :]
        ssq = jnp.sum(xh * xh, axis=0, keepdims=True)
        gain = pltpu.repeat(gains_ref[gi], reps, axis=1)
        xg = xh * lax.rsqrt(ssq * (1.0 / HEAD_DIM) + EPS) * gain
        if rope:
            swapped = jnp.concatenate([xg[HEAD_DIM // 2:], xg[:HEAD_DIM // 2]], axis=0)
            xg = xg * cos + swapped * sin
        return xg

    def put(r0, val):
        pt_ref[0, r0:r0 + val.shape[0], :] = val.astype(BF16)

    for j in range(8):
        put(j * 64, head(j * 64, 0, True))
    put(512, proj_t[512:1024, :])
    for j in range(4):
        put(1024 + j * 64, head(1024 + j * 64, 2, False))
    put(1280, proj_t[1280:1536, :])
    for j in range(4):
        put(1536 + j * 64, head(1536 + j * 64, 4, True))
    put(1792, proj_t[1792:1920, :])
    kbase = PT_ROWS
    specs = [(1, True)] * 4 + [(3, False)] * 2 + [(5, True)]
    for j, (gi, rope) in enumerate(specs):
        r0 = kbase + j * 128
        blk = jnp.concatenate([head(r0, gi, rope), head(r0 + 64, gi, rope)], axis=0)
        kc_ref[0, :, j * 128:(j + 1) * 128] = blk.T.astype(BF16)


def _inproj(x, g1, w_t, gains, cos_t, sin_t):
    B, S, _ = x.shape
    tm = TM_PROJ
    return pl.pallas_call(
        _inproj_kernel,
        grid=(B, S // tm),
        in_specs=[
            pl.BlockSpec((1, tm, D_MODEL), lambda b, i: (b, i, 0)),
            pl.BlockSpec((1, D_MODEL), lambda b, i: (0, 0)),
            pl.BlockSpec((IN_WIDTH, D_MODEL), lambda b, i: (0, 0)),
            pl.BlockSpec((6, HEAD_DIM, LANES), lambda b, i: (0, 0, 0)),
            pl.BlockSpec((HEAD_DIM, tm), lambda b, i: (0, i)),
            pl.BlockSpec((HEAD_DIM, tm), lambda b, i: (0, i)),
        ],
        out_specs=[
            pl.BlockSpec((1, PT_ROWS, tm), lambda b, i: (b, 0, i)),
            pl.BlockSpec((1, tm, KC_COLS), lambda b, i: (b, i, 0)),
        ],
        out_shape=[
            jax.ShapeDtypeStruct((B, PT_ROWS, S), BF16),
            jax.ShapeDtypeStruct((B, S, KC_COLS), BF16),
        ],
        compiler_params=_cparams(("parallel", "parallel"), 56),
        name="inproj",
    )(x, g1, w_t, gains, cos_t, sin_t)


def _diff_attn_kernel(q_ref, k_ref, v_ref, lam_ref, subg_ref, o_ref, qz_ref, acc_ref, l_ref, m_ref,
                      *, tk, safe, lambda_init):
    tq = q_ref.shape[2]
    S = k_ref.shape[1]
    q = q_ref[0]
    z = jnp.zeros((HEAD_DIM, tq), BF16)
    qz_ref[0] = jnp.concatenate([q[:HEAD_DIM], z], axis=0)
    qz_ref[1] = jnp.concatenate([z, q[HEAD_DIM:]], axis=0)
    acc_ref[...] = jnp.zeros_like(acc_ref)
    l_ref[...] = jnp.zeros_like(l_ref)
    if safe:
        m_ref[...] = jnp.full_like(m_ref, NEG)

    def body(i, carry):
        off = pl.multiple_of(i * tk, tk)
        kc = k_ref[0, pl.ds(off, tk), :]
        vc = v_ref[0, :, pl.ds(off, tk)]
        for c in range(2):
            s = _dot(kc, qz_ref[c])
            if safe:
                m_old = m_ref[c]
                m_new = jnp.maximum(m_old, jnp.max(s, axis=0, keepdims=True))
                alpha = jnp.exp(m_old - m_new)
                p = jnp.exp(s - m_new)
                m_ref[c] = m_new
                l_ref[c] = alpha * l_ref[c] + jnp.sum(p, axis=0, keepdims=True)
                acc_ref[c] = alpha * acc_ref[c] + _dot(vc, p.astype(BF16))
            else:
                p = jnp.exp(s)
                l_ref[c] = l_ref[c] + jnp.sum(p, axis=0, keepdims=True)
                acc_ref[c] = acc_ref[c] + _dot(vc, p.astype(BF16))
        return carry

    lax.fori_loop(0, S // tk, body, 0)

    lp = lam_ref[...]
    lam = (jnp.exp(jnp.sum(lp[0:1] * lp[1:2], axis=-1, keepdims=True))
           - jnp.exp(jnp.sum(lp[2:3] * lp[3:4], axis=-1, keepdims=True)) + lambda_init)
    o = acc_ref[0] / l_ref[0] - lam * (acc_ref[1] / l_ref[1])
    ms = jnp.mean(o * o, axis=0, keepdims=True)
    gain = pltpu.repeat(subg_ref[...], tq // LANES, axis=1)
    y = o * lax.rsqrt(ms + EPS) * gain * (1.0 - lambda_init)
    o_ref[0] = y.T.astype(BF16)


def _diff_attn(pt, kc, lam_params, subg, lambda_init, safe):
    B, _, S = pt.shape
    tq, tk = TQ_A, TK_A
    kern = functools.partial(_diff_attn_kernel, tk=tk, safe=safe, lambda_init=lambda_init)
    return pl.pallas_call(
        kern,
        grid=(B, A_HEADS, S // tq),
        in_specs=[
            pl.BlockSpec((1, 128, tq), lambda b, h, i: (b, h, i)),
            pl.BlockSpec((1, S, 128), lambda b, h, i: (b, 0, h)),
            pl.BlockSpec((1, 128, S), lambda b, h, i: (b, 4 + h, 0)),
            pl.BlockSpec((4, HEAD_DIM), lambda b, h, i: (0, 0)),
            pl.BlockSpec((A_VDIM, LANES), lambda b, h, i: (0, 0)),
        ],
        out_specs=pl.BlockSpec((1, tq, 128), lambda b, h, i: (b, i, h)),
        out_shape=jax.ShapeDtypeStruct((B, S, A_HEADS * A_VDIM), BF16),
        scratch_shapes=[
            pltpu.VMEM((2, 128, tq), BF16),
            pltpu.VMEM((2, A_VDIM, tq), F32),
            pltpu.VMEM((2, 1, tq), F32),
            pltpu.VMEM((2, 1, tq), F32),
        ],
        compiler_params=_cparams(("parallel", "parallel", "arbitrary"), 48),
        name="diff_attn_safe" if safe else "diff_attn",
    )(pt, kc, pt, lam_params, subg)


def _na_kernel(q_ref, k_ref, v_ref, bias_ref, o_ref):
    S = k_ref.shape[1]
    r = pl.program_id(2)
    start = pl.multiple_of(jnp.clip(r * NA_Q - 256, 0, S - NA_K), 256)
    ks = k_ref[0, pl.ds(start, NA_K), :]
    vs = v_ref[0, :, pl.ds(start, NA_K)]
    q = q_ref[0]
    z = jnp.zeros((HEAD_DIM, NA_Q), BF16)
    outs = []
    for hh in range(2):
        qh = q[hh * HEAD_DIM:(hh + 1) * HEAD_DIM]
        qz = jnp.concatenate([qh, z] if hh == 0 else [z, qh], axis=0)
        s = _dot(ks, qz) + bias_ref[hh, 0]
        m = jnp.max(s, axis=0, keepdims=True)
        p = jnp.exp(s - m)
        l = jnp.sum(p, axis=0, keepdims=True)
        outs.append(_dot(vs[hh * HEAD_DIM:(hh + 1) * HEAD_DIM], p.astype(BF16)) / l)
    o_ref[0] = jnp.concatenate(outs, axis=0).T.astype(BF16)


def _na_attn(pt, kc, bias):
    B, _, S = pt.shape
    nr = S // NA_Q

    def bias_map(b, p, r):
        return (p, jnp.where(r == 0, 0, jnp.where(r == nr - 1, 2, 1)), 0, 0)

    return pl.pallas_call(
        _na_kernel,
        grid=(B, 2, nr),
        in_specs=[
            pl.BlockSpec((1, 128, NA_Q), lambda b, p, r: (b, 8 + p, r)),
            pl.BlockSpec((1, S, 128), lambda b, p, r: (b, 0, 4 + p)),
            pl.BlockSpec((1, 128, S), lambda b, p, r: (b, 10 + p, 0)),
            pl.BlockSpec((2, 1, NA_K, NA_Q), bias_map),
        ],
        out_specs=pl.BlockSpec((1, NA_Q, 128), lambda b, p, r: (b, r, p)),
        out_shape=jax.ShapeDtypeStruct((B, S, B_HEADS * HEAD_DIM), BF16),
        compiler_params=_cparams(("parallel", "parallel", "arbitrary"), 48),
        name="na_attn",
    )(pt, kc, pt, bias)


def _na_bias_index():
    rows = 32
    a = np.arange(8)[None, :]
    i = np.arange(16)[:, None]
    ridx, rvalid = [], []
    for r0 in (0, 8, rows - 8):
        sb = np.clip(r0 - NA_KH // 2, 0, rows - 16)
        qrow = r0 + a
        krow = sb + i
        rs = np.clip(qrow - NA_KH // 2, 0, rows - NA_KH)
        rvalid.append((krow >= rs) & (krow < rs + NA_KH))
        ridx.append(np.clip(krow - qrow + NA_KH - 1, 0, 2 * NA_KH - 2))
    ridx = np.stack(ridx)
    rvalid = np.stack(rvalid)
    qc = np.arange(GRID_W)[None, :]
    kcol = np.arange(GRID_W)[:, None]
    cs = np.clip(qc - NA_KW // 2, 0, GRID_W - NA_KW)
    cvalid = (kcol >= cs) & (kcol < cs + NA_KW)
    cidx = np.clip(kcol - qc + NA_KW - 1, 0, 2 * NA_KW - 2)
    flat = ridx[:, :, None, :, None] * (2 * NA_KW - 1) + cidx[None, None, :, None, :]
    valid = rvalid[:, :, None, :, None] & cvalid[None, None, :, None, :]
    return flat.reshape(3, NA_K, NA_Q).astype(np.int32), valid.reshape(3, NA_K, NA_Q)


_NA_FLAT, _NA_VALID = _na_bias_index()


def _na_bias_table(rpb):
    flat = rpb.astype(F32).reshape(B_HEADS, -1)
    tab = jnp.take(flat, jnp.asarray(_NA_FLAT), axis=1)
    return jnp.where(jnp.asarray(_NA_VALID)[None], tab, NEG)


def _win_kernel(sink_ref, q_ref, k_ref, v_ref, o_ref):
    S = k_ref.shape[1]
    i = pl.program_id(1)
    q0 = i * TQ_C
    start = pl.multiple_of(jnp.clip(q0 - WINDOW, 0, S - TK_C), 128)
    ks = k_ref[0, pl.ds(start, TK_C), :]
    vs = v_ref[0, :, pl.ds(start, TK_C)]
    kpos = start + lax.broadcasted_iota(jnp.int32, (TK_C, TQ_C), 0)
    qpos = q0 + lax.broadcasted_iota(jnp.int32, (TK_C, TQ_C), 1)
    mask = jnp.where(jnp.abs(qpos - kpos) <= WINDOW, 0.0, NEG).astype(F32)
    z = jnp.zeros((HEAD_DIM, TQ_C), BF16)
    outs = []
    for kv in range(C_KV_HEADS):
        for g in range(C_HEADS // C_KV_HEADS):
            hq = kv * (C_HEADS // C_KV_HEADS) + g
            qh = q_ref[0, hq * HEAD_DIM:(hq + 1) * HEAD_DIM, :]
            qz = jnp.concatenate([qh, z] if kv == 0 else [z, qh], axis=0)
            s = _dot(ks, qz) + mask
            sk = sink_ref[hq]
            m = jnp.maximum(jnp.max(s, axis=0, keepdims=True), sk)
            e = jnp.exp(s - m)
            den = jnp.sum(e, axis=0, keepdims=True) + jnp.exp(sk - m)
            outs.append(_dot(vs[kv * HEAD_DIM:(kv + 1) * HEAD_DIM], e.astype(BF16)) / den)
    o_ref[0] = jnp.concatenate(outs, axis=0).T.astype(BF16)


def _win_attn(pt, kc, sink):
    B, _, S = pt.shape
    return pl.pallas_call(
        _win_kernel,
        grid=(B, S // TQ_C),
        in_specs=[
            pl.BlockSpec(memory_space=pltpu.SMEM),
            pl.BlockSpec((1, 256, TQ_C), lambda b, i: (b, 6, i)),
            pl.BlockSpec((1, S, 128), lambda b, i: (b, 0, 6)),
            pl.BlockSpec((1, 128, S), lambda b, i: (b, 14, 0)),
        ],
        out_specs=pl.BlockSpec((1, TQ_C, 256), lambda b, i: (b, i, 0)),
        out_shape=jax.ShapeDtypeStruct((B, S, C_HEADS * HEAD_DIM), BF16),
        compiler_params=_cparams(("parallel", "arbitrary"), 48),
        name="win_attn",
    )(sink, pt, kc, pt)


def _outproj_kernel(x_ref, ya_ref, yb_ref, yc_ref, wo_ref, g2_ref, wr_ref, br_ref,
                    h_ref, n2_ref, ri_ref):
    h = (x_ref[...]
         + _dot(ya_ref[...], wo_ref[0:512, :])
         + _dot(yb_ref[...], wo_ref[512:768, :])
         + _dot(yc_ref[...], wo_ref[768:1024, :]))
    h_ref[...] = h
    ms = jnp.mean(h * h, axis=-1, keepdims=True)
    n2 = h * lax.rsqrt(ms + EPS) * g2_ref[...]
    n2_ref[...] = n2.astype(BF16)
    logits = jnp.dot(n2, wr_ref[...], precision=lax.Precision.HIGHEST,
                     preferred_element_type=F32) + br_ref[...]
    lane = lax.broadcasted_iota(jnp.int32, logits.shape, 1).astype(F32)
    big = float(LANES)
    ninf = -jnp.inf

    def first_argmax(vals):
        vmax = jnp.max(vals, axis=-1, keepdims=True)
        idx = jnp.min(jnp.where(vals == vmax, lane, big), axis=-1, keepdims=True)
        return vmax, idx

    gl = jnp.where(lane < N_GROUPS, logits, ninf)
    gmax, gidx = first_argmax(gl)
    gp = 1.0 / jnp.sum(jnp.exp(gl - gmax), axis=-1, keepdims=True)
    lo = N_GROUPS + EXPERTS_PER_GROUP * gidx
    el = jnp.where((lane >= lo) & (lane < lo + EXPERTS_PER_GROUP), logits, ninf)
    v1, i1 = first_argmax(el)
    el2 = jnp.where(lane == i1, ninf, el)
    v2, i2 = first_argmax(el2)
    t = jnp.exp(v2 - v1)
    w1 = 1.0 / (1.0 + t)
    w2 = t / (1.0 + t)
    comb = jnp.where(lane == i1, gp * w1, jnp.where(lane == i2, gp * w2, 0.0))
    ri_ref[...] = comb


def _outproj(x2, ya, yb, yc, wo, g2, wr, br):
    T = x2.shape[0]
    tm = TM_PROJ
    row = lambda i: (i, 0)
    fixed = lambda i: (0, 0)
    return pl.pallas_call(
        _outproj_kernel,
        grid=(T // tm,),
        in_specs=[
            pl.BlockSpec((tm, D_MODEL), row),
            pl.BlockSpec((tm, 512), row),
            pl.BlockSpec((tm, 256), row),
            pl.BlockSpec((tm, 256), row),
            pl.BlockSpec((D_MODEL, D_MODEL), fixed),
            pl.BlockSpec((1, D_MODEL), fixed),
            pl.BlockSpec((D_MODEL, LANES), fixed),
            pl.BlockSpec((1, LANES), fixed),
        ],
        out_specs=[
            pl.BlockSpec((tm, D_MODEL), row),
            pl.BlockSpec((tm, D_MODEL), row),
            pl.BlockSpec((tm, LANES), row),
        ],
        out_shape=[
            jax.ShapeDtypeStruct((T, D_MODEL), F32),
            jax.ShapeDtypeStruct((T, D_MODEL), BF16),
            jax.ShapeDtypeStruct((T, LANES), F32),
        ],
        compiler_params=_cparams(("parallel",), 48),
        name="outproj",
    )(x2, ya, yb, yc, wo, g2, wr, br)


def _moe_kernel(x_ref, h_ref, ri_ref, wg_ref, wu_ref, wd_ref, o_ref):
    e = pl.program_id(1)

    @pl.when(e == 0)
    def _():
        o_ref[...] = h_ref[...]

    x = x_ref[...]
    hg = _dot(x, wg_ref[0])
    hu = _dot(x, wu_ref[0])
    ri = ri_ref[...]
    lane = lax.broadcasted_iota(jnp.int32, ri.shape, 1)
    w = jnp.sum(jnp.where(lane == N_GROUPS + e, ri, 0.0), axis=-1, keepdims=True)
    hh = (hg * jax.nn.sigmoid(hg) * hu * w).astype(BF16)
    o_ref[...] += _dot(hh, wd_ref[0])


def _moe(n2, h, ri, wg, wu, wd):
    T = n2.shape[0]
    tm = T_MOE
    row = lambda i, e: (i, 0)
    return pl.pallas_call(
        _moe_kernel,
        grid=(T // tm, N_EXPERTS),
        in_specs=[
            pl.BlockSpec((tm, D_MODEL), row),
            pl.BlockSpec((tm, D_MODEL), row),
            pl.BlockSpec((tm, LANES), row),
            pl.BlockSpec((1, D_MODEL, D_EXPERT), lambda i, e: (e, 0, 0)),
            pl.BlockSpec((1, D_MODEL, D_EXPERT), lambda i, e: (e, 0, 0)),
            pl.BlockSpec((1, D_EXPERT, D_MODEL), lambda i, e: (e, 0, 0)),
        ],
        out_specs=pl.BlockSpec((tm, D_MODEL), row),
        out_shape=jax.ShapeDtypeStruct((T, D_MODEL), F32),
        compiler_params=_cparams(("parallel", "arbitrary"), 56),
        name="moe",
    )(n2, h, ri, wg, wu, wd)


def _rope_tables_t(seq_len):
    pos = jnp.arange(seq_len, dtype=F32)
    inv = ROPE_THETA ** (-jnp.arange(0, HEAD_DIM, 2, dtype=F32) / HEAD_DIM)
    ang = pos[:, None] * inv[None, :]
    ang = jnp.concatenate([ang, ang], axis=-1)
    sign = jnp.where(jnp.arange(HEAD_DIM) < HEAD_DIM // 2, -1.0, 1.0).astype(F32)
    return jnp.cos(ang).T, (jnp.sin(ang) * sign[None, :]).T


def _lane_bcast(v):
    return jnp.broadcast_to(v.astype(F32)[:, None], (v.shape[0], LANES))


def _prep_layer(l, p):
    w_in = p["w_in"][l]
    cols = [w_in[:, _COLS[n][0]:_COLS[n][1]] for n in _PT_ORDER + _KT_ORDER]
    w_t = jnp.concatenate(cols, axis=1).T.astype(BF16)
    gains = jnp.stack([
        _lane_bcast(p["a_q_norm"][l] * QK_SCALE), _lane_bcast(p["a_k_norm"][l]),
        _lane_bcast(p["b_q_norm"][l] * QK_SCALE), _lane_bcast(p["b_k_norm"][l]),
        _lane_bcast(p["c_q_norm"][l] * QK_SCALE), _lane_bcast(p["c_k_norm"][l]),
    ])
    wr = jnp.zeros((D_MODEL, LANES), F32)
    wr = wr.at[:, :N_GROUPS].set(p["w_group"][l]).at[:, N_GROUPS:N_GROUPS + N_EXPERTS].set(p["w_router"][l])
    br = jnp.zeros((1, LANES), F32)
    br = br.at[0, :N_GROUPS].set(p["b_group"][l]).at[0, N_GROUPS:N_GROUPS + N_EXPERTS].set(p["b_router"][l])
    bound = 1.02 * HEAD_DIM * QK_SCALE * jnp.max(jnp.abs(p["a_q_norm"][l])) * jnp.max(jnp.abs(p["a_k_norm"][l]))
    return dict(
        g1=p["norm1_g"][l].astype(F32)[None, :], w_t=w_t, gains=gains,
        lam=jnp.stack([p["a_lambda_q1"][l], p["a_lambda_k1"][l],
                       p["a_lambda_q2"][l], p["a_lambda_k2"][l]]).astype(F32),
        subg=_lane_bcast(p["a_subln"][l]),
        na_bias=_na_bias_table(p["b_rpb"][l]),
        sink=p["c_sink"][l].astype(F32),
        wo=p["w_out"][l].astype(BF16), g2=p["norm2_g"][l].astype(F32)[None, :],
        wr=wr, br=br,
        wg=p["w_gate"][l].astype(BF16), wu=p["w_up"][l].astype(BF16), wd=p["w_down"][l].astype(BF16),
        fast_ok=bound <= FAST_SOFTMAX_BOUND,
    )


def _layer(x, lp, lambda_init, cos_t, sin_t):
    B, S, _ = x.shape
    pt, kc = _inproj(x, lp["g1"], lp["w_t"], lp["gains"], cos_t, sin_t)
    ya = lax.cond(
        lp["fast_ok"],
        lambda a, b: _diff_attn(a, b, lp["lam"], lp["subg"], lambda_init, False),
        lambda a, b: _diff_attn(a, b, lp["lam"], lp["subg"], lambda_init, True),
        pt, kc)
    yb = _na_attn(pt, kc, lp["na_bias"])
    yc = _win_attn(pt, kc, lp["sink"])
    T = B * S
    h, n2, ri = _outproj(x.reshape(T, D_MODEL), ya.reshape(T, -1), yb.reshape(T, -1), yc.reshape(T, -1),
                         lp["wo"], lp["g2"], lp["wr"], lp["br"])
    y = _moe(n2, h, ri, lp["wg"], lp["wu"], lp["wd"])
    return y.reshape(B, S, D_MODEL)


def kernel(x_prompt, x_sample, norm1_g, w_in, w_out, a_q_norm, a_k_norm, a_lambda_q1, a_lambda_k1,
           a_lambda_q2, a_lambda_k2, a_subln, b_q_norm, b_k_norm, b_rpb, c_q_norm, c_k_norm, c_sink,
           norm2_g, w_group, b_group, w_router, b_router, w_gate, w_up, w_down):
    p = dict(norm1_g=norm1_g, w_in=w_in, w_out=w_out, a_q_norm=a_q_norm, a_k_norm=a_k_norm,
             a_lambda_q1=a_lambda_q1, a_lambda_k1=a_lambda_k1, a_lambda_q2=a_lambda_q2,
             a_lambda_k2=a_lambda_k2, a_subln=a_subln, b_q_norm=b_q_norm, b_k_norm=b_k_norm,
             b_rpb=b_rpb, c_q_norm=c_q_norm, c_k_norm=c_k_norm, c_sink=c_sink, norm2_g=norm2_g,
             w_group=w_group, b_group=b_group, w_router=w_router, b_router=b_router,
             w_gate=w_gate, w_up=w_up, w_down=w_down)
    layers = [_prep_layer(l, p) for l in range(DEPTH)]

    def run(x):
        cos_t, sin_t = _rope_tables_t(x.shape[1])
        for l in range(DEPTH):
            lambda_init = 0.8 - 0.6 * math.exp(-0.3 * l)
            x = _layer(x, layers[l], lambda_init, cos_t, sin_t)
        return x

    return (run(x_prompt), run(x_sample))
```

```python
import functools
import math

import numpy as np
import jax
import jax.numpy as jnp
from jax import lax
from jax.experimental import pallas as pl
from jax.experimental.pallas import tpu as pltpu

F32 = jnp.float32
BF16 = jnp.bfloat16

D_MODEL = 1024
DEPTH = 2
HEAD_DIM = 64
GRID_W = 64
EPS = 1e-6
ROPE_THETA = 10000.0
A_HEADS = 4
A_VDIM = 2 * HEAD_DIM
B_HEADS = 4
NA_KH = 8
NA_KW = 16
C_HEADS = 4
C_KV_HEADS = 2
WINDOW = 128
N_GROUPS = 4
EXPERTS_PER_GROUP = 4
N_EXPERTS = 16
D_EXPERT = 512
QK_SCALE = HEAD_DIM ** -0.5
LOG2E = math.log2(math.e)

_COLS = dict(aq=(0, 512), ak=(512, 1024), av=(1024, 1536), bq=(1536, 1792), bk=(1792, 2048),
             bv=(2048, 2304), cq=(2304, 2560), ck=(2560, 2688), cv=(2688, 2816))
_PT_ORDER = ("aq", "av", "bq", "bv", "cq", "cv")
_KT_ORDER = ("ak", "bk", "ck")
PT_ROWS = 1920
KC_COLS = 896
IN_WIDTH = 2816

LANES = 128
NEG = -1e30

TM_PROJ = 512
TQ_A = 512
TK_A = 2048
UNROLL_A = 1
NA_Q = 512
NA_K = 1024
TQ_C = 512
TK_C = TQ_C + 2 * WINDOW
T_MOE = 1024
FAST_SOFTMAX_BOUND = 30.0


def _cparams(sem, vmem_mb):
    return pltpu.CompilerParams(dimension_semantics=sem, vmem_limit_bytes=vmem_mb * 1024 * 1024)


def _nt_dot(a, b):
    return lax.dot_general(a, b, (((1,), (1,)), ((), ())), preferred_element_type=F32)


def _dot(a, b):
    return jnp.dot(a, b, preferred_element_type=F32)


def _inproj_kernel(x_ref, g1_ref, w_ref, gains_ref, cos_ref, sin_ref, pt_ref, kc_ref):
    tm = x_ref.shape[1]
    x = x_ref[0]
    ms = jnp.mean(x * x, axis=-1, keepdims=True)
    n = (x * lax.rsqrt(ms + EPS) * g1_ref[...]).astype(BF16)
    proj_t = _nt_dot(w_ref[...], n)
    cos = cos_ref[...]
    sin = sin_ref[...]
    reps = tm // LANES

    def head(r0, gi, rope):
        xh = proj_t[r0:r0 + HEAD_DIM, :]
        ssq = jnp.sum(xh * xh, axis=0, keepdims=True)
        gain = pltpu.repeat(gains_ref[gi], reps, axis=1)
        xg = xh * lax.rsqrt(ssq * (1.0 / HEAD_DIM) + EPS) * gain
        if rope:
            swapped = jnp.concatenate([xg[HEAD_DIM // 2:], xg[:HEAD_DIM // 2]], axis=0)
            xg = xg * cos + swapped * sin
        return xg

    def put(r0, val):
        pt_ref[0, r0:r0 + val.shape[0], :] = val.astype(BF16)

    for j in range(8):
        put(j * 64, head(j * 64, 0, True))
    put(512, proj_t[512:1024, :])
    for j in range(4):
        put(1024 + j * 64, head(1024 + j * 64, 2, False))
    put(1280, proj_t[1280:1536, :])
    for j in range(4):
        put(1536 + j * 64, head(1536 + j * 64, 4, True))
    put(1792, proj_t[1792:1920, :])
    kbase = PT_ROWS
    specs = [(1, True)] * 4 + [(3, False)] * 2 + [(5, True)]
    for j, (gi, rope) in enumerate(specs):
        r0 = kbase + j * 128
        blk = jnp.concatenate([head(r0, gi, rope), head(r0 + 64, gi, rope)], axis=0)
        kc_ref[0, :, j * 128:(j + 1) * 128] = blk.T.astype(BF16)


def _inproj(x, g1, w_t, gains, cos_t, sin_t):
    B, S, _ = x.shape
    tm = TM_PROJ
    return pl.pallas_call(
        _inproj_kernel,
        grid=(B, S // tm),
        in_specs=[
            pl.BlockSpec((1, tm, D_MODEL), lambda b, i: (b, i, 0)),
            pl.BlockSpec((1, D_MODEL), lambda b, i: (0, 0)),
            pl.BlockSpec((IN_WIDTH, D_MODEL), lambda b, i: (0, 0)),
            pl.BlockSpec((6, HEAD_DIM, LANES), lambda b, i: (0, 0, 0)),
            pl.BlockSpec((HEAD_DIM, tm), lambda b, i: (0, i)),
            pl.BlockSpec((HEAD_DIM, tm), lambda b, i: (0, i)),
        ],
        out_specs=[
            pl.BlockSpec((1, PT_ROWS, tm), lambda b, i: (b, 0, i)),
            pl.BlockSpec((1, tm, KC_COLS), lambda b, i: (b, i, 0)),
        ],
        out_shape=[
            jax.ShapeDtypeStruct((B, PT_ROWS, S), BF16),
            jax.ShapeDtypeStruct((B, S, KC_COLS), BF16),
        ],
        compiler_params=_cparams(("parallel", "parallel"), 56),
        name="inproj",
    )(x, g1, w_t, gains, cos_t, sin_t)


def _diff_attn_kernel(q_ref, k_ref, v_ref, lam_ref, subg_ref, o_ref, qz_ref, acc_ref, l_ref, m_ref,
                      *, tk, safe, lambda_init):
    tq = q_ref.shape[2]
    S = k_ref.shape[1]
    q = q_ref[0]
    z = jnp.zeros((HEAD_DIM, tq), BF16)
    qz_ref[:, :tq] = jnp.concatenate([q[:HEAD_DIM], z], axis=0)
    qz_ref[:, tq:] = jnp.concatenate([z, q[HEAD_DIM:]], axis=0)
    acc_ref[...] = jnp.zeros_like(acc_ref)
    l_ref[...] = jnp.zeros_like(l_ref)
    if safe:
        m_ref[...] = jnp.full_like(m_ref, NEG)

    def body(i, carry):
        off = pl.multiple_of(i * tk, tk)
        kc = k_ref[0, pl.ds(off, tk), :]
        vc = v_ref[0, :, pl.ds(off, tk)]
        s = _dot(kc, qz_ref[...])
        if safe:
            m_old = m_ref[...]
            m_new = jnp.maximum(m_old, jnp.max(s, axis=0, keepdims=True))
            alpha = jnp.exp2(m_old - m_new)
            p = jnp.exp2(s - m_new)
            m_ref[...] = m_new
            l_ref[...] = alpha * l_ref[...] + jnp.sum(p, axis=0, keepdims=True)
            acc_ref[...] = alpha * acc_ref[...] + _dot(vc, p.astype(BF16))
        else:
            p = jnp.exp2(s)
            l_ref[...] = l_ref[...] + jnp.sum(p, axis=0, keepdims=True)
            acc_ref[...] = acc_ref[...] + _dot(vc, p.astype(BF16))
        return carry

    lax.fori_loop(0, S // tk, body, 0, unroll=UNROLL_A)

    lp = lam_ref[...]
    lam = (jnp.exp(jnp.sum(lp[0:1] * lp[1:2], axis=-1, keepdims=True))
           - jnp.exp(jnp.sum(lp[2:3] * lp[3:4], axis=-1, keepdims=True)) + lambda_init)
    on = acc_ref[...] / l_ref[...]
    o = on[:, :tq] - lam * on[:, tq:]
    ms = jnp.mean(o * o, axis=0, keepdims=True)
    gain = pltpu.repeat(subg_ref[...], tq // LANES, axis=1)
    y = o * lax.rsqrt(ms + EPS) * gain * (1.0 - lambda_init)
    o_ref[0] = y.T.astype(BF16)


def _diff_attn(pt, kc, lam_params, subg, lambda_init, safe):
    B, _, S = pt.shape
    tq, tk = TQ_A, min(TK_A, S)
    kern = functools.partial(_diff_attn_kernel, tk=tk, safe=safe, lambda_init=lambda_init)
    return pl.pallas_call(
        kern,
        grid=(B, A_HEADS, S // tq),
        in_specs=[
            pl.BlockSpec((1, 128, tq), lambda b, h, i: (b, h, i)),
            pl.BlockSpec((1, S, 128), lambda b, h, i: (b, 0, h)),
            pl.BlockSpec((1, 128, S), lambda b, h, i: (b, 4 + h, 0)),
            pl.BlockSpec((4, HEAD_DIM), lambda b, h, i: (0, 0)),
            pl.BlockSpec((A_VDIM, LANES), lambda b, h, i: (0, 0)),
        ],
        out_specs=pl.BlockSpec((1, tq, 128), lambda b, h, i: (b, i, h)),
        out_shape=jax.ShapeDtypeStruct((B, S, A_HEADS * A_VDIM), BF16),
        scratch_shapes=[
            pltpu.VMEM((128, 2 * tq), BF16),
            pltpu.VMEM((A_VDIM, 2 * tq), F32),
            pltpu.VMEM((1, 2 * tq), F32),
            pltpu.VMEM((1, 2 * tq), F32),
        ],
        compiler_params=_cparams(("parallel", "parallel", "arbitrary"), 48),
        name="diff_attn_safe" if safe else "diff_attn",
    )(pt, kc, pt, lam_params, subg)


def _na_kernel(q_ref, k_ref, v_ref, bias_ref, o_ref):
    S = k_ref.shape[1]
    r = pl.program_id(2)
    start = pl.multiple_of(jnp.clip(r * NA_Q - 256, 0, S - NA_K), 256)
    ks = k_ref[0, pl.ds(start, NA_K), :]
    vs = v_ref[0, :, pl.ds(start, NA_K)]
    q = q_ref[0]
    z = jnp.zeros((HEAD_DIM, NA_Q), BF16)
    outs = []
    for hh in range(2):
        qh = q[hh * HEAD_DIM:(hh + 1) * HEAD_DIM]
        qz = jnp.concatenate([qh, z] if hh == 0 else [z, qh], axis=0)
        s = _dot(ks, qz) + bias_ref[hh, 0]
        m = jnp.max(s, axis=0, keepdims=True)
        p = jnp.exp(s - m)
        l = jnp.sum(p, axis=0, keepdims=True)
        outs.append(_dot(vs[hh * HEAD_DIM:(hh + 1) * HEAD_DIM], p.astype(BF16)) / l)
    o_ref[0] = jnp.concatenate(outs, axis=0).T.astype(BF16)


def _na_attn(pt, kc, bias):
    B, _, S = pt.shape
    nr = S // NA_Q

    def bias_map(b, p, r):
        return (p, jnp.where(r == 0, 0, jnp.where(r == nr - 1, 2, 1)), 0, 0)

    return pl.pallas_call(
        _na_kernel,
        grid=(B, 2, nr),
        in_specs=[
            pl.BlockSpec((1, 128, NA_Q), lambda b, p, r: (b, 8 + p, r)),
            pl.BlockSpec((1, S, 128), lambda b, p, r: (b, 0, 4 + p)),
            pl.BlockSpec((1, 128, S), lambda b, p, r: (b, 10 + p, 0)),
            pl.BlockSpec((2, 1, NA_K, NA_Q), bias_map),
        ],
        out_specs=pl.BlockSpec((1, NA_Q, 128), lambda b, p, r: (b, r, p)),
        out_shape=jax.ShapeDtypeStruct((B, S, B_HEADS * HEAD_DIM), BF16),
        compiler_params=_cparams(("parallel", "parallel", "arbitrary"), 48),
        name="na_attn",
    )(pt, kc, pt, bias)


def _na_bias_index():
    rows = 32
    a = np.arange(8)[None, :]
    i = np.arange(16)[:, None]
    ridx, rvalid = [], []
    for r0 in (0, 8, rows - 8):
        sb = np.clip(r0 - NA_KH // 2, 0, rows - 16)
        qrow = r0 + a
        krow = sb + i
        rs = np.clip(qrow - NA_KH // 2, 0, rows - NA_KH)
        rvalid.append((krow >= rs) & (krow < rs + NA_KH))
        ridx.append(np.clip(krow - qrow + NA_KH - 1, 0, 2 * NA_KH - 2))
    ridx = np.stack(ridx)
    rvalid = np.stack(rvalid)
    qc = np.arange(GRID_W)[None, :]
    kcol = np.arange(GRID_W)[:, None]
    cs = np.clip(qc - NA_KW // 2, 0, GRID_W - NA_KW)
    cvalid = (kcol >= cs) & (kcol < cs + NA_KW)
    cidx = np.clip(kcol - qc + NA_KW - 1, 0, 2 * NA_KW - 2)
    valid = rvalid[:, :, None, :, None] & cvalid[None, None, :, None, :]
    col_onehot = (cidx.reshape(-1)[None, :] == np.arange(2 * NA_KW - 1)[:, None])
    return ridx.astype(np.int32), col_onehot.astype(np.float32), valid.reshape(3, NA_K, NA_Q)


_NA_RIDX, _NA_COL_ONEHOT, _NA_VALID = _na_bias_index()


def _na_bias_table(rpb):
    rows = jnp.take(rpb.astype(F32), jnp.asarray(_NA_RIDX), axis=1)
    tab = jnp.einsum("hvkqc,cn->hvkqn", rows, jnp.asarray(_NA_COL_ONEHOT),
                     precision=lax.Precision.HIGHEST)
    tab = tab.reshape(B_HEADS, 3, 16, 8, GRID_W, GRID_W).transpose(0, 1, 2, 4, 3, 5)
    tab = tab.reshape(B_HEADS, 3, NA_K, NA_Q)
    return jnp.where(jnp.asarray(_NA_VALID)[None], tab, NEG)


def _win_kernel(sink_ref, q_ref, k_ref, v_ref, o_ref):
    S = k_ref.shape[1]
    i = pl.program_id(1)
    q0 = i * TQ_C
    start = pl.multiple_of(jnp.clip(q0 - WINDOW, 0, S - TK_C), 128)
    ks = k_ref[0, pl.ds(start, TK_C), :]
    vs = v_ref[0, :, pl.ds(start, TK_C)]
    kpos = start + lax.broadcasted_iota(jnp.int32, (TK_C, TQ_C), 0)
    qpos = q0 + lax.broadcasted_iota(jnp.int32, (TK_C, TQ_C), 1)
    mask = jnp.where(jnp.abs(qpos - kpos) <= WINDOW, 0.0, NEG).astype(F32)
    z = jnp.zeros((HEAD_DIM, TQ_C), BF16)
    outs = []
    for kv in range(C_KV_HEADS):
        for g in range(C_HEADS // C_KV_HEADS):
            hq = kv * (C_HEADS // C_KV_HEADS) + g
            qh = q_ref[0, hq * HEAD_DIM:(hq + 1) * HEAD_DIM, :]
            qz = jnp.concatenate([qh, z] if kv == 0 else [z, qh], axis=0)
            s = _dot(ks, qz) + mask
            sk = sink_ref[hq]
            m = jnp.maximum(jnp.max(s, axis=0, keepdims=True), sk)
            e = jnp.exp(s - m)
            den = jnp.sum(e, axis=0, keepdims=True) + jnp.exp(sk - m)
            outs.append(_dot(vs[kv * HEAD_DIM:(kv + 1) * HEAD_DIM], e.astype(BF16)) / den)
    o_ref[0] = jnp.concatenate(outs, axis=0).T.astype(BF16)


def _win_attn(pt, kc, sink):
    B, _, S = pt.shape
    return pl.pallas_call(
        _win_kernel,
        grid=(B, S // TQ_C),
        in_specs=[
            pl.BlockSpec(memory_space=pltpu.SMEM),
            pl.BlockSpec((1, 256, TQ_C), lambda b, i: (b, 6, i)),
            pl.BlockSpec((1, S, 128), lambda b, i: (b, 0, 6)),
            pl.BlockSpec((1, 128, S), lambda b, i: (b, 14, 0)),
        ],
        out_specs=pl.BlockSpec((1, TQ_C, 256), lambda b, i: (b, i, 0)),
        out_shape=jax.ShapeDtypeStruct((B, S, C_HEADS * HEAD_DIM), BF16),
        compiler_params=_cparams(("parallel", "arbitrary"), 48),
        name="win_attn",
    )(sink, pt, kc, pt)


def _outproj_kernel(x_ref, ya_ref, yb_ref, yc_ref, wo_ref, g2_ref, wr_ref, br_ref,
                    h_ref, n2_ref, ri_ref):
    h = (x_ref[...]
         + _dot(ya_ref[...], wo_ref[0:512, :])
         + _dot(yb_ref[...], wo_ref[512:768, :])
         + _dot(yc_ref[...], wo_ref[768:1024, :]))
    h_ref[...] = h
    ms = jnp.mean(h * h, axis=-1, keepdims=True)
    n2 = h * lax.rsqrt(ms + EPS) * g2_ref[...]
    n2_ref[...] = n2.astype(BF16)
    logits = jnp.dot(n2, wr_ref[...], precision=lax.Precision.HIGHEST,
                     preferred_element_type=F32) + br_ref[...]
    lane = lax.broadcasted_iota(jnp.int32, logits.shape, 1).astype(F32)
    big = float(LANES)
    ninf = -jnp.inf

    def first_argmax(vals):
        vmax = jnp.max(vals, axis=-1, keepdims=True)
        idx = jnp.min(jnp.where(vals == vmax, lane, big), axis=-1, keepdims=True)
        return vmax, idx

    gl = jnp.where(lane < N_GROUPS, logits, ninf)
    gmax, gidx = first_argmax(gl)
    gp = 1.0 / jnp.sum(jnp.exp(gl - gmax), axis=-1, keepdims=True)
    lo = N_GROUPS + EXPERTS_PER_GROUP * gidx
    el = jnp.where((lane >= lo) & (lane < lo + EXPERTS_PER_GROUP), logits, ninf)
    v1, i1 = first_argmax(el)
    el2 = jnp.where(lane == i1, ninf, el)
    v2, i2 = first_argmax(el2)
    t = jnp.exp(v2 - v1)
    w1 = 1.0 / (1.0 + t)
    w2 = t / (1.0 + t)
    comb = jnp.where(lane == i1, gp * w1, jnp.where(lane == i2, gp * w2, 0.0))
    ri_ref[...] = comb


def _outproj(x2, ya, yb, yc, wo, g2, wr, br):
    T = x2.shape[0]
    tm = TM_PROJ
    row = lambda i: (i, 0)
    fixed = lambda i: (0, 0)
    return pl.pallas_call(
        _outproj_kernel,
        grid=(T // tm,),
        in_specs=[
            pl.BlockSpec((tm, D_MODEL), row),
            pl.BlockSpec((tm, 512), row),
            pl.BlockSpec((tm, 256), row),
            pl.BlockSpec((tm, 256), row),
            pl.BlockSpec((D_MODEL, D_MODEL), fixed),
            pl.BlockSpec((1, D_MODEL), fixed),
            pl.BlockSpec((D_MODEL, LANES), fixed),
            pl.BlockSpec((1, LANES), fixed),
        ],
        out_specs=[
            pl.BlockSpec((tm, D_MODEL), row),
            pl.BlockSpec((tm, D_MODEL), row),
            pl.BlockSpec((tm, LANES), row),
        ],
        out_shape=[
            jax.ShapeDtypeStruct((T, D_MODEL), F32),
            jax.ShapeDtypeStruct((T, D_MODEL), BF16),
            jax.ShapeDtypeStruct((T, LANES), F32),
        ],
        compiler_params=_cparams(("parallel",), 48),
        name="outproj",
    )(x2, ya, yb, yc, wo, g2, wr, br)


def _moe_kernel(x_ref, h_ref, ri_ref, wg_ref, wu_ref, wd_ref, o_ref):
    e = pl.program_id(1)

    @pl.when(e == 0)
    def _():
        o_ref[...] = h_ref[...]

    x = x_ref[...]
    hg = _dot(x, wg_ref[0])
    hu = _dot(x, wu_ref[0])
    ri = ri_ref[...]
    lane = lax.broadcasted_iota(jnp.int32, ri.shape, 1)
    w = jnp.sum(jnp.where(lane == N_GROUPS + e, ri, 0.0), axis=-1, keepdims=True)
    hh = (hg * jax.nn.sigmoid(hg) * hu * w).astype(BF16)
    o_ref[...] += _dot(hh, wd_ref[0])


def _moe(n2, h, ri, wg, wu, wd):
    T = n2.shape[0]
    tm = T_MOE
    row = lambda i, e: (i, 0)
    return pl.pallas_call(
        _moe_kernel,
        grid=(T // tm, N_EXPERTS),
        in_specs=[
            pl.BlockSpec((tm, D_MODEL), row),
            pl.BlockSpec((tm, D_MODEL), row),
            pl.BlockSpec((tm, LANES), row),
            pl.BlockSpec((1, D_MODEL, D_EXPERT), lambda i, e: (e, 0, 0)),
            pl.BlockSpec((1, D_MODEL, D_EXPERT), lambda i, e: (e, 0, 0)),
            pl.BlockSpec((1, D_EXPERT, D_MODEL), lambda i, e: (e, 0, 0)),
        ],
        out_specs=pl.BlockSpec((tm, D_MODEL), row),
        out_shape=jax.ShapeDtypeStruct((T, D_MODEL), F32),
        compiler_params=_cparams(("parallel", "arbitrary"), 56),
        name="moe",
    )(n2, h, ri, wg, wu, wd)


def _rope_tables_t(seq_len):
    pos = jnp.arange(seq_len, dtype=F32)
    inv = ROPE_THETA ** (-jnp.arange(0, HEAD_DIM, 2, dtype=F32) / HEAD_DIM)
    ang = pos[:, None] * inv[None, :]
    ang = jnp.concatenate([ang, ang], axis=-1)
    sign = jnp.where(jnp.arange(HEAD_DIM) < HEAD_DIM // 2, -1.0, 1.0).astype(F32)
    return jnp.cos(ang).T, (jnp.sin(ang) * sign[None, :]).T


def _lane_bcast(v):
    return jnp.broadcast_to(v.astype(F32)[:, None], (v.shape[0], LANES))


def _prep_layer(l, p):
    w_in = p["w_in"][l]
    cols = [w_in[:, _COLS[n][0]:_COLS[n][1]] for n in _PT_ORDER + _KT_ORDER]
    w_t = jnp.concatenate(cols, axis=1).T.astype(BF16)
    gains = jnp.stack([
        _lane_bcast(p["a_q_norm"][l] * (QK_SCALE * LOG2E)), _lane_bcast(p["a_k_norm"][l]),
        _lane_bcast(p["b_q_norm"][l] * QK_SCALE), _lane_bcast(p["b_k_norm"][l]),
        _lane_bcast(p["c_q_norm"][l] * QK_SCALE), _lane_bcast(p["c_k_norm"][l]),
    ])
    wr = jnp.zeros((D_MODEL, LANES), F32)
    wr = wr.at[:, :N_GROUPS].set(p["w_group"][l]).at[:, N_GROUPS:N_GROUPS + N_EXPERTS].set(p["w_router"][l])
    br = jnp.zeros((1, LANES), F32)
    br = br.at[0, :N_GROUPS].set(p["b_group"][l]).at[0, N_GROUPS:N_GROUPS + N_EXPERTS].set(p["b_router"][l])
    bound = 1.02 * HEAD_DIM * QK_SCALE * jnp.max(jnp.abs(p["a_q_norm"][l])) * jnp.max(jnp.abs(p["a_k_norm"][l]))
    return dict(
        g1=p["norm1_g"][l].astype(F32)[None, :], w_t=w_t, gains=gains,
        lam=jnp.stack([p["a_lambda_q1"][l], p["a_lambda_k1"][l],
                       p["a_lambda_q2"][l], p["a_lambda_k2"][l]]).astype(F32),
        subg=_lane_bcast(p["a_subln"][l]),
        na_bias=_na_bias_table(p["b_rpb"][l]),
        sink=p["c_sink"][l].astype(F32),
        wo=p["w_out"][l].astype(BF16), g2=p["norm2_g"][l].astype(F32)[None, :],
        wr=wr, br=br,
        wg=p["w_gate"][l].astype(BF16), wu=p["w_up"][l].astype(BF16), wd=p["w_down"][l].astype(BF16),
        fast_ok=bound <= FAST_SOFTMAX_BOUND,
    )


def _layer(x, lp, lambda_init, cos_t, sin_t):
    B, S, _ = x.shape
    pt, kc = _inproj(x, lp["g1"], lp["w_t"], lp["gains"], cos_t, sin_t)
    ya = lax.cond(
        lp["fast_ok"],
        lambda a, b: _diff_attn(a, b, lp["lam"], lp["subg"], lambda_init, False),
        lambda a, b: _diff_attn(a, b, lp["lam"], lp["subg"], lambda_init, True),
        pt, kc)
    yb = _na_attn(pt, kc, lp["na_bias"])
    yc = _win_attn(pt, kc, lp["sink"])
    T = B * S
    h, n2, ri = _outproj(x.reshape(T, D_MODEL), ya.reshape(T, -1), yb.reshape(T, -1), yc.reshape(T, -1),
                         lp["wo"], lp["g2"], lp["wr"], lp["br"])
    y = _moe(n2, h, ri, lp["wg"], lp["wu"], lp["wd"])
    return y.reshape(B, S, D_MODEL)


def kernel(x_prompt, x_sample, norm1_g, w_in, w_out, a_q_norm, a_k_norm, a_lambda_q1, a_lambda_k1,
           a_lambda_q2, a_lambda_k2, a_subln, b_q_norm, b_k_norm, b_rpb, c_q_norm, c_k_norm, c_sink,
           norm2_g, w_group, b_group, w_router, b_router, w_gate, w_up, w_down):
    p = dict(norm1_g=norm1_g, w_in=w_in, w_out=w_out, a_q_norm=a_q_norm, a_k_norm=a_k_norm,
             a_lambda_q1=a_lambda_q1, a_lambda_k1=a_lambda_k1, a_lambda_q2=a_lambda_q2,
             a_lambda_k2=a_lambda_k2, a_subln=a_subln, b_q_norm=b_q_norm, b_k_norm=b_k_norm,
             b_rpb=b_rpb, c_q_norm=c_q_norm, c_k_norm=c_k_norm, c_sink=c_sink, norm2_g=norm2_g,
             w_group=w_group, b_group=b_group, w_router=w_router, b_router=b_router,
             w_gate=w_gate, w_up=w_up, w_down=w_down)
    layers = [_prep_layer(l, p) for l in range(DEPTH)]

    def run(x):
        cos_t, sin_t = _rope_tables_t(x.shape[1])
        for l in range(DEPTH):
            lambda_init = 0.8 - 0.6 * math.exp(-0.3 * l)
            x = _layer(x, layers[l], lambda_init, cos_t, sin_t)
        return x

    return (run(x_prompt), run(x_sample))
```

```python
import functools
import math

import numpy as np
import jax
import jax.numpy as jnp
from jax import lax
from jax.experimental import pallas as pl
from jax.experimental.pallas import tpu as pltpu

F32 = jnp.float32
BF16 = jnp.bfloat16

D_MODEL = 1024
DEPTH = 2
HEAD_DIM = 64
GRID_W = 64
EPS = 1e-6
ROPE_THETA = 10000.0
A_HEADS = 4
A_VDIM = 2 * HEAD_DIM
B_HEADS = 4
NA_KH = 8
NA_KW = 16
C_HEADS = 4
C_KV_HEADS = 2
WINDOW = 128
N_GROUPS = 4
EXPERTS_PER_GROUP = 4
N_EXPERTS = 16
D_EXPERT = 512
QK_SCALE = HEAD_DIM ** -0.5
LOG2E = math.log2(math.e)

_COLS = dict(aq=(0, 512), ak=(512, 1024), av=(1024, 1536), bq=(1536, 1792), bk=(1792, 2048),
             bv=(2048, 2304), cq=(2304, 2560), ck=(2560, 2688), cv=(2688, 2816))
_PT_ORDER = ("aq", "av", "bq", "bv", "cq", "cv")
_KT_ORDER = ("ak", "bk", "ck")
PT_ROWS = 1920
KC_COLS = 896
IN_WIDTH = 2816

LANES = 128
NEG = -1e30

TM_PROJ = 512
TQ_A = 512
TK_A = 2048
UNROLL_A = 1
NA_Q = 512
NA_K = 1024
TQ_C = 512
TK_C = TQ_C + 2 * WINDOW
T_MOE = 1024
MOE_SLAB = 320
FAST_SOFTMAX_BOUND = 30.0


def _cparams(sem, vmem_mb):
    return pltpu.CompilerParams(dimension_semantics=sem, vmem_limit_bytes=vmem_mb * 1024 * 1024)


def _nt_dot(a, b):
    return lax.dot_general(a, b, (((1,), (1,)), ((), ())), preferred_element_type=F32)


def _dot(a, b):
    return jnp.dot(a, b, preferred_element_type=F32)


def _inproj_kernel(x_ref, g1_ref, w_ref, gains_ref, cos_ref, sin_ref, pt_ref, kc_ref):
    tm = x_ref.shape[1]
    x = x_ref[0]
    ms = jnp.mean(x * x, axis=-1, keepdims=True)
    n = (x * lax.rsqrt(ms + EPS) * g1_ref[...]).astype(BF16)
    proj_t = _nt_dot(w_ref[...], n)
    cos = cos_ref[...]
    sin = sin_ref[...]
    reps = tm // LANES

    def head(r0, gi, rope):
        xh = proj_t[r0:r0 + HEAD_DIM, :]
        ssq = jnp.sum(xh * xh, axis=0, keepdims=True)
        gain = pltpu.repeat(gains_ref[gi], reps, axis=1)
        xg = xh * lax.rsqrt(ssq * (1.0 / HEAD_DIM) + EPS) * gain
        if rope:
            swapped = jnp.concatenate([xg[HEAD_DIM // 2:], xg[:HEAD_DIM // 2]], axis=0)
            xg = xg * cos + swapped * sin
        return xg

    def put(r0, val):
        pt_ref[0, r0:r0 + val.shape[0], :] = val.astype(BF16)

    for j in range(8):
        put(j * 64, head(j * 64, 0, True))
    put(512, proj_t[512:1024, :])
    for j in range(4):
        put(1024 + j * 64, head(1024 + j * 64, 2, False))
    put(1280, proj_t[1280:1536, :])
    for j in range(4):
        put(1536 + j * 64, head(1536 + j * 64, 4, True))
    put(1792, proj_t[1792:1920, :])
    kbase = PT_ROWS
    specs = [(1, True)] * 4 + [(3, False)] * 2 + [(5, True)]
    for j, (gi, rope) in enumerate(specs):
        r0 = kbase + j * 128
        blk = jnp.concatenate([head(r0, gi, rope), head(r0 + 64, gi, rope)], axis=0)
        kc_ref[0, :, j * 128:(j + 1) * 128] = blk.T.astype(BF16)


def _inproj(x, g1, w_t, gains, cos_t, sin_t):
    B, S, _ = x.shape
    tm = TM_PROJ
    return pl.pallas_call(
        _inproj_kernel,
        grid=(B, S // tm),
        in_specs=[
            pl.BlockSpec((1, tm, D_MODEL), lambda b, i: (b, i, 0)),
            pl.BlockSpec((1, D_MODEL), lambda b, i: (0, 0)),
            pl.BlockSpec((IN_WIDTH, D_MODEL), lambda b, i: (0, 0)),
            pl.BlockSpec((6, HEAD_DIM, LANES), lambda b, i: (0, 0, 0)),
            pl.BlockSpec((HEAD_DIM, tm), lambda b, i: (0, i)),
            pl.BlockSpec((HEAD_DIM, tm), lambda b, i: (0, i)),
        ],
        out_specs=[
            pl.BlockSpec((1, PT_ROWS, tm), lambda b, i: (b, 0, i)),
            pl.BlockSpec((1, tm, KC_COLS), lambda b, i: (b, i, 0)),
        ],
        out_shape=[
            jax.ShapeDtypeStruct((B, PT_ROWS, S), BF16),
            jax.ShapeDtypeStruct((B, S, KC_COLS), BF16),
        ],
        compiler_params=_cparams(("parallel", "parallel"), 56),
        name="inproj",
    )(x, g1, w_t, gains, cos_t, sin_t)


def _diff_attn_kernel(q_ref, k_ref, v_ref, lam_ref, subg_ref, o_ref, qz_ref, acc_ref, l_ref, m_ref,
                      *, tk, safe, lambda_init):
    tq = q_ref.shape[2]
    S = k_ref.shape[1]
    q = q_ref[0]
    z = jnp.zeros((HEAD_DIM, tq), BF16)
    qz_ref[:, :tq] = jnp.concatenate([q[:HEAD_DIM], z], axis=0)
    qz_ref[:, tq:] = jnp.concatenate([z, q[HEAD_DIM:]], axis=0)
    acc_ref[...] = jnp.zeros_like(acc_ref)
    l_ref[...] = jnp.zeros_like(l_ref)
    if safe:
        m_ref[...] = jnp.full_like(m_ref, NEG)

    def body(i, carry):
        off = pl.multiple_of(i * tk, tk)
        kc = k_ref[0, pl.ds(off, tk), :]
        vc = v_ref[0, :, pl.ds(off, tk)]
        s = _dot(kc, qz_ref[...])
        if safe:
            m_old = m_ref[...]
            m_new = jnp.maximum(m_old, jnp.max(s, axis=0, keepdims=True))
            alpha = jnp.exp2(m_old - m_new)
            p = jnp.exp2(s - m_new)
            m_ref[...] = m_new
            l_ref[...] = alpha * l_ref[...] + jnp.sum(p, axis=0, keepdims=True)
            acc_ref[...] = alpha * acc_ref[...] + _dot(vc, p.astype(BF16))
        else:
            p = jnp.exp2(s)
            l_ref[...] = l_ref[...] + jnp.sum(p, axis=0, keepdims=True)
            acc_ref[...] = acc_ref[...] + _dot(vc, p.astype(BF16))
        return carry

    lax.fori_loop(0, S // tk, body, 0, unroll=UNROLL_A)

    lp = lam_ref[...]
    lam = (jnp.exp(jnp.sum(lp[0:1] * lp[1:2], axis=-1, keepdims=True))
           - jnp.exp(jnp.sum(lp[2:3] * lp[3:4], axis=-1, keepdims=True)) + lambda_init)
    on = acc_ref[...] / l_ref[...]
    o = on[:, :tq] - lam * on[:, tq:]
    ms = jnp.mean(o * o, axis=0, keepdims=True)
    gain = pltpu.repeat(subg_ref[...], tq // LANES, axis=1)
    y = o * lax.rsqrt(ms + EPS) * gain * (1.0 - lambda_init)
    o_ref[0] = y.T.astype(BF16)


def _diff_attn(pt, kc, lam_params, subg, lambda_init, safe):
    B, _, S = pt.shape
    tq, tk = TQ_A, min(TK_A, S)
    kern = functools.partial(_diff_attn_kernel, tk=tk, safe=safe, lambda_init=lambda_init)
    return pl.pallas_call(
        kern,
        grid=(B, A_HEADS, S // tq),
        in_specs=[
            pl.BlockSpec((1, 128, tq), lambda b, h, i: (b, h, i)),
            pl.BlockSpec((1, S, 128), lambda b, h, i: (b, 0, h)),
            pl.BlockSpec((1, 128, S), lambda b, h, i: (b, 4 + h, 0)),
            pl.BlockSpec((4, HEAD_DIM), lambda b, h, i: (0, 0)),
            pl.BlockSpec((A_VDIM, LANES), lambda b, h, i: (0, 0)),
        ],
        out_specs=pl.BlockSpec((1, tq, 128), lambda b, h, i: (b, i, h)),
        out_shape=jax.ShapeDtypeStruct((B, S, A_HEADS * A_VDIM), BF16),
        scratch_shapes=[
            pltpu.VMEM((128, 2 * tq), BF16),
            pltpu.VMEM((A_VDIM, 2 * tq), F32),
            pltpu.VMEM((1, 2 * tq), F32),
            pltpu.VMEM((1, 2 * tq), F32),
        ],
        compiler_params=_cparams(("parallel", "parallel", "arbitrary"), 48),
        name="diff_attn_safe" if safe else "diff_attn",
    )(pt, kc, pt, lam_params, subg)


def _na_kernel(q_ref, k_ref, v_ref, bias_ref, o_ref):
    S = k_ref.shape[1]
    r = pl.program_id(2)
    start = pl.multiple_of(jnp.clip(r * NA_Q - 256, 0, S - NA_K), 256)
    ks = k_ref[0, pl.ds(start, NA_K), :]
    vs = v_ref[0, :, pl.ds(start, NA_K)]
    q = q_ref[0]
    z = jnp.zeros((HEAD_DIM, NA_Q), BF16)
    outs = []
    for hh in range(2):
        qh = q[hh * HEAD_DIM:(hh + 1) * HEAD_DIM]
        qz = jnp.concatenate([qh, z] if hh == 0 else [z, qh], axis=0)
        s = _dot(ks, qz) + bias_ref[hh, 0]
        m = jnp.max(s, axis=0, keepdims=True)
        p = jnp.exp(s - m)
        l = jnp.sum(p, axis=0, keepdims=True)
        outs.append(_dot(vs[hh * HEAD_DIM:(hh + 1) * HEAD_DIM], p.astype(BF16)) / l)
    o_ref[0] = jnp.concatenate(outs, axis=0).T.astype(BF16)


def _na_attn(pt, kc, bias):
    B, _, S = pt.shape
    nr = S // NA_Q

    def bias_map(b, p, r):
        return (p, jnp.where(r == 0, 0, jnp.where(r == nr - 1, 2, 1)), 0, 0)

    return pl.pallas_call(
        _na_kernel,
        grid=(B, 2, nr),
        in_specs=[
            pl.BlockSpec((1, 128, NA_Q), lambda b, p, r: (b, 8 + p, r)),
            pl.BlockSpec((1, S, 128), lambda b, p, r: (b, 0, 4 + p)),
            pl.BlockSpec((1, 128, S), lambda b, p, r: (b, 10 + p, 0)),
            pl.BlockSpec((2, 1, NA_K, NA_Q), bias_map),
        ],
        out_specs=pl.BlockSpec((1, NA_Q, 128), lambda b, p, r: (b, r, p)),
        out_shape=jax.ShapeDtypeStruct((B, S, B_HEADS * HEAD_DIM), BF16),
        compiler_params=_cparams(("parallel", "parallel", "arbitrary"), 48),
        name="na_attn",
    )(pt, kc, pt, bias)


def _na_bias_index():
    rows = 32
    a = np.arange(8)[None, :]
    i = np.arange(16)[:, None]
    ridx, rvalid = [], []
    for r0 in (0, 8, rows - 8):
        sb = np.clip(r0 - NA_KH // 2, 0, rows - 16)
        qrow = r0 + a
        krow = sb + i
        rs = np.clip(qrow - NA_KH // 2, 0, rows - NA_KH)
        rvalid.append((krow >= rs) & (krow < rs + NA_KH))
        ridx.append(np.clip(krow - qrow + NA_KH - 1, 0, 2 * NA_KH - 2))
    ridx = np.stack(ridx)
    rvalid = np.stack(rvalid)
    qc = np.arange(GRID_W)[None, :]
    kcol = np.arange(GRID_W)[:, None]
    cs = np.clip(qc - NA_KW // 2, 0, GRID_W - NA_KW)
    cvalid = (kcol >= cs) & (kcol < cs + NA_KW)
    cidx = np.clip(kcol - qc + NA_KW - 1, 0, 2 * NA_KW - 2)
    valid = rvalid[:, :, None, :, None] & cvalid[None, None, :, None, :]
    col_onehot = (cidx.reshape(-1)[None, :] == np.arange(2 * NA_KW - 1)[:, None])
    return ridx.astype(np.int32), col_onehot.astype(np.float32), valid.reshape(3, NA_K, NA_Q)


_NA_RIDX, _NA_COL_ONEHOT, _NA_VALID = _na_bias_index()


def _na_bias_table(rpb):
    rows = jnp.take(rpb.astype(F32), jnp.asarray(_NA_RIDX), axis=1)
    tab = jnp.einsum("hvkqc,cn->hvkqn", rows, jnp.asarray(_NA_COL_ONEHOT),
                     precision=lax.Precision.HIGHEST)
    tab = tab.reshape(B_HEADS, 3, 16, 8, GRID_W, GRID_W).transpose(0, 1, 2, 4, 3, 5)
    tab = tab.reshape(B_HEADS, 3, NA_K, NA_Q)
    return jnp.where(jnp.asarray(_NA_VALID)[None], tab, NEG)


def _win_kernel(sink_ref, q_ref, k_ref, v_ref, o_ref):
    S = k_ref.shape[1]
    i = pl.program_id(1)
    q0 = i * TQ_C
    start = pl.multiple_of(jnp.clip(q0 - WINDOW, 0, S - TK_C), 128)
    ks = k_ref[0, pl.ds(start, TK_C), :]
    vs = v_ref[0, :, pl.ds(start, TK_C)]
    kpos = start + lax.broadcasted_iota(jnp.int32, (TK_C, TQ_C), 0)
    qpos = q0 + lax.broadcasted_iota(jnp.int32, (TK_C, TQ_C), 1)
    mask = jnp.where(jnp.abs(qpos - kpos) <= WINDOW, 0.0, NEG).astype(F32)
    z = jnp.zeros((HEAD_DIM, TQ_C), BF16)
    outs = []
    for kv in range(C_KV_HEADS):
        for g in range(C_HEADS // C_KV_HEADS):
            hq = kv * (C_HEADS // C_KV_HEADS) + g
            qh = q_ref[0, hq * HEAD_DIM:(hq + 1) * HEAD_DIM, :]
            qz = jnp.concatenate([qh, z] if kv == 0 else [z, qh], axis=0)
            s = _dot(ks, qz) + mask
            sk = sink_ref[hq]
            m = jnp.maximum(jnp.max(s, axis=0, keepdims=True), sk)
            e = jnp.exp(s - m)
            den = jnp.sum(e, axis=0, keepdims=True) + jnp.exp(sk - m)
            outs.append(_dot(vs[kv * HEAD_DIM:(kv + 1) * HEAD_DIM], e.astype(BF16)) / den)
    o_ref[0] = jnp.concatenate(outs, axis=0).T.astype(BF16)


def _win_attn(pt, kc, sink):
    B, _, S = pt.shape
    return pl.pallas_call(
        _win_kernel,
        grid=(B, S // TQ_C),
        in_specs=[
            pl.BlockSpec(memory_space=pltpu.SMEM),
            pl.BlockSpec((1, 256, TQ_C), lambda b, i: (b, 6, i)),
            pl.BlockSpec((1, S, 128), lambda b, i: (b, 0, 6)),
            pl.BlockSpec((1, 128, S), lambda b, i: (b, 14, 0)),
        ],
        out_specs=pl.BlockSpec((1, TQ_C, 256), lambda b, i: (b, i, 0)),
        out_shape=jax.ShapeDtypeStruct((B, S, C_HEADS * HEAD_DIM), BF16),
        compiler_params=_cparams(("parallel", "arbitrary"), 48),
        name="win_attn",
    )(sink, pt, kc, pt)


def _outproj_kernel(x_ref, ya_ref, yb_ref, yc_ref, wo_ref, g2_ref, wr_ref, br_ref, tri_ref,
                    h_ref, n2_ref, ri_ref, rs_ref, pr_ref, cnt_ref):
    tm = x_ref.shape[0]
    h = (x_ref[...]
         + _dot(ya_ref[...], wo_ref[0:512, :])
         + _dot(yb_ref[...], wo_ref[512:768, :])
         + _dot(yc_ref[...], wo_ref[768:1024, :]))
    h_ref[...] = h
    ms = jnp.mean(h * h, axis=-1, keepdims=True)
    n2 = h * lax.rsqrt(ms + EPS) * g2_ref[...]
    n2_hi = n2.astype(BF16)
    n2_ref[...] = n2_hi
    n2_lo = (n2 - n2_hi.astype(F32)).astype(BF16)
    both = _dot(n2_hi, wr_ref[...])
    logits = both[:, :LANES] + both[:, LANES:] + _dot(n2_lo, wr_ref[:, :LANES]) + br_ref[...]
    lane = lax.broadcasted_iota(jnp.int32, logits.shape, 1).astype(F32)
    big = float(LANES)
    ninf = -jnp.inf

    def first_argmax(vals):
        vmax = jnp.max(vals, axis=-1, keepdims=True)
        idx = jnp.min(jnp.where(vals == vmax, lane, big), axis=-1, keepdims=True)
        return vmax, idx

    gl = jnp.where(lane < N_GROUPS, logits, ninf)
    gmax, gidx = first_argmax(gl)
    gp = 1.0 / jnp.sum(jnp.exp(gl - gmax), axis=-1, keepdims=True)
    lo = N_GROUPS + EXPERTS_PER_GROUP * gidx
    el = jnp.where((lane >= lo) & (lane < lo + EXPERTS_PER_GROUP), logits, ninf)
    v1, i1 = first_argmax(el)
    el2 = jnp.where(lane == i1, ninf, el)
    v2, i2 = first_argmax(el2)
    t = jnp.exp(v2 - v1)
    w1 = 1.0 / (1.0 + t)
    w2 = t / (1.0 + t)
    comb = jnp.where(lane == i1, gp * w1, jnp.where(lane == i2, gp * w2, 0.0))

    onehot = jnp.where(lane == gidx, 1.0, 0.0)
    before = _dot(tri_ref[...], onehot.astype(BF16))
    cnt = jnp.sum(onehot, axis=0, keepdims=True)
    pos = jnp.sum(jnp.where(lane < gidx, cnt, 0.0) + jnp.where(lane == gidx, before, 0.0),
                  axis=-1, keepdims=True)
    ri = jnp.where(lane == 0.0, pos, comb)
    ri_ref[...] = ri
    hi = ri.astype(BF16)
    r1 = ri - hi.astype(F32)
    mid = r1.astype(BF16)
    low = (r1 - mid.astype(F32)).astype(BF16)
    rs_ref[...] = jnp.concatenate([hi, mid, low], axis=1)
    pos_t = jnp.broadcast_to(pos, (tm, LANES)).T
    pr_ref[0] = pos_t[0:8, :]
    cnt_ref[0] = jnp.broadcast_to(cnt, (8, LANES)).astype(jnp.int32)


def _outproj(x2, ya, yb, yc, wo, g2, wr, br, tri):
    T = x2.shape[0]
    tm = T_MOE
    nt = T // tm
    row = lambda i: (i, 0)
    fixed = lambda i: (0, 0)
    return pl.pallas_call(
        _outproj_kernel,
        grid=(nt,),
        in_specs=[
            pl.BlockSpec((tm, D_MODEL), row),
            pl.BlockSpec((tm, 512), row),
            pl.BlockSpec((tm, 256), row),
            pl.BlockSpec((tm, 256), row),
            pl.BlockSpec((D_MODEL, D_MODEL), fixed),
            pl.BlockSpec((1, D_MODEL), fixed),
            pl.BlockSpec((D_MODEL, 2 * LANES), fixed),
            pl.BlockSpec((1, LANES), fixed),
            pl.BlockSpec((tm, tm), fixed),
        ],
        out_specs=[
            pl.BlockSpec((tm, D_MODEL), row),
            pl.BlockSpec((tm, D_MODEL), row),
            pl.BlockSpec((tm, LANES), row),
            pl.BlockSpec((tm, 3 * LANES), row),
            pl.BlockSpec((1, 8, tm), lambda i: (i, 0, 0)),
            pl.BlockSpec((1, 8, LANES), lambda i: (i, 0, 0)),
        ],
        out_shape=[
            jax.ShapeDtypeStruct((T, D_MODEL), F32),
            jax.ShapeDtypeStruct((T, D_MODEL), BF16),
            jax.ShapeDtypeStruct((T, LANES), F32),
            jax.ShapeDtypeStruct((T, 3 * LANES), BF16),
            jax.ShapeDtypeStruct((nt, 8, tm), F32),
            jax.ShapeDtypeStruct((nt, 8, LANES), jnp.int32),
        ],
        compiler_params=_cparams(("parallel",), 56),
        name="outproj",
    )(x2, ya, yb, yc, wo, g2, wr, br, tri)


def _moe_kernel(cnt_ref, x_ref, h_ref, ri_ref, rs_ref, pr_ref, wg_ref, wu_ref, wd_ref, o_ref,
                xs_ref, cs_ref, os_ref):
    tm = x_ref.shape[0]
    i = pl.program_id(0)
    e = pl.program_id(1)
    g = lax.shift_right_logical(e, 2)
    n =[cnt_ref[i * N_GROUPS + k] for k in range(N_GROUPS)]
    off = (jnp.where(g > 0, n[0], 0) + jnp.where(g > 1, n[1], 0) + jnp.where(g > 2, n[2], 0))
    ng = jnp.where(g == 0, n[0], jnp.where(g == 1, n[1], jnp.where(g == 2, n[2], n[3])))
    start = lax.shift_left(lax.shift_right_logical(off, 4), 4)
    nslab = lax.div(off - start + ng + (MOE_SLAB - 1), MOE_SLAB)

    @pl.when(e == 0)
    def _():
        want = lax.broadcasted_iota(jnp.int32, (tm, tm), 0).astype(F32)
        perm = jnp.where(pr_ref[0, 0:1, :] == want, 1.0, 0.0).astype(BF16)
        xs_ref[0:tm, :] = _dot(perm, x_ref[...]).astype(BF16)
        c3 = _dot(perm, rs_ref[...])
        cs_ref[0:tm, :] = c3[:, :LANES] + c3[:, LANES:2 * LANES] + c3[:, 2 * LANES:]
        xs_ref[tm:, :] = jnp.zeros((MOE_SLAB, D_MODEL), BF16)
        cs_ref[tm:, :] = jnp.zeros((MOE_SLAB, LANES), F32)
        os_ref[...] = jnp.zeros_like(os_ref)

    def ffn(s, carry):
        r0 = pl.multiple_of(start + s * MOE_SLAB, 16)
        xs = xs_ref[pl.ds(r0, MOE_SLAB), :]
        hg = _dot(xs, wg_ref[0])
        hu = _dot(xs, wu_ref[0])
        cs = cs_ref[pl.ds(r0, MOE_SLAB), :]
        lane = lax.broadcasted_iota(jnp.int32, cs.shape, 1)
        w = jnp.sum(jnp.where(lane == N_GROUPS + e, cs, 0.0), axis=-1, keepdims=True)
        hh = (hg * jax.nn.sigmoid(hg) * hu * w).astype(BF16)
        os_ref[pl.ds(r0, MOE_SLAB), :] = os_ref[pl.ds(r0, MOE_SLAB), :] + _dot(hh, wd_ref[0])
        return carry
    lax.fori_loop(0, nslab, ffn, 0)

    @pl.when(e == N_EXPERTS - 1)
    def _():
        want = lax.broadcasted_iota(jnp.int32, (tm, tm), 1).astype(F32)
        perm_t = jnp.where(ri_ref[:, 0:1] == want, 1.0, 0.0).astype(BF16)
        o_ref[...] = h_ref[...] + _dot(perm_t, os_ref[0:tm, :].astype(BF16))


def _moe(n2, h, ri, rs, pr, cnt, wg, wu, wd):
    T = n2.shape[0]
    tm = T_MOE
    rows = tm + MOE_SLAB
    row = lambda i, e, c: (i, 0)
    wmap = lambda i, e, c: (e, 0, 0)
    return pl.pallas_call(
        _moe_kernel,
        grid_spec=pltpu.PrefetchScalarGridSpec(
            num_scalar_prefetch=1,
            grid=(T // tm, N_EXPERTS),
            in_specs=[
                pl.BlockSpec((tm, D_MODEL), row),
                pl.BlockSpec((tm, D_MODEL), row),
                pl.BlockSpec((tm, LANES), row),
                pl.BlockSpec((tm, 3 * LANES), row),
                pl.BlockSpec((1, 8, tm), lambda i, e, c: (i, 0, 0)),
                pl.BlockSpec((1, D_MODEL, D_EXPERT), wmap),
                pl.BlockSpec((1, D_MODEL, D_EXPERT), wmap),
                pl.BlockSpec((1, D_EXPERT, D_MODEL), wmap),
            ],
            out_specs=pl.BlockSpec((tm, D_MODEL), row),
            scratch_shapes=[
                pltpu.VMEM((rows, D_MODEL), BF16),
                pltpu.VMEM((rows, LANES), F32),
                pltpu.VMEM((rows, D_MODEL), F32),
            ],
        ),
        out_shape=jax.ShapeDtypeStruct((T, D_MODEL), F32),
        compiler_params=_cparams(("parallel", "arbitrary"), 56),
        name="moe",
    )(cnt, n2, h, ri, rs, pr, wg, wu, wd)


def _rope_tables_t(seq_len):
    pos = jnp.arange(seq_len, dtype=F32)
    inv = ROPE_THETA ** (-jnp.arange(0, HEAD_DIM, 2, dtype=F32) / HEAD_DIM)
    ang = pos[:, None] * inv[None, :]
    ang = jnp.concatenate([ang, ang], axis=-1)
    sign = jnp.where(jnp.arange(HEAD_DIM) < HEAD_DIM // 2, -1.0, 1.0).astype(F32)
    return jnp.cos(ang).T, (jnp.sin(ang) * sign[None, :]).T


def _lane_bcast(v):
    return jnp.broadcast_to(v.astype(F32)[:, None], (v.shape[0], LANES))


def _prep_layer(l, p):
    w_in = p["w_in"][l]
    cols = [w_in[:, _COLS[n][0]:_COLS[n][1]] for n in _PT_ORDER + _KT_ORDER]
    w_t = jnp.concatenate(cols, axis=1).T.astype(BF16)
    gains = jnp.stack([
        _lane_bcast(p["a_q_norm"][l] * (QK_SCALE * LOG2E)), _lane_bcast(p["a_k_norm"][l]),
        _lane_bcast(p["b_q_norm"][l] * QK_SCALE), _lane_bcast(p["b_k_norm"][l]),
        _lane_bcast(p["c_q_norm"][l] * QK_SCALE), _lane_bcast(p["c_k_norm"][l]),
    ])
    wr = jnp.zeros((D_MODEL, LANES), F32)
    wr = wr.at[:, :N_GROUPS].set(p["w_group"][l]).at[:, N_GROUPS:N_GROUPS + N_EXPERTS].set(p["w_router"][l])
    wr_hi = wr.astype(BF16)
    wr_lo = (wr - wr_hi.astype(F32)).astype(BF16)
    wr2 = jnp.concatenate([wr_hi, wr_lo], axis=1)
    br = jnp.zeros((1, LANES), F32)
    br = br.at[0, :N_GROUPS].set(p["b_group"][l]).at[0, N_GROUPS:N_GROUPS + N_EXPERTS].set(p["b_router"][l])
    bound = 1.02 * HEAD_DIM * QK_SCALE * jnp.max(jnp.abs(p["a_q_norm"][l])) * jnp.max(jnp.abs(p["a_k_norm"][l]))
    return dict(
        g1=p["norm1_g"][l].astype(F32)[None, :], w_t=w_t, gains=gains,
        lam=jnp.stack([p["a_lambda_q1"][l], p["a_lambda_k1"][l],
                       p["a_lambda_q2"][l], p["a_lambda_k2"][l]]).astype(F32),
        subg=_lane_bcast(p["a_subln"][l]),
        na_bias=_na_bias_table(p["b_rpb"][l]),
        sink=p["c_sink"][l].astype(F32),
        wo=p["w_out"][l].astype(BF16), g2=p["norm2_g"][l].astype(F32)[None, :],
        wr=wr2, br=br,
        wg=p["w_gate"][l].astype(BF16), wu=p["w_up"][l].astype(BF16), wd=p["w_down"][l].astype(BF16),
        fast_ok=bound <= FAST_SOFTMAX_BOUND,
    )


def _layer(x, lp, lambda_init, cos_t, sin_t):
    B, S, _ = x.shape
    pt, kc = _inproj(x, lp["g1"], lp["w_t"], lp["gains"], cos_t, sin_t)
    ya = lax.cond(
        lp["fast_ok"],
        lambda a, b: _diff_attn(a, b, lp["lam"], lp["subg"], lambda_init, False),
        lambda a, b: _diff_attn(a, b, lp["lam"], lp["subg"], lambda_init, True),
        pt, kc)
    yb = _na_attn(pt, kc, lp["na_bias"])
    yc = _win_attn(pt, kc, lp["sink"])
    T = B * S
    tri = jnp.tril(jnp.ones((T_MOE, T_MOE), BF16), -1)
    h, n2, ri, rs, pr, cnt = _outproj(x.reshape(T, D_MODEL), ya.reshape(T, -1), yb.reshape(T, -1),
                                      yc.reshape(T, -1), lp["wo"], lp["g2"], lp["wr"], lp["br"], tri)
    counts = cnt[:, 0, :N_GROUPS].reshape(-1)
    y = _moe(n2, h, ri, rs, pr, counts, lp["wg"], lp["wu"], lp["wd"])
    return y.reshape(B, S, D_MODEL)


def kernel(x_prompt, x_sample, norm1_g, w_in, w_out, a_q_norm, a_k_norm, a_lambda_q1, a_lambda_k1,
           a_lambda_q2, a_lambda_k2, a_subln, b_q_norm, b_k_norm, b_rpb, c_q_norm, c_k_norm, c_sink,
           norm2_g, w_group, b_group, w_router, b_router, w_gate, w_up, w_down):
    p = dict(norm1_g=norm1_g, w_in=w_in, w_out=w_out, a_q_norm=a_q_norm, a_k_norm=a_k_norm,
             a_lambda_q1=a_lambda_q1, a_lambda_k1=a_lambda_k1, a_lambda_q2=a_lambda_q2,
             a_lambda_k2=a_lambda_k2, a_subln=a_subln, b_q_norm=b_q_norm, b_k_norm=b_k_norm,
             b_rpb=b_rpb, c_q_norm=c_q_norm, c_k_norm=c_k_norm, c_sink=c_sink, norm2_g=norm2_g,
             w_group=w_group, b_group=b_group, w_router=w_router, b_router=b_router,
             w_gate=w_gate, w_up=w_up, w_down=w_down)
    layers = [_prep_layer(l, p) for l in range(DEPTH)]

    def run(x):
        cos_t, sin_t = _rope_tables_t(x.shape[1])
        for l in range(DEPTH):
            lambda_init = 0.8 - 0.6 * math.exp(-0.3 * l)
            x = _layer(x, layers[l], lambda_init, cos_t, sin_t)
        return x

    return (run(x_prompt), run(x_sample))
```

```python
import functools
import math

import numpy as np
import jax
import jax.numpy as jnp
from jax import lax
from jax.experimental import pallas as pl
from jax.experimental.pallas import tpu as pltpu

F32 = jnp.float32
BF16 = jnp.bfloat16

D_MODEL = 1024
DEPTH = 2
HEAD_DIM = 64
GRID_W = 64
EPS = 1e-6
ROPE_THETA = 10000.0
A_HEADS = 4
A_VDIM = 2 * HEAD_DIM
B_HEADS = 4
NA_KH = 8
NA_KW = 16
C_HEADS = 4
C_KV_HEADS = 2
WINDOW = 128
N_GROUPS = 4
EXPERTS_PER_GROUP = 4
N_EXPERTS = 16
D_EXPERT = 512
QK_SCALE = HEAD_DIM ** -0.5
LOG2E = math.log2(math.e)

_COLS = dict(aq=(0, 512), ak=(512, 1024), av=(1024, 1536), bq=(1536, 1792), bk=(1792, 2048),
             bv=(2048, 2304), cq=(2304, 2560), ck=(2560, 2688), cv=(2688, 2816))
_PT_ORDER = ("aq", "av", "bq", "bv", "cq", "cv")
_KT_ORDER = ("ak", "bk", "ck")
PT_ROWS = 1920
KC_COLS = 896
IN_WIDTH = 2816

LANES = 128
NEG = -1e30

TM_PROJ = 512
TQ_A = 512
TK_A = 2048
UNROLL_A = 2
NA_Q = 512
NA_PAIR_Q = 2 * GRID_W
NA_PAIR_ROWS = NA_KH + 2
NA_PAIR_K = NA_PAIR_ROWS * GRID_W
TQ_C = 256
TK_C = TQ_C + 2 * WINDOW
T_MOE = 1024
MOE_SLAB = 320
MOE_TILES_PER_WEIGHT = 2
FAST_SOFTMAX_BOUND = 30.0


def _cparams(sem, vmem_mb):
    return pltpu.CompilerParams(dimension_semantics=sem, vmem_limit_bytes=vmem_mb * 1024 * 1024)


def _nt_dot(a, b):
    return lax.dot_general(a, b, (((1,), (1,)), ((), ())), preferred_element_type=F32)


def _dot(a, b):
    return jnp.dot(a, b, preferred_element_type=F32)


def _inproj_kernel(x_ref, g1_ref, w_ref, gains_ref, cos_ref, sin_ref, pt_ref, kc_ref):
    tm = x_ref.shape[1]
    x = x_ref[0]
    ms = jnp.mean(x * x, axis=-1, keepdims=True)
    n = (x * lax.rsqrt(ms + EPS) * g1_ref[...]).astype(BF16)
    proj_t = _nt_dot(w_ref[...], n)
    cos = cos_ref[...]
    sin = sin_ref[...]
    reps = tm // LANES

    def head(r0, gi, rope):
        xh = proj_t[r0:r0 + HEAD_DIM, :]
        ssq = jnp.sum(xh * xh, axis=0, keepdims=True)
        gain = pltpu.repeat(gains_ref[gi], reps, axis=1)
        xg = xh * lax.rsqrt(ssq * (1.0 / HEAD_DIM) + EPS) * gain
        if rope:
            swapped = jnp.concatenate([xg[HEAD_DIM // 2:], xg[:HEAD_DIM // 2]], axis=0)
            xg = xg * cos + swapped * sin
        return xg

    def put(r0, val):
        pt_ref[0, r0:r0 + val.shape[0], :] = val.astype(BF16)

    for j in range(8):
        put(j * 64, head(j * 64, 0, True))
    put(512, proj_t[512:1024, :])
    for j in range(4):
        put(1024 + j * 64, head(1024 + j * 64, 2, False))
    put(1280, proj_t[1280:1536, :])
    for j in range(4):
        put(1536 + j * 64, head(1536 + j * 64, 4, True))
    put(1792, proj_t[1792:1920, :])
    kbase = PT_ROWS
    specs = [(1, True)] * 4 + [(3, False)] * 2 + [(5, True)]
    for j, (gi, rope) in enumerate(specs):
        r0 = kbase + j * 128
        blk = jnp.concatenate([head(r0, gi, rope), head(r0 + 64, gi, rope)], axis=0)
        kc_ref[0, :, j * 128:(j + 1) * 128] = blk.T.astype(BF16)


def _inproj(x, g1, w_t, gains, cos_t, sin_t):
    B, S, _ = x.shape
    tm = TM_PROJ
    return pl.pallas_call(
        _inproj_kernel,
        grid=(B, S // tm),
        in_specs=[
            pl.BlockSpec((1, tm, D_MODEL), lambda b, i: (b, i, 0)),
            pl.BlockSpec((1, D_MODEL), lambda b, i: (0, 0)),
            pl.BlockSpec((IN_WIDTH, D_MODEL), lambda b, i: (0, 0)),
            pl.BlockSpec((6, HEAD_DIM, LANES), lambda b, i: (0, 0, 0)),
            pl.BlockSpec((HEAD_DIM, tm), lambda b, i: (0, i)),
            pl.BlockSpec((HEAD_DIM, tm), lambda b, i: (0, i)),
        ],
        out_specs=[
            pl.BlockSpec((1, PT_ROWS, tm), lambda b, i: (b, 0, i)),
            pl.BlockSpec((1, tm, KC_COLS), lambda b, i: (b, i, 0)),
        ],
        out_shape=[
            jax.ShapeDtypeStruct((B, PT_ROWS, S), BF16),
            jax.ShapeDtypeStruct((B, S, KC_COLS), BF16),
        ],
        compiler_params=_cparams(("parallel", "parallel"), 56),
        name="inproj",
    )(x, g1, w_t, gains, cos_t, sin_t)


def _diff_attn_kernel(q_ref, k_ref, v_ref, lam_ref, subg_ref, o_ref, qz_ref, acc_ref, l_ref, m_ref,
                      *, tk, safe, lambda_init):
    tq = q_ref.shape[2]
    S = k_ref.shape[1]
    q = q_ref[0]
    z = jnp.zeros((HEAD_DIM, tq), BF16)
    qz_ref[:, :tq] = jnp.concatenate([q[:HEAD_DIM], z], axis=0)
    qz_ref[:, tq:] = jnp.concatenate([z, q[HEAD_DIM:]], axis=0)
    acc_ref[...] = jnp.zeros_like(acc_ref)
    l_ref[...] = jnp.zeros_like(l_ref)
    if safe:
        m_ref[...] = jnp.full_like(m_ref, NEG)

    def body(i, carry):
        off = pl.multiple_of(i * tk, tk)
        kc = k_ref[0, pl.ds(off, tk), :]
        vc = v_ref[0, :, pl.ds(off, tk)]
        s = _dot(kc, qz_ref[...])
        if safe:
            m_old = m_ref[...]
            m_new = jnp.maximum(m_old, jnp.max(s, axis=0, keepdims=True))
            alpha = jnp.exp2(m_old - m_new)
            p = jnp.exp2(s - m_new)
            m_ref[...] = m_new
            l_ref[...] = alpha * l_ref[...] + jnp.sum(p, axis=0, keepdims=True)
            acc_ref[...] = alpha * acc_ref[...] + _dot(vc, p.astype(BF16))
        else:
            p = jnp.exp2(s)
            l_ref[...] = l_ref[...] + jnp.sum(p, axis=0, keepdims=True)
            acc_ref[...] = acc_ref[...] + _dot(vc, p.astype(BF16))
        return carry

    lax.fori_loop(0, S // tk, body, 0, unroll=UNROLL_A)

    lp = lam_ref[...]
    lam = (jnp.exp(jnp.sum(lp[0:1] * lp[1:2], axis=-1, keepdims=True))
           - jnp.exp(jnp.sum(lp[2:3] * lp[3:4], axis=-1, keepdims=True)) + lambda_init)
    on = acc_ref[...] / l_ref[...]
    o = on[:, :tq] - lam * on[:, tq:]
    ms = jnp.mean(o * o, axis=0, keepdims=True)
    gain = pltpu.repeat(subg_ref[...], tq // LANES, axis=1)
    y = o * lax.rsqrt(ms + EPS) * gain * (1.0 - lambda_init)
    o_ref[0] = y.T.astype(BF16)


def _diff_attn(pt, kc, lam_params, subg, lambda_init, safe):
    B, _, S = pt.shape
    tq, tk = TQ_A, min(TK_A, S)
    kern = functools.partial(_diff_attn_kernel, tk=tk, safe=safe, lambda_init=lambda_init)
    return pl.pallas_call(
        kern,
        grid=(B, A_HEADS, S // tq),
        in_specs=[
            pl.BlockSpec((1, 128, tq), lambda b, h, i: (b, h, i)),
            pl.BlockSpec((1, S, 128), lambda b, h, i: (b, 0, h)),
            pl.BlockSpec((1, 128, S), lambda b, h, i: (b, 4 + h, 0)),
            pl.BlockSpec((4, HEAD_DIM), lambda b, h, i: (0, 0)),
            pl.BlockSpec((A_VDIM, LANES), lambda b, h, i: (0, 0)),
        ],
        out_specs=pl.BlockSpec((1, tq, 128), lambda b, h, i: (b, i, h)),
        out_shape=jax.ShapeDtypeStruct((B, S, A_HEADS * A_VDIM), BF16),
        scratch_shapes=[
            pltpu.VMEM((128, 2 * tq), BF16),
            pltpu.VMEM((A_VDIM, 2 * tq), F32),
            pltpu.VMEM((1, 2 * tq), F32),
            pltpu.VMEM((1, 2 * tq), F32),
        ],
        compiler_params=_cparams(("parallel", "parallel", "arbitrary"), 48),
        name="diff_attn_safe" if safe else "diff_attn",
    )(pt, kc, pt, lam_params, subg)


def _na_kernel(q_ref, k_ref, v_ref, bias_ref, o_ref):
    S = k_ref.shape[1]
    rows = S // GRID_W
    r = pl.program_id(2)
    q = q_ref[0]
    z = jnp.zeros((HEAD_DIM, NA_PAIR_Q), BF16)
    npairs = NA_Q // NA_PAIR_Q
    scores, vals = [], []
    for u in range(npairs):
        mp = r * npairs + u
        srow = jnp.clip(2 * mp - NA_KH // 2, 0, rows - NA_PAIR_ROWS)
        start = pl.multiple_of(srow * GRID_W, 128)
        var = jnp.where(mp == 0, 0, jnp.where(mp == 1, 1, jnp.where(
            mp == rows // 2 - 2, 3, jnp.where(mp == rows // 2 - 1, 4, 2))))
        ks = k_ref[0, pl.ds(start, NA_PAIR_K), :]
        vals.append(v_ref[0, :, pl.ds(start, NA_PAIR_K)])
        qp = q[:, u * NA_PAIR_Q:(u + 1) * NA_PAIR_Q]
        qz = jnp.concatenate([jnp.concatenate([qp[:HEAD_DIM], z], axis=0),
                              jnp.concatenate([z, qp[HEAD_DIM:]], axis=0)], axis=1)
        scores.append(_dot(ks, qz) + bias_ref[0, var])
    probs, sums = [], []
    for s in scores:
        m = jnp.max(s, axis=0, keepdims=True)
        p = jnp.exp(s - m)
        sums.append(jnp.sum(p, axis=0, keepdims=True))
        probs.append(p.astype(BF16))
    outs = []
    for vs, p, l in zip(vals, probs, sums):
        o = _dot(vs, p) / l
        outs.append(jnp.concatenate([o[:HEAD_DIM, :NA_PAIR_Q], o[HEAD_DIM:, NA_PAIR_Q:]], axis=0))
    o_ref[0] = jnp.concatenate(outs, axis=1).T.astype(BF16)


def _na_attn(pt, kc, bias):
    B, _, S = pt.shape
    nr = S // NA_Q
    return pl.pallas_call(
        _na_kernel,
        grid=(B, 2, nr),
        in_specs=[
            pl.BlockSpec((1, 128, NA_Q), lambda b, p, r: (b, 8 + p, r)),
            pl.BlockSpec((1, S, 128), lambda b, p, r: (b, 0, 4 + p)),
            pl.BlockSpec((1, 128, S), lambda b, p, r: (b, 10 + p, 0)),
            pl.BlockSpec((1, 5, NA_PAIR_K, 2 * NA_PAIR_Q), lambda b, p, r: (p, 0, 0, 0)),
        ],
        out_specs=pl.BlockSpec((1, NA_Q, 128), lambda b, p, r: (b, r, p)),
        out_shape=jax.ShapeDtypeStruct((B, S, B_HEADS * HEAD_DIM), BF16),
        compiler_params=_cparams(("parallel", "parallel", "arbitrary"), 48),
        name="na_attn",
    )(pt, kc, pt, bias)


def _na_bias_index():
    rows = 32
    a = np.arange(2)[None, :]
    i = np.arange(NA_PAIR_ROWS)[:, None]
    ridx, rvalid = [], []
    for mp in (0, 1, 5, rows // 2 - 2, rows // 2 - 1):
        srow = np.clip(2 * mp - NA_KH // 2, 0, rows - NA_PAIR_ROWS)
        qrow = 2 * mp + a
        krow = srow + i
        rs = np.clip(qrow - NA_KH // 2, 0, rows - NA_KH)
        rvalid.append((krow >= rs) & (krow < rs + NA_KH))
        ridx.append(np.clip(krow - qrow + NA_KH - 1, 0, 2 * NA_KH - 2))
    ridx = np.stack(ridx)
    rvalid = np.stack(rvalid)
    qc = np.arange(GRID_W)[None, :]
    kcol = np.arange(GRID_W)[:, None]
    cs = np.clip(qc - NA_KW // 2, 0, GRID_W - NA_KW)
    cvalid = (kcol >= cs) & (kcol < cs + NA_KW)
    cidx = np.clip(kcol - qc + NA_KW - 1, 0, 2 * NA_KW - 2)
    valid = rvalid[:, :, None, :, None] & cvalid[None, None, :, None, :]
    col_onehot = (cidx.reshape(-1)[None, :] == np.arange(2 * NA_KW - 1)[:, None])
    return (ridx.astype(np.int32), col_onehot.astype(np.float32),
            valid.reshape(5, NA_PAIR_K, NA_PAIR_Q))


_NA_RIDX, _NA_COL_ONEHOT, _NA_VALID = _na_bias_index()


def _na_bias_table(rpb):
    rows = jnp.take(rpb.astype(F32), jnp.asarray(_NA_RIDX), axis=1)
    tab = jnp.einsum("hvkqc,cn->hvkqn", rows, jnp.asarray(_NA_COL_ONEHOT),
                     precision=lax.Precision.HIGHEST)
    tab = tab.reshape(B_HEADS, 5, NA_PAIR_ROWS, 2, GRID_W, GRID_W).transpose(0, 1, 2, 4, 3, 5)
    tab = tab.reshape(B_HEADS, 5, NA_PAIR_K, NA_PAIR_Q)
    tab = jnp.where(jnp.asarray(_NA_VALID)[None], tab, NEG)
    tab = tab.reshape(2, 2, 5, NA_PAIR_K, NA_PAIR_Q).transpose(0, 2, 3, 1, 4)
    return tab.reshape(2, 5, NA_PAIR_K, 2 * NA_PAIR_Q)


def _win_kernel(sink_ref, q_ref, k_ref, v_ref, o_ref):
    S = k_ref.shape[1]
    i = pl.program_id(1)
    q0 = i * TQ_C
    start = pl.multiple_of(jnp.clip(q0 - WINDOW, 0, S - TK_C), 128)
    ks = k_ref[0, pl.ds(start, TK_C), :]
    vs = v_ref[0, :, pl.ds(start, TK_C)]
    kpos = start + lax.broadcasted_iota(jnp.int32, (TK_C, TQ_C), 0)
    qpos = q0 + lax.broadcasted_iota(jnp.int32, (TK_C, TQ_C), 1)
    mask = jnp.where(jnp.abs(qpos - kpos) <= WINDOW, 0.0, NEG).astype(F32)
    z = jnp.zeros((HEAD_DIM, TQ_C), BF16)
    gq = C_HEADS // C_KV_HEADS
    scores = []
    for hq in range(C_HEADS):
        qh = q_ref[0, hq * HEAD_DIM:(hq + 1) * HEAD_DIM, :]
        qz = jnp.concatenate([qh, z] if hq // gq == 0 else [z, qh], axis=0)
        scores.append(_dot(ks, qz) + mask)
    probs, dens = [], []
    for hq, s in enumerate(scores):
        sk = sink_ref[hq]
        m = jnp.maximum(jnp.max(s, axis=0, keepdims=True), sk)
        e = jnp.exp(s - m)
        dens.append(jnp.sum(e, axis=0, keepdims=True) + jnp.exp(sk - m))
        probs.append(e.astype(BF16))
    outs = []
    for hq, (e, den) in enumerate(zip(probs, dens)):
        kv = hq // gq
        outs.append(_dot(vs[kv * HEAD_DIM:(kv + 1) * HEAD_DIM], e) / den)
    o_ref[0] = jnp.concatenate(outs, axis=0).T.astype(BF16)


def _win_attn(pt, kc, sink):
    B, _, S = pt.shape
    return pl.pallas_call(
        _win_kernel,
        grid=(B, S // TQ_C),
        in_specs=[
            pl.BlockSpec(memory_space=pltpu.SMEM),
            pl.BlockSpec((1, 256, TQ_C), lambda b, i: (b, 6, i)),
            pl.BlockSpec((1, S, 128), lambda b, i: (b, 0, 6)),
            pl.BlockSpec((1, 128, S), lambda b, i: (b, 14, 0)),
        ],
        out_specs=pl.BlockSpec((1, TQ_C, 256), lambda b, i: (b, i, 0)),
        out_shape=jax.ShapeDtypeStruct((B, S, C_HEADS * HEAD_DIM), BF16),
        compiler_params=_cparams(("parallel", "arbitrary"), 48),
        name="win_attn",
    )(sink, pt, kc, pt)


def _outproj_kernel(x_ref, ya_ref, yb_ref, yc_ref, wo_ref, g2_ref, wr_ref, br_ref, tri_ref,
                    h_ref, n2_ref, ri_ref, rs_ref, pr_ref, cnt_ref):
    tm = x_ref.shape[0]
    h = (x_ref[...]
         + _dot(ya_ref[...], wo_ref[0:512, :])
         + _dot(yb_ref[...], wo_ref[512:768, :])
         + _dot(yc_ref[...], wo_ref[768:1024, :]))
    h_ref[...] = h
    ms = jnp.mean(h * h, axis=-1, keepdims=True)
    n2 = h * lax.rsqrt(ms + EPS) * g2_ref[...]
    n2_hi = n2.astype(BF16)
    n2_ref[...] = n2_hi
    n2_lo = (n2 - n2_hi.astype(F32)).astype(BF16)
    both = _dot(n2_hi, wr_ref[...])
    logits = both[:, :LANES] + both[:, LANES:] + _dot(n2_lo, wr_ref[:, :LANES]) + br_ref[...]
    lane = lax.broadcasted_iota(jnp.int32, logits.shape, 1).astype(F32)
    big = float(LANES)
    ninf = -jnp.inf

    def first_argmax(vals):
        vmax = jnp.max(vals, axis=-1, keepdims=True)
        idx = jnp.min(jnp.where(vals == vmax, lane, big), axis=-1, keepdims=True)
        return vmax, idx

    gl = jnp.where(lane < N_GROUPS, logits, ninf)
    gmax, gidx = first_argmax(gl)
    gp = 1.0 / jnp.sum(jnp.exp(gl - gmax), axis=-1, keepdims=True)
    lo = N_GROUPS + EXPERTS_PER_GROUP * gidx
    el = jnp.where((lane >= lo) & (lane < lo + EXPERTS_PER_GROUP), logits, ninf)
    v1, i1 = first_argmax(el)
    el2 = jnp.where(lane == i1, ninf, el)
    v2, i2 = first_argmax(el2)
    t = jnp.exp(v2 - v1)
    w1 = 1.0 / (1.0 + t)
    w2 = t / (1.0 + t)
    comb = jnp.where(lane == i1, gp * w1, jnp.where(lane == i2, gp * w2, 0.0))

    onehot = jnp.where(lane == gidx, 1.0, 0.0)
    before = _dot(tri_ref[...], onehot.astype(BF16))
    cnt = jnp.sum(onehot, axis=0, keepdims=True)
    pos = jnp.sum(jnp.where(lane < gidx, cnt, 0.0) + jnp.where(lane == gidx, before, 0.0),
                  axis=-1, keepdims=True)
    ri = jnp.where(lane == 0.0, pos, comb)
    ri_ref[...] = ri
    hi = ri.astype(BF16)
    r1 = ri - hi.astype(F32)
    mid = r1.astype(BF16)
    low = (r1 - mid.astype(F32)).astype(BF16)
    rs_ref[...] = jnp.concatenate([hi, mid, low], axis=1)
    pos_t = jnp.broadcast_to(pos, (tm, LANES)).T
    pr_ref[0] = pos_t[0:8, :]
    cnt_ref[0] = jnp.broadcast_to(cnt, (8, LANES)).astype(jnp.int32)


def _outproj(x2, ya, yb, yc, wo, g2, wr, br, tri):
    T = x2.shape[0]
    tm = T_MOE
    nt = T // tm
    row = lambda i: (i, 0)
    fixed = lambda i: (0, 0)
    return pl.pallas_call(
        _outproj_kernel,
        grid=(nt,),
        in_specs=[
            pl.BlockSpec((tm, D_MODEL), row),
            pl.BlockSpec((tm, 512), row),
            pl.BlockSpec((tm, 256), row),
            pl.BlockSpec((tm, 256), row),
            pl.BlockSpec((D_MODEL, D_MODEL), fixed),
            pl.BlockSpec((1, D_MODEL), fixed),
            pl.BlockSpec((D_MODEL, 2 * LANES), fixed),
            pl.BlockSpec((1, LANES), fixed),
            pl.BlockSpec((tm, tm), fixed),
        ],
        out_specs=[
            pl.BlockSpec((tm, D_MODEL), row),
            pl.BlockSpec((tm, D_MODEL), row),
            pl.BlockSpec((tm, LANES), row),
            pl.BlockSpec((tm, 3 * LANES), row),
            pl.BlockSpec((1, 8, tm), lambda i: (i, 0, 0)),
            pl.BlockSpec((1, 8, LANES), lambda i: (i, 0, 0)),
        ],
        out_shape=[
            jax.ShapeDtypeStruct((T, D_MODEL), F32),
            jax.ShapeDtypeStruct((T, D_MODEL), BF16),
            jax.ShapeDtypeStruct((T, LANES), F32),
            jax.ShapeDtypeStruct((T, 3 * LANES), BF16),
            jax.ShapeDtypeStruct((nt, 8, tm), F32),
            jax.ShapeDtypeStruct((nt, 8, LANES), jnp.int32),
        ],
        compiler_params=_cparams(("parallel",), 56),
        name="outproj",
    )(x2, ya, yb, yc, wo, g2, wr, br, tri)


def _moe_kernel(cnt_ref, x_ref, h_ref, ri_ref, rs_ref, pr_ref, wg_ref, wu_ref, wd_ref, o_ref,
                xs_ref, cs_ref, os_ref):
    tm = x_ref.shape[0]
    e = pl.program_id(1)
    t = pl.program_id(2)
    i = pl.program_id(0) * pl.num_programs(2) + t
    g = lax.shift_right_logical(e, 2)
    n = [cnt_ref[i * N_GROUPS + k] for k in range(N_GROUPS)]
    off = (jnp.where(g > 0, n[0], 0) + jnp.where(g > 1, n[1], 0) + jnp.where(g > 2, n[2], 0))
    ng = jnp.where(g == 0, n[0], jnp.where(g == 1, n[1], jnp.where(g == 2, n[2], n[3])))
    start = lax.shift_left(lax.shift_right_logical(off, 4), 4)
    nslab = lax.div(off - start + ng + (MOE_SLAB - 1), MOE_SLAB)

    @pl.when(e == 0)
    def _():
        want = lax.broadcasted_iota(jnp.int32, (tm, tm), 0).astype(F32)
        perm = jnp.where(pr_ref[0, 0:1, :] == want, 1.0, 0.0).astype(BF16)
        xs_ref[t, 0:tm, :] = _dot(perm, x_ref[...]).astype(BF16)
        c3 = _dot(perm, rs_ref[...])
        cs_ref[t, 0:tm, :] = c3[:, :LANES] + c3[:, LANES:2 * LANES] + c3[:, 2 * LANES:]
        xs_ref[t, tm:, :] = jnp.zeros((MOE_SLAB, D_MODEL), BF16)
        cs_ref[t, tm:, :] = jnp.zeros((MOE_SLAB, LANES), F32)
        os_ref[t] = jnp.zeros(os_ref.shape[1:], F32)

    def ffn(s, carry):
        r0 = pl.multiple_of(start + s * MOE_SLAB, 16)
        xs = xs_ref[t, pl.ds(r0, MOE_SLAB), :]
        hg = _dot(xs, wg_ref[0])
        hu = _dot(xs, wu_ref[0])
        cs = cs_ref[t, pl.ds(r0, MOE_SLAB), :]
        lane = lax.broadcasted_iota(jnp.int32, cs.shape, 1)
        w = jnp.sum(jnp.where(lane == N_GROUPS + e, cs, 0.0), axis=-1, keepdims=True)
        hh = (hg * jax.nn.sigmoid(hg) * hu * w).astype(BF16)
        os_ref[t, pl.ds(r0, MOE_SLAB), :] = os_ref[t, pl.ds(r0, MOE_SLAB), :] + _dot(hh, wd_ref[0])
        return carry
    lax.fori_loop(0, nslab, ffn, 0)

    @pl.when(e == N_EXPERTS - 1)
    def _():
        want = lax.broadcasted_iota(jnp.int32, (tm, tm), 1).astype(F32)
        perm_t = jnp.where(ri_ref[:, 0:1] == want, 1.0, 0.0).astype(BF16)
        o_ref[...] = h_ref[...] + _dot(perm_t, os_ref[t, 0:tm, :].astype(BF16))


def _moe(n2, h, ri, rs, pr, cnt, wg, wu, wd):
    T = n2.shape[0]
    tm = T_MOE
    nt = T // tm
    tpw = MOE_TILES_PER_WEIGHT if nt % MOE_TILES_PER_WEIGHT == 0 else 1
    rows = tm + MOE_SLAB
    last = N_EXPERTS - 1
    first_map = lambda s, e, t, c: (s * tpw + jnp.where(e == 0, t, tpw - 1), 0)
    last_map = lambda s, e, t, c: (s * tpw + jnp.where(e == last, t, 0), 0)
    wmap = lambda s, e, t, c: (e, 0, 0)
    return pl.pallas_call(
        _moe_kernel,
        grid_spec=pltpu.PrefetchScalarGridSpec(
            num_scalar_prefetch=1,
            grid=(nt // tpw, N_EXPERTS, tpw),
            in_specs=[
                pl.BlockSpec((tm, D_MODEL), first_map),
                pl.BlockSpec((tm, D_MODEL), last_map),
                pl.BlockSpec((tm, LANES), last_map),
                pl.BlockSpec((tm, 3 * LANES), first_map),
                pl.BlockSpec((1, 8, tm), lambda s, e, t, c: (first_map(s, e, t, c)[0], 0, 0)),
                pl.BlockSpec((1, D_MODEL, D_EXPERT), wmap),
                pl.BlockSpec((1, D_MODEL, D_EXPERT), wmap),
                pl.BlockSpec((1, D_EXPERT, D_MODEL), wmap),
            ],
            out_specs=pl.BlockSpec((tm, D_MODEL), last_map),
            scratch_shapes=[
                pltpu.VMEM((tpw, rows, D_MODEL), BF16),
                pltpu.VMEM((tpw, rows, LANES), F32),
                pltpu.VMEM((tpw, rows, D_MODEL), F32),
            ],
        ),
        out_shape=jax.ShapeDtypeStruct((T, D_MODEL), F32),
        compiler_params=_cparams(("parallel", "arbitrary", "arbitrary"), 60),
        name="moe",
    )(cnt, n2, h, ri, rs, pr, wg, wu, wd)


def _rope_tables_t(seq_len):
    pos = jnp.arange(seq_len, dtype=F32)
    inv = ROPE_THETA ** (-jnp.arange(0, HEAD_DIM, 2, dtype=F32) / HEAD_DIM)
    ang = pos[:, None] * inv[None, :]
    ang = jnp.concatenate([ang, ang], axis=-1)
    sign = jnp.where(jnp.arange(HEAD_DIM) < HEAD_DIM // 2, -1.0, 1.0).astype(F32)
    return jnp.cos(ang).T, (jnp.sin(ang) * sign[None, :]).T


def _lane_bcast(v):
    return jnp.broadcast_to(v.astype(F32)[:, None], (v.shape[0], LANES))


def _prep_layer(l, p):
    w_in = p["w_in"][l]
    cols = [w_in[:, _COLS[n][0]:_COLS[n][1]] for n in _PT_ORDER + _KT_ORDER]
    w_t = jnp.concatenate(cols, axis=1).T.astype(BF16)
    gains = jnp.stack([
        _lane_bcast(p["a_q_norm"][l] * (QK_SCALE * LOG2E)), _lane_bcast(p["a_k_norm"][l]),
        _lane_bcast(p["b_q_norm"][l] * QK_SCALE), _lane_bcast(p["b_k_norm"][l]),
        _lane_bcast(p["c_q_norm"][l] * QK_SCALE), _lane_bcast(p["c_k_norm"][l]),
    ])
    wr = jnp.zeros((D_MODEL, LANES), F32)
    wr = wr.at[:, :N_GROUPS].set(p["w_group"][l]).at[:, N_GROUPS:N_GROUPS + N_EXPERTS].set(p["w_router"][l])
    wr_hi = wr.astype(BF16)
    wr_lo = (wr - wr_hi.astype(F32)).astype(BF16)
    wr2 = jnp.concatenate([wr_hi, wr_lo], axis=1)
    br = jnp.zeros((1, LANES), F32)
    br = br.at[0, :N_GROUPS].set(p["b_group"][l]).at[0, N_GROUPS:N_GROUPS + N_EXPERTS].set(p["b_router"][l])
    bound = 1.02 * HEAD_DIM * QK_SCALE * jnp.max(jnp.abs(p["a_q_norm"][l])) * jnp.max(jnp.abs(p["a_k_norm"][l]))
    return dict(
        g1=p["norm1_g"][l].astype(F32)[None, :], w_t=w_t, gains=gains,
        lam=jnp.stack([p["a_lambda_q1"][l], p["a_lambda_k1"][l],
                       p["a_lambda_q2"][l], p["a_lambda_k2"][l]]).astype(F32),
        subg=_lane_bcast(p["a_subln"][l]),
        na_bias=_na_bias_table(p["b_rpb"][l]),
        sink=p["c_sink"][l].astype(F32),
        wo=p["w_out"][l].astype(BF16), g2=p["norm2_g"][l].astype(F32)[None, :],
        wr=wr2, br=br,
        wg=p["w_gate"][l].astype(BF16), wu=p["w_up"][l].astype(BF16), wd=p["w_down"][l].astype(BF16),
        fast_ok=bound <= FAST_SOFTMAX_BOUND,
    )


def _layer(x, lp, lambda_init, cos_t, sin_t):
    B, S, _ = x.shape
    pt, kc = _inproj(x, lp["g1"], lp["w_t"], lp["gains"], cos_t, sin_t)
    ya = lax.cond(
        lp["fast_ok"],
        lambda a, b: _diff_attn(a, b, lp["lam"], lp["subg"], lambda_init, False),
        lambda a, b: _diff_attn(a, b, lp["lam"], lp["subg"], lambda_init, True),
        pt, kc)
    yb = _na_attn(pt, kc, lp["na_bias"])
    yc = _win_attn(pt, kc, lp["sink"])
    T = B * S
    tri = jnp.tril(jnp.ones((T_MOE, T_MOE), BF16), -1)
    h, n2, ri, rs, pr, cnt = _outproj(x.reshape(T, D_MODEL), ya.reshape(T, -1), yb.reshape(T, -1),
                                      yc.reshape(T, -1), lp["wo"], lp["g2"], lp["wr"], lp["br"], tri)
    counts = cnt[:, 0, :N_GROUPS].reshape(-1)
    y = _moe(n2, h, ri, rs, pr, counts, lp["wg"], lp["wu"], lp["wd"])
    return y.reshape(B, S, D_MODEL)


def kernel(x_prompt, x_sample, norm1_g, w_in, w_out, a_q_norm, a_k_norm, a_lambda_q1, a_lambda_k1,
           a_lambda_q2, a_lambda_k2, a_subln, b_q_norm, b_k_norm, b_rpb, c_q_norm, c_k_norm, c_sink,
           norm2_g, w_group, b_group, w_router, b_router, w_gate, w_up, w_down):
    p = dict(norm1_g=norm1_g, w_in=w_in, w_out=w_out, a_q_norm=a_q_norm, a_k_norm=a_k_norm,
             a_lambda_q1=a_lambda_q1, a_lambda_k1=a_lambda_k1, a_lambda_q2=a_lambda_q2,
             a_lambda_k2=a_lambda_k2, a_subln=a_subln, b_q_norm=b_q_norm, b_k_norm=b_k_norm,
             b_rpb=b_rpb, c_q_norm=c_q_norm, c_k_norm=c_k_norm, c_sink=c_sink, norm2_g=norm2_g,
             w_group=w_group, b_group=b_group, w_router=w_router, b_router=b_router,
             w_gate=w_gate, w_up=w_up, w_down=w_down)
    layers = [_prep_layer(l, p) for l in range(DEPTH)]

    def run(x):
        cos_t, sin_t = _rope_tables_t(x.shape[1])
        for l in range(DEPTH):
            lambda_init = 0.8 - 0.6 * math.exp(-0.3 * l)
            x = _layer(x, layers[l], lambda_init, cos_t, sin_t)
        return x

    return (run(x_prompt), run(x_sample))
```

```python
import functools
import math

import numpy as np
import jax
import jax.numpy as jnp
from jax import lax
from jax.experimental import pallas as pl
from jax.experimental.pallas import tpu as pltpu

F32 = jnp.float32
BF16 = jnp.bfloat16

D_MODEL = 1024
DEPTH = 2
HEAD_DIM = 64
GRID_W = 64
EPS = 1e-6
ROPE_THETA = 10000.0
A_HEADS = 4
A_VDIM = 2 * HEAD_DIM
B_HEADS = 4
NA_KH = 8
NA_KW = 16
C_HEADS = 4
C_KV_HEADS = 2
WINDOW = 128
N_GROUPS = 4
EXPERTS_PER_GROUP = 4
N_EXPERTS = 16
D_EXPERT = 512
QK_SCALE = HEAD_DIM ** -0.5
LOG2E = math.log2(math.e)

_COLS = dict(aq=(0, 512), ak=(512, 1024), av=(1024, 1536), bq=(1536, 1792), bk=(1792, 2048),
             bv=(2048, 2304), cq=(2304, 2560), ck=(2560, 2688), cv=(2688, 2816))
_PT_ORDER = ("aq", "av", "bq", "bv", "cq", "cv")
_KT_ORDER = ("ak", "bk", "ck")
PT_ROWS = 1920
KC_COLS = 896
IN_WIDTH = 2816

LANES = 128
NEG = -1e30

TM_PROJ = 1024
TQ_A = 512
TK_A = 2048
UNROLL_A = 2
NA_Q = 512
NA_PAIR_Q = 2 * GRID_W
NA_PAIR_ROWS = NA_KH + 2
NA_PAIR_K = NA_PAIR_ROWS * GRID_W
TQ_C = 256
TK_C = TQ_C + 2 * WINDOW
T_MOE = 1024
MOE_SLAB = 512
MOE_STEP = 64
MOE_TILES_PER_WEIGHT = 2
FAST_SOFTMAX_BOUND = 30.0


def _cparams(sem, vmem_mb):
    return pltpu.CompilerParams(dimension_semantics=sem, vmem_limit_bytes=vmem_mb * 1024 * 1024)


def _nt_dot(a, b):
    return lax.dot_general(a, b, (((1,), (1,)), ((), ())), preferred_element_type=F32)


def _dot(a, b):
    return jnp.dot(a, b, preferred_element_type=F32)


def _inproj_kernel(x_ref, g1_ref, w_ref, gains_ref, cos_ref, sin_ref, pt_ref, kc_ref):
    tm = x_ref.shape[1]
    x = x_ref[0]
    ms = jnp.mean(x * x, axis=-1, keepdims=True)
    n = (x * lax.rsqrt(ms + EPS) * g1_ref[...]).astype(BF16)
    cos = cos_ref[...]
    sin = sin_ref[...]
    reps = tm // LANES

    def project(r0, r1):
        return _nt_dot(w_ref[r0:r1, :], n)

    def head(chunk, r0, gi, rope):
        xh = chunk[r0:r0 + HEAD_DIM, :]
        ssq = jnp.sum(xh * xh, axis=0, keepdims=True)
        gain = pltpu.repeat(gains_ref[gi], reps, axis=1)
        xg = xh * lax.rsqrt(ssq * (1.0 / HEAD_DIM) + EPS) * gain
        if rope:
            swapped = jnp.concatenate([xg[HEAD_DIM // 2:], xg[:HEAD_DIM // 2]], axis=0)
            xg = xg * cos + swapped * sin
        return xg

    def put(r0, val):
        pt_ref[0, r0:r0 + val.shape[0], :] = val.astype(BF16)

    c = project(0, 512)
    for j in range(8):
        put(j * 64, head(c, j * 64, 0, True))
    put(512, project(512, 1024))
    c = project(1024, 1536)
    for j in range(4):
        put(1024 + j * 64, head(c, j * 64, 2, False))
    put(1280, c[256:512, :])
    c = project(1536, 1920)
    for j in range(4):
        put(1536 + j * 64, head(c, j * 64, 4, True))
    put(1792, c[256:384, :])
    def put_keys(chunk, j0, specs):
        for j, (gi, rope) in enumerate(specs):
            blk = jnp.concatenate([head(chunk, j * 128, gi, rope),
                                   head(chunk, j * 128 + 64, gi, rope)], axis=0)
            kc_ref[0, :, (j0 + j) * 128:(j0 + j + 1) * 128] = blk.T.astype(BF16)

    put_keys(project(PT_ROWS, PT_ROWS + 512), 0, [(1, True)] * 4)
    put_keys(project(PT_ROWS + 512, IN_WIDTH), 4, [(3, False)] * 2 + [(5, True)])


def _inproj(x, g1, w_t, gains, cos_t, sin_t):
    B, S, _ = x.shape
    tm = TM_PROJ
    return pl.pallas_call(
        _inproj_kernel,
        grid=(B, S // tm),
        in_specs=[
            pl.BlockSpec((1, tm, D_MODEL), lambda b, i: (b, i, 0)),
            pl.BlockSpec((1, D_MODEL), lambda b, i: (0, 0)),
            pl.BlockSpec((IN_WIDTH, D_MODEL), lambda b, i: (0, 0)),
            pl.BlockSpec((6, HEAD_DIM, LANES), lambda b, i: (0, 0, 0)),
            pl.BlockSpec((HEAD_DIM, tm), lambda b, i: (0, i)),
            pl.BlockSpec((HEAD_DIM, tm), lambda b, i: (0, i)),
        ],
        out_specs=[
            pl.BlockSpec((1, PT_ROWS, tm), lambda b, i: (b, 0, i)),
            pl.BlockSpec((1, tm, KC_COLS), lambda b, i: (b, i, 0)),
        ],
        out_shape=[
            jax.ShapeDtypeStruct((B, PT_ROWS, S), BF16),
            jax.ShapeDtypeStruct((B, S, KC_COLS), BF16),
        ],
        compiler_params=_cparams(("parallel", "parallel"), 56),
        name="inproj",
    )(x, g1, w_t, gains, cos_t, sin_t)


def _diff_attn_kernel(q_ref, k_ref, v_ref, lam_ref, subg_ref, o_ref, qz_ref, acc_ref, l_ref, m_ref,
                      *, tk, safe, lambda_init):
    tq = q_ref.shape[2]
    S = k_ref.shape[1]
    q = q_ref[0]
    z = jnp.zeros((HEAD_DIM, tq), BF16)
    qz_ref[:, :tq] = jnp.concatenate([q[:HEAD_DIM], z], axis=0)
    qz_ref[:, tq:] = jnp.concatenate([z, q[HEAD_DIM:]], axis=0)
    acc_ref[...] = jnp.zeros_like(acc_ref)
    l_ref[...] = jnp.zeros_like(l_ref)
    if safe:
        m_ref[...] = jnp.full_like(m_ref, NEG)

    def body(i, carry):
        off = pl.multiple_of(i * tk, tk)
        kc = k_ref[0, pl.ds(off, tk), :]
        vc = v_ref[0, :, pl.ds(off, tk)]
        s = _dot(kc, qz_ref[...])
        if safe:
            m_old = m_ref[...]
            m_new = jnp.maximum(m_old, jnp.max(s, axis=0, keepdims=True))
            alpha = jnp.exp2(m_old - m_new)
            p = jnp.exp2(s - m_new)
            m_ref[...] = m_new
            l_ref[...] = alpha * l_ref[...] + jnp.sum(p, axis=0, keepdims=True)
            acc_ref[...] = alpha * acc_ref[...] + _dot(vc, p.astype(BF16))
        else:
            p = jnp.exp2(s)
            l_ref[...] = l_ref[...] + jnp.sum(p, axis=0, keepdims=True)
            acc_ref[...] = acc_ref[...] + _dot(vc, p.astype(BF16))
        return carry

    lax.fori_loop(0, S // tk, body, 0, unroll=UNROLL_A)

    lp = lam_ref[...]
    lam = (jnp.exp(jnp.sum(lp[0:1] * lp[1:2], axis=-1, keepdims=True))
           - jnp.exp(jnp.sum(lp[2:3] * lp[3:4], axis=-1, keepdims=True)) + lambda_init)
    on = acc_ref[...] / l_ref[...]
    o = on[:, :tq] - lam * on[:, tq:]
    ms = jnp.mean(o * o, axis=0, keepdims=True)
    gain = pltpu.repeat(subg_ref[...], tq // LANES, axis=1)
    y = o * lax.rsqrt(ms + EPS) * gain * (1.0 - lambda_init)
    o_ref[0] = y.T.astype(BF16)


def _diff_attn(pt, kc, lam_params, subg, lambda_init, safe):
    B, _, S = pt.shape
    tq, tk = TQ_A, min(TK_A, S)
    kern = functools.partial(_diff_attn_kernel, tk=tk, safe=safe, lambda_init=lambda_init)
    return pl.pallas_call(
        kern,
        grid=(B, A_HEADS, S // tq),
        in_specs=[
            pl.BlockSpec((1, 128, tq), lambda b, h, i: (b, h, i)),
            pl.BlockSpec((1, S, 128), lambda b, h, i: (b, 0, h)),
            pl.BlockSpec((1, 128, S), lambda b, h, i: (b, 4 + h, 0)),
            pl.BlockSpec((4, HEAD_DIM), lambda b, h, i: (0, 0)),
            pl.BlockSpec((A_VDIM, LANES), lambda b, h, i: (0, 0)),
        ],
        out_specs=pl.BlockSpec((1, tq, 128), lambda b, h, i: (b, i, h)),
        out_shape=jax.ShapeDtypeStruct((B, S, A_HEADS * A_VDIM), BF16),
        scratch_shapes=[
            pltpu.VMEM((128, 2 * tq), BF16),
            pltpu.VMEM((A_VDIM, 2 * tq), F32),
            pltpu.VMEM((1, 2 * tq), F32),
            pltpu.VMEM((1, 2 * tq), F32),
        ],
        compiler_params=_cparams(("parallel", "parallel", "arbitrary"), 48),
        name="diff_attn_safe" if safe else "diff_attn",
    )(pt, kc, pt, lam_params, subg)


def _na_kernel(q_ref, k_ref, v_ref, bias_ref, o_ref):
    S = k_ref.shape[1]
    rows = S // GRID_W
    r = pl.program_id(2)
    q = q_ref[0]
    z = jnp.zeros((HEAD_DIM, NA_PAIR_Q), BF16)
    npairs = NA_Q // NA_PAIR_Q
    scores, vals = [], []
    for u in range(npairs):
        mp = r * npairs + u
        srow = jnp.clip(2 * mp - NA_KH // 2, 0, rows - NA_PAIR_ROWS)
        start = pl.multiple_of(srow * GRID_W, 128)
        var = jnp.where(mp == 0, 0, jnp.where(mp == 1, 1, jnp.where(
            mp == rows // 2 - 2, 3, jnp.where(mp == rows // 2 - 1, 4, 2))))
        ks = k_ref[0, pl.ds(start, NA_PAIR_K), :]
        vals.append(v_ref[0, :, pl.ds(start, NA_PAIR_K)])
        qp = q[:, u * NA_PAIR_Q:(u + 1) * NA_PAIR_Q]
        qz = jnp.concatenate([jnp.concatenate([qp[:HEAD_DIM], z], axis=0),
                              jnp.concatenate([z, qp[HEAD_DIM:]], axis=0)], axis=1)
        scores.append(_dot(ks, qz) + bias_ref[0, var])
    probs, sums = [], []
    for s in scores:
        m = jnp.max(s, axis=0, keepdims=True)
        p = jnp.exp(s - m)
        sums.append(jnp.sum(p, axis=0, keepdims=True))
        probs.append(p.astype(BF16))
    outs = []
    for vs, p, l in zip(vals, probs, sums):
        o = _dot(vs, p) / l
        outs.append(jnp.concatenate([o[:HEAD_DIM, :NA_PAIR_Q], o[HEAD_DIM:, NA_PAIR_Q:]], axis=0))
    o_ref[0] = jnp.concatenate(outs, axis=1).T.astype(BF16)


def _na_attn(pt, kc, bias):
    B, _, S = pt.shape
    nr = S // NA_Q
    return pl.pallas_call(
        _na_kernel,
        grid=(B, 2, nr),
        in_specs=[
            pl.BlockSpec((1, 128, NA_Q), lambda b, p, r: (b, 8 + p, r)),
            pl.BlockSpec((1, S, 128), lambda b, p, r: (b, 0, 4 + p)),
            pl.BlockSpec((1, 128, S), lambda b, p, r: (b, 10 + p, 0)),
            pl.BlockSpec((1, 5, NA_PAIR_K, 2 * NA_PAIR_Q), lambda b, p, r: (p, 0, 0, 0)),
        ],
        out_specs=pl.BlockSpec((1, NA_Q, 128), lambda b, p, r: (b, r, p)),
        out_shape=jax.ShapeDtypeStruct((B, S, B_HEADS * HEAD_DIM), BF16),
        compiler_params=_cparams(("parallel", "parallel", "arbitrary"), 48),
        name="na_attn",
    )(pt, kc, pt, bias)


def _na_bias_index():
    rows = 32
    a = np.arange(2)[None, :]
    i = np.arange(NA_PAIR_ROWS)[:, None]
    ridx, rvalid = [], []
    for mp in (0, 1, 5, rows // 2 - 2, rows // 2 - 1):
        srow = np.clip(2 * mp - NA_KH // 2, 0, rows - NA_PAIR_ROWS)
        qrow = 2 * mp + a
        krow = srow + i
        rs = np.clip(qrow - NA_KH // 2, 0, rows - NA_KH)
        rvalid.append((krow >= rs) & (krow < rs + NA_KH))
        ridx.append(np.clip(krow - qrow + NA_KH - 1, 0, 2 * NA_KH - 2))
    ridx = np.stack(ridx)
    rvalid = np.stack(rvalid)
    qc = np.arange(GRID_W)[None, :]
    kcol = np.arange(GRID_W)[:, None]
    cs = np.clip(qc - NA_KW // 2, 0, GRID_W - NA_KW)
    cvalid = (kcol >= cs) & (kcol < cs + NA_KW)
    cidx = np.clip(kcol - qc + NA_KW - 1, 0, 2 * NA_KW - 2)
    valid = rvalid[:, :, None, :, None] & cvalid[None, None, :, None, :]
    col_onehot = (cidx.reshape(-1)[None, :] == np.arange(2 * NA_KW - 1)[:, None])
    return (ridx.astype(np.int32), col_onehot.astype(np.float32),
            valid.reshape(5, NA_PAIR_K, NA_PAIR_Q))


_NA_RIDX, _NA_COL_ONEHOT, _NA_VALID = _na_bias_index()


def _na_bias_table(rpb):
    rows = jnp.take(rpb.astype(F32), jnp.asarray(_NA_RIDX), axis=1)
    tab = jnp.einsum("hvkqc,cn->hvkqn", rows, jnp.asarray(_NA_COL_ONEHOT),
                     precision=lax.Precision.HIGHEST)
    tab = tab.reshape(B_HEADS, 5, NA_PAIR_ROWS, 2, GRID_W, GRID_W).transpose(0, 1, 2, 4, 3, 5)
    tab = tab.reshape(B_HEADS, 5, NA_PAIR_K, NA_PAIR_Q)
    tab = jnp.where(jnp.asarray(_NA_VALID)[None], tab, NEG)
    tab = tab.reshape(2, 2, 5, NA_PAIR_K, NA_PAIR_Q).transpose(0, 2, 3, 1, 4)
    return tab.reshape(2, 5, NA_PAIR_K, 2 * NA_PAIR_Q)


def _win_kernel(sink_ref, q_ref, k_ref, v_ref, o_ref):
    S = k_ref.shape[1]
    i = pl.program_id(1)
    q0 = i * TQ_C
    start = pl.multiple_of(jnp.clip(q0 - WINDOW, 0, S - TK_C), 128)
    ks = k_ref[0, pl.ds(start, TK_C), :]
    vs = v_ref[0, :, pl.ds(start, TK_C)]
    kpos = start + lax.broadcasted_iota(jnp.int32, (TK_C, TQ_C), 0)
    qpos = q0 + lax.broadcasted_iota(jnp.int32, (TK_C, TQ_C), 1)
    mask = jnp.where(jnp.abs(qpos - kpos) <= WINDOW, 0.0, NEG).astype(F32)
    z = jnp.zeros((HEAD_DIM, TQ_C), BF16)
    gq = C_HEADS // C_KV_HEADS
    scores = []
    for hq in range(C_HEADS):
        qh = q_ref[0, hq * HEAD_DIM:(hq + 1) * HEAD_DIM, :]
        qz = jnp.concatenate([qh, z] if hq // gq == 0 else [z, qh], axis=0)
        scores.append(_dot(ks, qz) + mask)
    probs, dens = [], []
    for hq, s in enumerate(scores):
        sk = sink_ref[hq]
        m = jnp.maximum(jnp.max(s, axis=0, keepdims=True), sk)
        e = jnp.exp(s - m)
        dens.append(jnp.sum(e, axis=0, keepdims=True) + jnp.exp(sk - m))
        probs.append(e.astype(BF16))
    outs = []
    for hq, (e, den) in enumerate(zip(probs, dens)):
        kv = hq // gq
        outs.append(_dot(vs[kv * HEAD_DIM:(kv + 1) * HEAD_DIM], e) / den)
    o_ref[0] = jnp.concatenate(outs, axis=0).T.astype(BF16)


def _win_attn(pt, kc, sink):
    B, _, S = pt.shape
    return pl.pallas_call(
        _win_kernel,
        grid=(B, S // TQ_C),
        in_specs=[
            pl.BlockSpec(memory_space=pltpu.SMEM),
            pl.BlockSpec((1, 256, TQ_C), lambda b, i: (b, 6, i)),
            pl.BlockSpec((1, S, 128), lambda b, i: (b, 0, 6)),
            pl.BlockSpec((1, 128, S), lambda b, i: (b, 14, 0)),
        ],
        out_specs=pl.BlockSpec((1, TQ_C, 256), lambda b, i: (b, i, 0)),
        out_shape=jax.ShapeDtypeStruct((B, S, C_HEADS * HEAD_DIM), BF16),
        compiler_params=_cparams(("parallel", "arbitrary"), 48),
        name="win_attn",
    )(sink, pt, kc, pt)


def _outproj_kernel(x_ref, ya_ref, yb_ref, yc_ref, wo_ref, g2_ref, wr_ref, br_ref, tri_ref,
                    h_ref, n2_ref, ri_ref, rs_ref, pr_ref, cnt_ref):
    tm = x_ref.shape[0]
    h = (x_ref[...]
         + _dot(ya_ref[...], wo_ref[0:512, :])
         + _dot(yb_ref[...], wo_ref[512:768, :])
         + _dot(yc_ref[...], wo_ref[768:1024, :]))
    h_ref[...] = h
    ms = jnp.mean(h * h, axis=-1, keepdims=True)
    n2 = h * lax.rsqrt(ms + EPS) * g2_ref[...]
    n2_hi = n2.astype(BF16)
    n2_ref[...] = n2_hi
    n2_lo = (n2 - n2_hi.astype(F32)).astype(BF16)
    both = _dot(n2_hi, wr_ref[...])
    logits = both[:, :LANES] + both[:, LANES:] + _dot(n2_lo, wr_ref[:, :LANES]) + br_ref[...]
    lane = lax.broadcasted_iota(jnp.int32, logits.shape, 1).astype(F32)
    big = float(LANES)
    ninf = -jnp.inf

    def first_argmax(vals):
        vmax = jnp.max(vals, axis=-1, keepdims=True)
        idx = jnp.min(jnp.where(vals == vmax, lane, big), axis=-1, keepdims=True)
        return vmax, idx

    gl = jnp.where(lane < N_GROUPS, logits, ninf)
    gmax, gidx = first_argmax(gl)
    gp = 1.0 / jnp.sum(jnp.exp(gl - gmax), axis=-1, keepdims=True)
    lo = N_GROUPS + EXPERTS_PER_GROUP * gidx
    el = jnp.where((lane >= lo) & (lane < lo + EXPERTS_PER_GROUP), logits, ninf)
    v1, i1 = first_argmax(el)
    el2 = jnp.where(lane == i1, ninf, el)
    v2, i2 = first_argmax(el2)
    t = jnp.exp(v2 - v1)
    w1 = 1.0 / (1.0 + t)
    w2 = t / (1.0 + t)
    comb = jnp.where(lane == i1, gp * w1, jnp.where(lane == i2, gp * w2, 0.0))

    onehot = jnp.where(lane == gidx, 1.0, 0.0)
    before = _dot(tri_ref[...], onehot.astype(BF16))
    cnt = jnp.sum(onehot, axis=0, keepdims=True)
    pos = jnp.sum(jnp.where(lane < gidx, cnt, 0.0) + jnp.where(lane == gidx, before, 0.0),
                  axis=-1, keepdims=True)
    ri = jnp.where(lane == 0.0, pos, comb)
    ri_ref[...] = ri
    hi = ri.astype(BF16)
    r1 = ri - hi.astype(F32)
    mid = r1.astype(BF16)
    low = (r1 - mid.astype(F32)).astype(BF16)
    rs_ref[...] = jnp.concatenate([hi, mid, low], axis=1)
    pos_t = jnp.broadcast_to(pos, (tm, LANES)).T
    pr_ref[0] = pos_t[0:8, :]
    cnt_ref[0] = jnp.broadcast_to(cnt, (8, LANES)).astype(jnp.int32)


def _outproj(x2, ya, yb, yc, wo, g2, wr, br, tri):
    T = x2.shape[0]
    tm = T_MOE
    nt = T // tm
    row = lambda i: (i, 0)
    fixed = lambda i: (0, 0)
    return pl.pallas_call(
        _outproj_kernel,
        grid=(nt,),
        in_specs=[
            pl.BlockSpec((tm, D_MODEL), row),
            pl.BlockSpec((tm, 512), row),
            pl.BlockSpec((tm, 256), row),
            pl.BlockSpec((tm, 256), row),
            pl.BlockSpec((D_MODEL, D_MODEL), fixed),
            pl.BlockSpec((1, D_MODEL), fixed),
            pl.BlockSpec((D_MODEL, 2 * LANES), fixed),
            pl.BlockSpec((1, LANES), fixed),
            pl.BlockSpec((tm, tm), fixed),
        ],
        out_specs=[
            pl.BlockSpec((tm, D_MODEL), row),
            pl.BlockSpec((tm, D_MODEL), row),
            pl.BlockSpec((tm, LANES), row),
            pl.BlockSpec((tm, 3 * LANES), row),
            pl.BlockSpec((1, 8, tm), lambda i: (i, 0, 0)),
            pl.BlockSpec((1, 8, LANES), lambda i: (i, 0, 0)),
        ],
        out_shape=[
            jax.ShapeDtypeStruct((T, D_MODEL), F32),
            jax.ShapeDtypeStruct((T, D_MODEL), BF16),
            jax.ShapeDtypeStruct((T, LANES), F32),
            jax.ShapeDtypeStruct((T, 3 * LANES), BF16),
            jax.ShapeDtypeStruct((nt, 8, tm), F32),
            jax.ShapeDtypeStruct((nt, 8, LANES), jnp.int32),
        ],
        compiler_params=_cparams(("parallel",), 56),
        name="outproj",
    )(x2, ya, yb, yc, wo, g2, wr, br, tri)


def _moe_kernel(cnt_ref, x_ref, h_ref, ri_ref, rs_ref, pr_ref, wg_ref, wu_ref, wd_ref, o_ref,
                xs_ref, cs_ref, os_ref):
    tm = x_ref.shape[0]
    e = pl.program_id(1)
    t = pl.program_id(2)
    i = pl.program_id(0) * pl.num_programs(2) + t
    g = lax.shift_right_logical(e, 2)
    n = [cnt_ref[i * N_GROUPS + k] for k in range(N_GROUPS)]
    off = (jnp.where(g > 0, n[0], 0) + jnp.where(g > 1, n[1], 0) + jnp.where(g > 2, n[2], 0))
    ng = jnp.where(g == 0, n[0], jnp.where(g == 1, n[1], jnp.where(g == 2, n[2], n[3])))
    start = lax.shift_left(lax.shift_right_logical(off, 4), 4)
    need = off - start + ng
    nfull = jnp.maximum(lax.div(need - 1, MOE_SLAB), 0)
    rem_steps = lax.div(need - nfull * MOE_SLAB + (MOE_STEP - 1), MOE_STEP)

    @pl.when(e == 0)
    def _():
        want = lax.broadcasted_iota(jnp.int32, (tm, tm), 0).astype(F32)
        perm = jnp.where(pr_ref[0, 0:1, :] == want, 1.0, 0.0).astype(BF16)
        xs_ref[t, 0:tm, :] = _dot(perm, x_ref[...]).astype(BF16)
        c3 = _dot(perm, rs_ref[...])
        cs_ref[t, 0:tm, :] = c3[:, :LANES] + c3[:, LANES:2 * LANES] + c3[:, 2 * LANES:]
        xs_ref[t, tm:, :] = jnp.zeros((MOE_SLAB, D_MODEL), BF16)
        cs_ref[t, tm:, :] = jnp.zeros((MOE_SLAB, LANES), F32)
        os_ref[t] = jnp.zeros(os_ref.shape[1:], F32)

    def ffn_pass(r0, size):
        r0 = pl.multiple_of(r0, 16)
        xs = xs_ref[t, pl.ds(r0, size), :]
        hg = _dot(xs, wg_ref[0])
        hu = _dot(xs, wu_ref[0])
        cs = cs_ref[t, pl.ds(r0, size), :]
        lane = lax.broadcasted_iota(jnp.int32, cs.shape, 1)
        w = jnp.sum(jnp.where(lane == N_GROUPS + e, cs, 0.0), axis=-1, keepdims=True)
        hh = (hg * jax.nn.sigmoid(hg) * hu * w).astype(BF16)
        os_ref[t, pl.ds(r0, size), :] = os_ref[t, pl.ds(r0, size), :] + _dot(hh, wd_ref[0])

    def full_pass(s, carry):
        ffn_pass(start + s * MOE_SLAB, MOE_SLAB)
        return carry
    lax.fori_loop(0, nfull, full_pass, 0)

    for k in range(1, MOE_SLAB // MOE_STEP + 1):
        @pl.when(rem_steps == k)
        def _(k=k):
            ffn_pass(start + nfull * MOE_SLAB, k * MOE_STEP)

    @pl.when(e == N_EXPERTS - 1)
    def _():
        want = lax.broadcasted_iota(jnp.int32, (tm, tm), 1).astype(F32)
        perm_t = jnp.where(ri_ref[:, 0:1] == want, 1.0, 0.0).astype(BF16)
        o_ref[...] = h_ref[...] + _dot(perm_t, os_ref[t, 0:tm, :].astype(BF16))


def _moe(n2, h, ri, rs, pr, cnt, wg, wu, wd):
    T = n2.shape[0]
    tm = T_MOE
    nt = T // tm
    tpw = MOE_TILES_PER_WEIGHT if nt % MOE_TILES_PER_WEIGHT == 0 else 1
    rows = tm + MOE_SLAB
    last = N_EXPERTS - 1
    first_map = lambda s, e, t, c: (s * tpw + jnp.where(e == 0, t, tpw - 1), 0)
    last_map = lambda s, e, t, c: (s * tpw + jnp.where(e == last, t, 0), 0)
    wmap = lambda s, e, t, c: (e, 0, 0)
    return pl.pallas_call(
        _moe_kernel,
        grid_spec=pltpu.PrefetchScalarGridSpec(
            num_scalar_prefetch=1,
            grid=(nt // tpw, N_EXPERTS, tpw),
            in_specs=[
                pl.BlockSpec((tm, D_MODEL), first_map),
                pl.BlockSpec((tm, D_MODEL), last_map),
                pl.BlockSpec((tm, LANES), last_map),
                pl.BlockSpec((tm, 3 * LANES), first_map),
                pl.BlockSpec((1, 8, tm), lambda s, e, t, c: (first_map(s, e, t, c)[0], 0, 0)),
                pl.BlockSpec((1, D_MODEL, D_EXPERT), wmap),
                pl.BlockSpec((1, D_MODEL, D_EXPERT), wmap),
                pl.BlockSpec((1, D_EXPERT, D_MODEL), wmap),
            ],
            out_specs=pl.BlockSpec((tm, D_MODEL), last_map),
            scratch_shapes=[
                pltpu.VMEM((tpw, rows, D_MODEL), BF16),
                pltpu.VMEM((tpw, rows, LANES), F32),
                pltpu.VMEM((tpw, rows, D_MODEL), F32),
            ],
        ),
        out_shape=jax.ShapeDtypeStruct((T, D_MODEL), F32),
        compiler_params=_cparams(("parallel", "arbitrary", "arbitrary"), 60),
        name="moe",
    )(cnt, n2, h, ri, rs, pr, wg, wu, wd)


def _rope_tables_t(seq_len):
    pos = jnp.arange(seq_len, dtype=F32)
    inv = ROPE_THETA ** (-jnp.arange(0, HEAD_DIM, 2, dtype=F32) / HEAD_DIM)
    ang = pos[:, None] * inv[None, :]
    ang = jnp.concatenate([ang, ang], axis=-1)
    sign = jnp.where(jnp.arange(HEAD_DIM) < HEAD_DIM // 2, -1.0, 1.0).astype(F32)
    return jnp.cos(ang).T, (jnp.sin(ang) * sign[None, :]).T


def _lane_bcast(v):
    return jnp.broadcast_to(v.astype(F32)[:, None], (v.shape[0], LANES))


def _prep_layer(l, p):
    w_in = p["w_in"][l]
    cols = [w_in[:, _COLS[n][0]:_COLS[n][1]] for n in _PT_ORDER + _KT_ORDER]
    w_t = jnp.concatenate(cols, axis=1).T.astype(BF16)
    gains = jnp.stack([
        _lane_bcast(p["a_q_norm"][l] * (QK_SCALE * LOG2E)), _lane_bcast(p["a_k_norm"][l]),
        _lane_bcast(p["b_q_norm"][l] * QK_SCALE), _lane_bcast(p["b_k_norm"][l]),
        _lane_bcast(p["c_q_norm"][l] * QK_SCALE), _lane_bcast(p["c_k_norm"][l]),
    ])
    wr = jnp.zeros((D_MODEL, LANES), F32)
    wr = wr.at[:, :N_GROUPS].set(p["w_group"][l]).at[:, N_GROUPS:N_GROUPS + N_EXPERTS].set(p["w_router"][l])
    wr_hi = wr.astype(BF16)
    wr_lo = (wr - wr_hi.astype(F32)).astype(BF16)
    wr2 = jnp.concatenate([wr_hi, wr_lo], axis=1)
    br = jnp.zeros((1, LANES), F32)
    br = br.at[0, :N_GROUPS].set(p["b_group"][l]).at[0, N_GROUPS:N_GROUPS + N_EXPERTS].set(p["b_router"][l])
    bound = 1.02 * HEAD_DIM * QK_SCALE * jnp.max(jnp.abs(p["a_q_norm"][l])) * jnp.max(jnp.abs(p["a_k_norm"][l]))
    return dict(
        g1=p["norm1_g"][l].astype(F32)[None, :], w_t=w_t, gains=gains,
        lam=jnp.stack([p["a_lambda_q1"][l], p["a_lambda_k1"][l],
                       p["a_lambda_q2"][l], p["a_lambda_k2"][l]]).astype(F32),
        subg=_lane_bcast(p["a_subln"][l]),
        na_bias=_na_bias_table(p["b_rpb"][l]),
        sink=p["c_sink"][l].astype(F32),
        wo=p["w_out"][l].astype(BF16), g2=p["norm2_g"][l].astype(F32)[None, :],
        wr=wr2, br=br,
        wg=p["w_gate"][l].astype(BF16), wu=p["w_up"][l].astype(BF16), wd=p["w_down"][l].astype(BF16),
        fast_ok=bound <= FAST_SOFTMAX_BOUND,
    )


def _layer(x, lp, lambda_init, cos_t, sin_t):
    B, S, _ = x.shape
    pt, kc = _inproj(x, lp["g1"], lp["w_t"], lp["gains"], cos_t, sin_t)
    ya = lax.cond(
        lp["fast_ok"],
        lambda a, b: _diff_attn(a, b, lp["lam"], lp["subg"], lambda_init, False),
        lambda a, b: _diff_attn(a, b, lp["lam"], lp["subg"], lambda_init, True),
        pt, kc)
    yb = _na_attn(pt, kc, lp["na_bias"])
    yc = _win_attn(pt, kc, lp["sink"])
    T = B * S
    tri = jnp.tril(jnp.ones((T_MOE, T_MOE), BF16), -1)
    h, n2, ri, rs, pr, cnt = _outproj(x.reshape(T, D_MODEL), ya.reshape(T, -1), yb.reshape(T, -1),
                                      yc.reshape(T, -1), lp["wo"], lp["g2"], lp["wr"], lp["br"], tri)
    counts = cnt[:, 0, :N_GROUPS].reshape(-1)
    y = _moe(n2, h, ri, rs, pr, counts, lp["wg"], lp["wu"], lp["wd"])
    return y.reshape(B, S, D_MODEL)


def kernel(x_prompt, x_sample, norm1_g, w_in, w_out, a_q_norm, a_k_norm, a_lambda_q1, a_lambda_k1,
           a_lambda_q2, a_lambda_k2, a_subln, b_q_norm, b_k_norm, b_rpb, c_q_norm, c_k_norm, c_sink,
           norm2_g, w_group, b_group, w_router, b_router, w_gate, w_up, w_down):
    p = dict(norm1_g=norm1_g, w_in=w_in, w_out=w_out, a_q_norm=a_q_norm, a_k_norm=a_k_norm,
             a_lambda_q1=a_lambda_q1, a_lambda_k1=a_lambda_k1, a_lambda_q2=a_lambda_q2,
             a_lambda_k2=a_lambda_k2, a_subln=a_subln, b_q_norm=b_q_norm, b_k_norm=b_k_norm,
             b_rpb=b_rpb, c_q_norm=c_q_norm, c_k_norm=c_k_norm, c_sink=c_sink, norm2_g=norm2_g,
             w_group=w_group, b_group=b_group, w_router=w_router, b_router=b_router,
             w_gate=w_gate, w_up=w_up, w_down=w_down)
    layers = [_prep_layer(l, p) for l in range(DEPTH)]

    def run(x):
        cos_t, sin_t = _rope_tables_t(x.shape[1])
        for l in range(DEPTH):
            lambda_init = 0.8 - 0.6 * math.exp(-0.3 * l)
            x = _layer(x, layers[l], lambda_init, cos_t, sin_t)
        return x

    return (run(x_prompt), run(x_sample))
```

```python
import functools
import math

import numpy as np
import jax
import jax.numpy as jnp
from jax import lax
from jax.experimental import pallas as pl
from jax.experimental.pallas import tpu as pltpu

F32 = jnp.float32
BF16 = jnp.bfloat16

D_MODEL = 1024
DEPTH = 2
HEAD_DIM = 64
GRID_W = 64
EPS = 1e-6
ROPE_THETA = 10000.0
A_HEADS = 4
A_VDIM = 2 * HEAD_DIM
B_HEADS = 4
NA_KH = 8
NA_KW = 16
C_HEADS = 4
C_KV_HEADS = 2
WINDOW = 128
N_GROUPS = 4
EXPERTS_PER_GROUP = 4
N_EXPERTS = 16
D_EXPERT = 512
QK_SCALE = HEAD_DIM ** -0.5
LOG2E = math.log2(math.e)

_COLS = dict(aq=(0, 512), ak=(512, 1024), av=(1024, 1536), bq=(1536, 1792), bk=(1792, 2048),
             bv=(2048, 2304), cq=(2304, 2560), ck=(2560, 2688), cv=(2688, 2816))
_PT_ORDER = ("aq", "av", "bq", "bv", "cq", "cv")
_KT_ORDER = ("ak", "bk", "ck")
PT_ROWS = 1920
KC_COLS = 896
IN_WIDTH = 2816

LANES = 128
NEG = -1e30

TM_PROJ = 1024
TQ_A = 512
TK_A = 2048
UNROLL_A = 2
NA_Q = 512
NA_PAIR_Q = 2 * GRID_W
NA_PAIR_ROWS = NA_KH + 2
NA_PAIR_K = NA_PAIR_ROWS * GRID_W
TQ_C = 256
TK_C = TQ_C + 2 * WINDOW
T_MOE = 1024
MOE_SLAB = 512
MOE_STEP = 64
MOE_PAD = 128
MOE_TILES_PER_WEIGHT = 2
FAST_SOFTMAX_BOUND = 30.0


def _cparams(sem, vmem_mb):
    return pltpu.CompilerParams(dimension_semantics=sem, vmem_limit_bytes=vmem_mb * 1024 * 1024)


def _nt_dot(a, b):
    return lax.dot_general(a, b, (((1,), (1,)), ((), ())), preferred_element_type=F32)


def _dot(a, b):
    return jnp.dot(a, b, preferred_element_type=F32)


def _inproj_kernel(x_ref, g1_ref, w_ref, gains_ref, cos_ref, sin_ref, pt_ref, kc_ref):
    tm = x_ref.shape[1]
    x = x_ref[0]
    ms = jnp.mean(x * x, axis=-1, keepdims=True)
    n = (x * lax.rsqrt(ms + EPS) * g1_ref[...]).astype(BF16)
    cos = cos_ref[...]
    sin = sin_ref[...]
    reps = tm // LANES

    def project(r0, r1):
        return _nt_dot(w_ref[r0:r1, :], n)

    def head(chunk, r0, gi, rope):
        xh = chunk[r0:r0 + HEAD_DIM, :]
        ssq = jnp.sum(xh * xh, axis=0, keepdims=True)
        gain = pltpu.repeat(gains_ref[gi], reps, axis=1)
        xg = xh * lax.rsqrt(ssq * (1.0 / HEAD_DIM) + EPS) * gain
        if rope:
            swapped = jnp.concatenate([xg[HEAD_DIM // 2:], xg[:HEAD_DIM // 2]], axis=0)
            xg = xg * cos + swapped * sin
        return xg

    def put(r0, val):
        pt_ref[0, r0:r0 + val.shape[0], :] = val.astype(BF16)

    c = project(0, 512)
    for j in range(8):
        put(j * 64, head(c, j * 64, 0, True))
    put(512, project(512, 1024))
    c = project(1024, 1536)
    for j in range(4):
        put(1024 + j * 64, head(c, j * 64, 2, False))
    put(1280, c[256:512, :])
    c = project(1536, 1920)
    for j in range(4):
        put(1536 + j * 64, head(c, j * 64, 4, True))
    put(1792, c[256:384, :])
    def put_keys(chunk, j0, specs):
        for j, (gi, rope) in enumerate(specs):
            blk = jnp.concatenate([head(chunk, j * 128, gi, rope),
                                   head(chunk, j * 128 + 64, gi, rope)], axis=0)
            kc_ref[0, :, (j0 + j) * 128:(j0 + j + 1) * 128] = blk.T.astype(BF16)

    put_keys(project(PT_ROWS, PT_ROWS + 512), 0, [(1, True)] * 4)
    put_keys(project(PT_ROWS + 512, IN_WIDTH), 4, [(3, False)] * 2 + [(5, True)])


def _inproj(x, g1, w_t, gains, cos_t, sin_t):
    B, S, _ = x.shape
    tm = TM_PROJ
    return pl.pallas_call(
        _inproj_kernel,
        grid=(B, S // tm),
        in_specs=[
            pl.BlockSpec((1, tm, D_MODEL), lambda b, i: (b, i, 0)),
            pl.BlockSpec((1, D_MODEL), lambda b, i: (0, 0)),
            pl.BlockSpec((IN_WIDTH, D_MODEL), lambda b, i: (0, 0)),
            pl.BlockSpec((6, HEAD_DIM, LANES), lambda b, i: (0, 0, 0)),
            pl.BlockSpec((HEAD_DIM, tm), lambda b, i: (0, i)),
            pl.BlockSpec((HEAD_DIM, tm), lambda b, i: (0, i)),
        ],
        out_specs=[
            pl.BlockSpec((1, PT_ROWS, tm), lambda b, i: (b, 0, i)),
            pl.BlockSpec((1, tm, KC_COLS), lambda b, i: (b, i, 0)),
        ],
        out_shape=[
            jax.ShapeDtypeStruct((B, PT_ROWS, S), BF16),
            jax.ShapeDtypeStruct((B, S, KC_COLS), BF16),
        ],
        compiler_params=_cparams(("parallel", "parallel"), 56),
        name="inproj",
    )(x, g1, w_t, gains, cos_t, sin_t)


def _diff_attn_kernel(q_ref, k_ref, v_ref, lam_ref, subg_ref, o_ref, qz_ref, acc_ref, l_ref, m_ref,
                      *, tk, safe, lambda_init):
    tq = q_ref.shape[2]
    S = k_ref.shape[1]
    q = q_ref[0]
    z = jnp.zeros((HEAD_DIM, tq), BF16)
    qz_ref[:, :tq] = jnp.concatenate([q[:HEAD_DIM], z], axis=0)
    qz_ref[:, tq:] = jnp.concatenate([z, q[HEAD_DIM:]], axis=0)
    acc_ref[...] = jnp.zeros_like(acc_ref)
    l_ref[...] = jnp.zeros_like(l_ref)
    if safe:
        m_ref[...] = jnp.full_like(m_ref, NEG)

    def body(i, carry):
        off = pl.multiple_of(i * tk, tk)
        kc = k_ref[0, pl.ds(off, tk), :]
        vc = v_ref[0, :, pl.ds(off, tk)]
        s = _dot(kc, qz_ref[...])
        if safe:
            m_old = m_ref[...]
            m_new = jnp.maximum(m_old, jnp.max(s, axis=0, keepdims=True))
            alpha = jnp.exp2(m_old - m_new)
            p = jnp.exp2(s - m_new)
            m_ref[...] = m_new
            l_ref[...] = alpha * l_ref[...] + jnp.sum(p, axis=0, keepdims=True)
            acc_ref[...] = alpha * acc_ref[...] + _dot(vc, p.astype(BF16))
        else:
            p = jnp.exp2(s)
            l_ref[...] = l_ref[...] + jnp.sum(p, axis=0, keepdims=True)
            acc_ref[...] = acc_ref[...] + _dot(vc, p.astype(BF16))
        return carry

    lax.fori_loop(0, S // tk, body, 0, unroll=UNROLL_A)

    lp = lam_ref[...]
    lam = (jnp.exp(jnp.sum(lp[0:1] * lp[1:2], axis=-1, keepdims=True))
           - jnp.exp(jnp.sum(lp[2:3] * lp[3:4], axis=-1, keepdims=True)) + lambda_init)
    on = acc_ref[...] / l_ref[...]
    o = on[:, :tq] - lam * on[:, tq:]
    ms = jnp.mean(o * o, axis=0, keepdims=True)
    gain = pltpu.repeat(subg_ref[...], tq // LANES, axis=1)
    y = o * lax.rsqrt(ms + EPS) * gain * (1.0 - lambda_init)
    o_ref[0] = y.T.astype(BF16)


def _diff_attn(pt, kc, lam_params, subg, lambda_init, safe):
    B, _, S = pt.shape
    tq, tk = TQ_A, min(TK_A, S)
    kern = functools.partial(_diff_attn_kernel, tk=tk, safe=safe, lambda_init=lambda_init)
    return pl.pallas_call(
        kern,
        grid=(B, A_HEADS, S // tq),
        in_specs=[
            pl.BlockSpec((1, 128, tq), lambda b, h, i: (b, h, i)),
            pl.BlockSpec((1, S, 128), lambda b, h, i: (b, 0, h)),
            pl.BlockSpec((1, 128, S), lambda b, h, i: (b, 4 + h, 0)),
            pl.BlockSpec((4, HEAD_DIM), lambda b, h, i: (0, 0)),
            pl.BlockSpec((A_VDIM, LANES), lambda b, h, i: (0, 0)),
        ],
        out_specs=pl.BlockSpec((1, tq, 128), lambda b, h, i: (b, i, h)),
        out_shape=jax.ShapeDtypeStruct((B, S, A_HEADS * A_VDIM), BF16),
        scratch_shapes=[
            pltpu.VMEM((128, 2 * tq), BF16),
            pltpu.VMEM((A_VDIM, 2 * tq), F32),
            pltpu.VMEM((1, 2 * tq), F32),
            pltpu.VMEM((1, 2 * tq), F32),
        ],
        compiler_params=_cparams(("parallel", "parallel", "arbitrary"), 48),
        name="diff_attn_safe" if safe else "diff_attn",
    )(pt, kc, pt, lam_params, subg)


def _na_kernel(q_ref, k_ref, v_ref, bias_ref, o_ref):
    S = k_ref.shape[1]
    rows = S // GRID_W
    r = pl.program_id(2)
    q = q_ref[0]
    z = jnp.zeros((HEAD_DIM, NA_PAIR_Q), BF16)
    npairs = NA_Q // NA_PAIR_Q
    scores, vals = [], []
    for u in range(npairs):
        mp = r * npairs + u
        srow = jnp.clip(2 * mp - NA_KH // 2, 0, rows - NA_PAIR_ROWS)
        start = pl.multiple_of(srow * GRID_W, 128)
        var = jnp.where(mp == 0, 0, jnp.where(mp == 1, 1, jnp.where(
            mp == rows // 2 - 2, 3, jnp.where(mp == rows // 2 - 1, 4, 2))))
        ks = k_ref[0, pl.ds(start, NA_PAIR_K), :]
        vals.append(v_ref[0, :, pl.ds(start, NA_PAIR_K)])
        qp = q[:, u * NA_PAIR_Q:(u + 1) * NA_PAIR_Q]
        qz = jnp.concatenate([jnp.concatenate([qp[:HEAD_DIM], z], axis=0),
                              jnp.concatenate([z, qp[HEAD_DIM:]], axis=0)], axis=1)
        scores.append(_dot(ks, qz) + bias_ref[0, var])
    probs, sums = [], []
    for s in scores:
        m = jnp.max(s, axis=0, keepdims=True)
        p = jnp.exp(s - m)
        sums.append(jnp.sum(p, axis=0, keepdims=True))
        probs.append(p.astype(BF16))
    outs = []
    for vs, p, l in zip(vals, probs, sums):
        o = _dot(vs, p) / l
        outs.append(jnp.concatenate([o[:HEAD_DIM, :NA_PAIR_Q], o[HEAD_DIM:, NA_PAIR_Q:]], axis=0))
    o_ref[0] = jnp.concatenate(outs, axis=1).T.astype(BF16)


def _na_attn(pt, kc, bias):
    B, _, S = pt.shape
    nr = S // NA_Q
    return pl.pallas_call(
        _na_kernel,
        grid=(B, 2, nr),
        in_specs=[
            pl.BlockSpec((1, 128, NA_Q), lambda b, p, r: (b, 8 + p, r)),
            pl.BlockSpec((1, S, 128), lambda b, p, r: (b, 0, 4 + p)),
            pl.BlockSpec((1, 128, S), lambda b, p, r: (b, 10 + p, 0)),
            pl.BlockSpec((1, 5, NA_PAIR_K, 2 * NA_PAIR_Q), lambda b, p, r: (p, 0, 0, 0)),
        ],
        out_specs=pl.BlockSpec((1, NA_Q, 128), lambda b, p, r: (b, r, p)),
        out_shape=jax.ShapeDtypeStruct((B, S, B_HEADS * HEAD_DIM), BF16),
        compiler_params=_cparams(("parallel", "parallel", "arbitrary"), 48),
        name="na_attn",
    )(pt, kc, pt, bias)


def _na_bias_index():
    rows = 32
    a = np.arange(2)[None, :]
    i = np.arange(NA_PAIR_ROWS)[:, None]
    ridx, rvalid = [], []
    for mp in (0, 1, 5, rows // 2 - 2, rows // 2 - 1):
        srow = np.clip(2 * mp - NA_KH // 2, 0, rows - NA_PAIR_ROWS)
        qrow = 2 * mp + a
        krow = srow + i
        rs = np.clip(qrow - NA_KH // 2, 0, rows - NA_KH)
        rvalid.append((krow >= rs) & (krow < rs + NA_KH))
        ridx.append(np.clip(krow - qrow + NA_KH - 1, 0, 2 * NA_KH - 2))
    ridx = np.stack(ridx)
    rvalid = np.stack(rvalid)
    qc = np.arange(GRID_W)[None, :]
    kcol = np.arange(GRID_W)[:, None]
    cs = np.clip(qc - NA_KW // 2, 0, GRID_W - NA_KW)
    cvalid = (kcol >= cs) & (kcol < cs + NA_KW)
    cidx = np.clip(kcol - qc + NA_KW - 1, 0, 2 * NA_KW - 2)
    valid = rvalid[:, :, None, :, None] & cvalid[None, None, :, None, :]
    col_onehot = (cidx.reshape(-1)[None, :] == np.arange(2 * NA_KW - 1)[:, None])
    return (ridx.astype(np.int32), col_onehot.astype(np.float32),
            valid.reshape(5, NA_PAIR_K, NA_PAIR_Q))


_NA_RIDX, _NA_COL_ONEHOT, _NA_VALID = _na_bias_index()


def _na_bias_table(rpb):
    rows = jnp.take(rpb.astype(F32), jnp.asarray(_NA_RIDX), axis=1)
    tab = jnp.einsum("hvkqc,cn->hvkqn", rows, jnp.asarray(_NA_COL_ONEHOT),
                     precision=lax.Precision.HIGHEST)
    tab = tab.reshape(B_HEADS, 5, NA_PAIR_ROWS, 2, GRID_W, GRID_W).transpose(0, 1, 2, 4, 3, 5)
    tab = tab.reshape(B_HEADS, 5, NA_PAIR_K, NA_PAIR_Q)
    tab = jnp.where(jnp.asarray(_NA_VALID)[None], tab, NEG)
    tab = tab.reshape(2, 2, 5, NA_PAIR_K, NA_PAIR_Q).transpose(0, 2, 3, 1, 4)
    return tab.reshape(2, 5, NA_PAIR_K, 2 * NA_PAIR_Q)


def _win_kernel(sink_ref, q_ref, k_ref, v_ref, o_ref):
    S = k_ref.shape[1]
    i = pl.program_id(1)
    q0 = i * TQ_C
    start = pl.multiple_of(jnp.clip(q0 - WINDOW, 0, S - TK_C), 128)
    ks = k_ref[0, pl.ds(start, TK_C), :]
    vs = v_ref[0, :, pl.ds(start, TK_C)]
    kpos = start + lax.broadcasted_iota(jnp.int32, (TK_C, TQ_C), 0)
    qpos = q0 + lax.broadcasted_iota(jnp.int32, (TK_C, TQ_C), 1)
    mask = jnp.where(jnp.abs(qpos - kpos) <= WINDOW, 0.0, NEG).astype(F32)
    z = jnp.zeros((HEAD_DIM, TQ_C), BF16)
    gq = C_HEADS // C_KV_HEADS
    scores = []
    for hq in range(C_HEADS):
        qh = q_ref[0, hq * HEAD_DIM:(hq + 1) * HEAD_DIM, :]
        qz = jnp.concatenate([qh, z] if hq // gq == 0 else [z, qh], axis=0)
        scores.append(_dot(ks, qz) + mask)
    probs, dens = [], []
    for hq, s in enumerate(scores):
        sk = sink_ref[hq]
        m = jnp.maximum(jnp.max(s, axis=0, keepdims=True), sk)
        e = jnp.exp(s - m)
        dens.append(jnp.sum(e, axis=0, keepdims=True) + jnp.exp(sk - m))
        probs.append(e.astype(BF16))
    outs = []
    for hq, (e, den) in enumerate(zip(probs, dens)):
        kv = hq // gq
        outs.append(_dot(vs[kv * HEAD_DIM:(kv + 1) * HEAD_DIM], e) / den)
    o_ref[0] = jnp.concatenate(outs, axis=0).T.astype(BF16)


def _win_attn(pt, kc, sink):
    B, _, S = pt.shape
    return pl.pallas_call(
        _win_kernel,
        grid=(B, S // TQ_C),
        in_specs=[
            pl.BlockSpec(memory_space=pltpu.SMEM),
            pl.BlockSpec((1, 256, TQ_C), lambda b, i: (b, 6, i)),
            pl.BlockSpec((1, S, 128), lambda b, i: (b, 0, 6)),
            pl.BlockSpec((1, 128, S), lambda b, i: (b, 14, 0)),
        ],
        out_specs=pl.BlockSpec((1, TQ_C, 256), lambda b, i: (b, i, 0)),
        out_shape=jax.ShapeDtypeStruct((B, S, C_HEADS * HEAD_DIM), BF16),
        compiler_params=_cparams(("parallel", "arbitrary"), 48),
        name="win_attn",
    )(sink, pt, kc, pt)


def _outproj_kernel(x_ref, ya_ref, yb_ref, yc_ref, wo_ref, g2_ref, wr_ref, br_ref, tri_ref,
                    h_ref, xs_ref, cs_ref, ri_ref, cnt_ref):
    tm = x_ref.shape[0]
    h = (x_ref[...]
         + _dot(ya_ref[...], wo_ref[0:512, :])
         + _dot(yb_ref[...], wo_ref[512:768, :])
         + _dot(yc_ref[...], wo_ref[768:1024, :]))
    h_ref[...] = h
    ms = jnp.mean(h * h, axis=-1, keepdims=True)
    n2 = h * lax.rsqrt(ms + EPS) * g2_ref[...]
    n2_hi = n2.astype(BF16)
    n2_lo = (n2 - n2_hi.astype(F32)).astype(BF16)
    both = _dot(n2_hi, wr_ref[...])
    logits = both[:, :LANES] + both[:, LANES:] + _dot(n2_lo, wr_ref[:, :LANES]) + br_ref[...]
    lane = lax.broadcasted_iota(jnp.int32, logits.shape, 1).astype(F32)
    big = float(LANES)
    ninf = -jnp.inf

    def first_argmax(vals):
        vmax = jnp.max(vals, axis=-1, keepdims=True)
        idx = jnp.min(jnp.where(vals == vmax, lane, big), axis=-1, keepdims=True)
        return vmax, idx

    gl = jnp.where(lane < N_GROUPS, logits, ninf)
    gmax, gidx = first_argmax(gl)
    gp = 1.0 / jnp.sum(jnp.exp(gl - gmax), axis=-1, keepdims=True)
    lo = N_GROUPS + EXPERTS_PER_GROUP * gidx
    el = jnp.where((lane >= lo) & (lane < lo + EXPERTS_PER_GROUP), logits, ninf)
    v1, i1 = first_argmax(el)
    el2 = jnp.where(lane == i1, ninf, el)
    v2, i2 = first_argmax(el2)
    t = jnp.exp(v2 - v1)
    w1 = 1.0 / (1.0 + t)
    w2 = t / (1.0 + t)
    comb = jnp.where(lane == i1, gp * w1, jnp.where(lane == i2, gp * w2, 0.0))

    onehot = jnp.where(lane == gidx, 1.0, 0.0)
    before = _dot(tri_ref[...], onehot.astype(BF16))
    cnt = jnp.sum(onehot, axis=0, keepdims=True)
    pos = jnp.sum(jnp.where(lane < gidx, cnt, 0.0) + jnp.where(lane == gidx, before, 0.0),
                  axis=-1, keepdims=True)
    ri = jnp.where(lane == 0.0, pos, comb)
    ri_ref[...] = ri
    hi = ri.astype(BF16)
    r1 = ri - hi.astype(F32)
    mid = r1.astype(BF16)
    low = (r1 - mid.astype(F32)).astype(BF16)
    rs = jnp.concatenate([hi, mid, low], axis=1)
    pos_t = jnp.broadcast_to(pos, (tm, LANES)).T
    want = lax.broadcasted_iota(jnp.int32, (tm, tm), 0).astype(F32)
    perm = jnp.where(pos_t[0:1, :] == want, 1.0, 0.0).astype(BF16)
    xs_ref[0, 0:tm, :] = _dot(perm, n2_hi).astype(BF16)
    c3 = _dot(perm, rs)
    cs_ref[0, 0:tm, :] = c3[:, :LANES] + c3[:, LANES:2 * LANES] + c3[:, 2 * LANES:]
    xs_ref[0, tm:, :] = jnp.zeros((MOE_PAD, D_MODEL), BF16)
    cs_ref[0, tm:, :] = jnp.zeros((MOE_PAD, LANES), F32)
    cnt_ref[0] = jnp.broadcast_to(cnt, (8, LANES)).astype(jnp.int32)


def _outproj(x2, ya, yb, yc, wo, g2, wr, br, tri):
    T = x2.shape[0]
    tm = T_MOE
    nt = T // tm
    rows = tm + MOE_PAD
    row = lambda i: (i, 0)
    fixed = lambda i: (0, 0)
    return pl.pallas_call(
        _outproj_kernel,
        grid=(nt,),
        in_specs=[
            pl.BlockSpec((tm, D_MODEL), row),
            pl.BlockSpec((tm, 512), row),
            pl.BlockSpec((tm, 256), row),
            pl.BlockSpec((tm, 256), row),
            pl.BlockSpec((D_MODEL, D_MODEL), fixed),
            pl.BlockSpec((1, D_MODEL), fixed),
            pl.BlockSpec((D_MODEL, 2 * LANES), fixed),
            pl.BlockSpec((1, LANES), fixed),
            pl.BlockSpec((tm, tm), fixed),
        ],
        out_specs=[
            pl.BlockSpec((tm, D_MODEL), row),
            pl.BlockSpec((1, rows, D_MODEL), lambda i: (i, 0, 0)),
            pl.BlockSpec((1, rows, LANES), lambda i: (i, 0, 0)),
            pl.BlockSpec((tm, LANES), row),
            pl.BlockSpec((1, 8, LANES), lambda i: (i, 0, 0)),
        ],
        out_shape=[
            jax.ShapeDtypeStruct((T, D_MODEL), F32),
            jax.ShapeDtypeStruct((nt, rows, D_MODEL), BF16),
            jax.ShapeDtypeStruct((nt, rows, LANES), F32),
            jax.ShapeDtypeStruct((T, LANES), F32),
            jax.ShapeDtypeStruct((nt, 8, LANES), jnp.int32),
        ],
        compiler_params=_cparams(("parallel",), 56),
        name="outproj",
    )(x2, ya, yb, yc, wo, g2, wr, br, tri)


def _moe_kernel(cnt_ref, xs_ref, cs_ref, wg_ref, wu_ref, wd_ref, o_ref, acc_ref):
    tiles = xs_ref.shape[0]
    e = pl.program_id(1)
    g = lax.shift_right_logical(e, 2)

    @pl.when(e == 0)
    def _():
        acc_ref[...] = jnp.zeros_like(acc_ref)

    def ffn_pass(t, r0, size):
        r0 = pl.multiple_of(r0, 16)
        xs = xs_ref[t, pl.ds(r0, size), :]
        hg = _dot(xs, wg_ref[0])
        hu = _dot(xs, wu_ref[0])
        cs = cs_ref[t, pl.ds(r0, size), :]
        lane = lax.broadcasted_iota(jnp.int32, cs.shape, 1)
        w = jnp.sum(jnp.where(lane == N_GROUPS + e, cs, 0.0), axis=-1, keepdims=True)
        hh = (hg * jax.nn.sigmoid(hg) * hu * w).astype(BF16)
        acc_ref[t, pl.ds(r0, size), :] = acc_ref[t, pl.ds(r0, size), :] + _dot(hh, wd_ref[0])

    for t in range(tiles):
        i = pl.program_id(0) * tiles + t
        n = [cnt_ref[i * N_GROUPS + k] for k in range(N_GROUPS)]
        off = (jnp.where(g > 0, n[0], 0) + jnp.where(g > 1, n[1], 0) + jnp.where(g > 2, n[2], 0))
        ng = jnp.where(g == 0, n[0], jnp.where(g == 1, n[1], jnp.where(g == 2, n[2], n[3])))
        start = lax.shift_left(lax.shift_right_logical(off, 4), 4)
        need = off - start + ng
        nfull = jnp.maximum(lax.div(need - 1, MOE_SLAB), 0)
        rem_steps = lax.div(need - nfull * MOE_SLAB + (MOE_STEP - 1), MOE_STEP)

        def full_pass(s, carry, t=t, start=start):
            ffn_pass(t, start + s * MOE_SLAB, MOE_SLAB)
            return carry
        lax.fori_loop(0, nfull, full_pass, 0)

        for k in range(1, MOE_SLAB // MOE_STEP + 1):
            @pl.when(rem_steps == k)
            def _(k=k, t=t, start=start, nfull=nfull):
                ffn_pass(t, start + nfull * MOE_SLAB, k * MOE_STEP)

    @pl.when(e == N_EXPERTS - 1)
    def _():
        o_ref[...] = acc_ref[...].astype(BF16)


def _moe(xs, cs, cnt, wg, wu, wd):
    nt, rows, _ = xs.shape
    tpw = MOE_TILES_PER_WEIGHT if nt % MOE_TILES_PER_WEIGHT == 0 else 1
    tmap = lambda s, e, c: (s, 0, 0)
    wmap = lambda s, e, c: (e, 0, 0)
    return pl.pallas_call(
        _moe_kernel,
        grid_spec=pltpu.PrefetchScalarGridSpec(
            num_scalar_prefetch=1,
            grid=(nt // tpw, N_EXPERTS),
            in_specs=[
                pl.BlockSpec((tpw, rows, D_MODEL), tmap),
                pl.BlockSpec((tpw, rows, LANES), tmap),
                pl.BlockSpec((1, D_MODEL, D_EXPERT), wmap),
                pl.BlockSpec((1, D_MODEL, D_EXPERT), wmap),
                pl.BlockSpec((1, D_EXPERT, D_MODEL), wmap),
            ],
            out_specs=pl.BlockSpec((tpw, rows, D_MODEL), tmap),
            scratch_shapes=[pltpu.VMEM((tpw, rows, D_MODEL), F32)],
        ),
        out_shape=jax.ShapeDtypeStruct((nt, rows, D_MODEL), BF16),
        compiler_params=_cparams(("parallel", "arbitrary"), 56),
        name="moe",
    )(cnt, xs, cs, wg, wu, wd)


def _unsort_kernel(os_ref, h_ref, ri_ref, o_ref):
    tm = h_ref.shape[0]
    want = lax.broadcasted_iota(jnp.int32, (tm, tm), 1).astype(F32)
    perm_t = jnp.where(ri_ref[:, 0:1] == want, 1.0, 0.0).astype(BF16)
    o_ref[...] = h_ref[...] + _dot(perm_t, os_ref[0, 0:tm, :])


def _unsort(os, h, ri):
    T = h.shape[0]
    tm = T_MOE
    rows = os.shape[1]
    row = lambda i: (i, 0)
    return pl.pallas_call(
        _unsort_kernel,
        grid=(T // tm,),
        in_specs=[
            pl.BlockSpec((1, rows, D_MODEL), lambda i: (i, 0, 0)),
            pl.BlockSpec((tm, D_MODEL), row),
            pl.BlockSpec((tm, LANES), row),
        ],
        out_specs=pl.BlockSpec((tm, D_MODEL), row),
        out_shape=jax.ShapeDtypeStruct((T, D_MODEL), F32),
        compiler_params=_cparams(("parallel",), 48),
        name="unsort",
    )(os, h, ri)


def _rope_tables_t(seq_len):
    pos = jnp.arange(seq_len, dtype=F32)
    inv = ROPE_THETA ** (-jnp.arange(0, HEAD_DIM, 2, dtype=F32) / HEAD_DIM)
    ang = pos[:, None] * inv[None, :]
    ang = jnp.concatenate([ang, ang], axis=-1)
    sign = jnp.where(jnp.arange(HEAD_DIM) < HEAD_DIM // 2, -1.0, 1.0).astype(F32)
    return jnp.cos(ang).T, (jnp.sin(ang) * sign[None, :]).T


def _lane_bcast(v):
    return jnp.broadcast_to(v.astype(F32)[:, None], (v.shape[0], LANES))


def _prep_layer(l, p):
    w_in = p["w_in"][l]
    cols = [w_in[:, _COLS[n][0]:_COLS[n][1]] for n in _PT_ORDER + _KT_ORDER]
    w_t = jnp.concatenate(cols, axis=1).T.astype(BF16)
    gains = jnp.stack([
        _lane_bcast(p["a_q_norm"][l] * (QK_SCALE * LOG2E)), _lane_bcast(p["a_k_norm"][l]),
        _lane_bcast(p["b_q_norm"][l] * QK_SCALE), _lane_bcast(p["b_k_norm"][l]),
        _lane_bcast(p["c_q_norm"][l] * QK_SCALE), _lane_bcast(p["c_k_norm"][l]),
    ])
    wr = jnp.zeros((D_MODEL, LANES), F32)
    wr = wr.at[:, :N_GROUPS].set(p["w_group"][l]).at[:, N_GROUPS:N_GROUPS + N_EXPERTS].set(p["w_router"][l])
    wr_hi = wr.astype(BF16)
    wr_lo = (wr - wr_hi.astype(F32)).astype(BF16)
    wr2 = jnp.concatenate([wr_hi, wr_lo], axis=1)
    br = jnp.zeros((1, LANES), F32)
    br = br.at[0, :N_GROUPS].set(p["b_group"][l]).at[0, N_GROUPS:N_GROUPS + N_EXPERTS].set(p["b_router"][l])
    bound = 1.02 * HEAD_DIM * QK_SCALE * jnp.max(jnp.abs(p["a_q_norm"][l])) * jnp.max(jnp.abs(p["a_k_norm"][l]))
    return dict(
        g1=p["norm1_g"][l].astype(F32)[None, :], w_t=w_t, gains=gains,
        lam=jnp.stack([p["a_lambda_q1"][l], p["a_lambda_k1"][l],
                       p["a_lambda_q2"][l], p["a_lambda_k2"][l]]).astype(F32),
        subg=_lane_bcast(p["a_subln"][l]),
        na_bias=_na_bias_table(p["b_rpb"][l]),
        sink=p["c_sink"][l].astype(F32),
        wo=p["w_out"][l].astype(BF16), g2=p["norm2_g"][l].astype(F32)[None, :],
        wr=wr2, br=br,
        wg=p["w_gate"][l].astype(BF16), wu=p["w_up"][l].astype(BF16), wd=p["w_down"][l].astype(BF16),
        fast_ok=bound <= FAST_SOFTMAX_BOUND,
    )


def _layer(x, lp, lambda_init, cos_t, sin_t):
    B, S, _ = x.shape
    pt, kc = _inproj(x, lp["g1"], lp["w_t"], lp["gains"], cos_t, sin_t)
    ya = lax.cond(
        lp["fast_ok"],
        lambda a, b: _diff_attn(a, b, lp["lam"], lp["subg"], lambda_init, False),
        lambda a, b: _diff_attn(a, b, lp["lam"], lp["subg"], lambda_init, True),
        pt, kc)
    yb = _na_attn(pt, kc, lp["na_bias"])
    yc = _win_attn(pt, kc, lp["sink"])
    T = B * S
    tri = jnp.tril(jnp.ones((T_MOE, T_MOE), BF16), -1)
    h, xs, cs, ri, cnt = _outproj(x.reshape(T, D_MODEL), ya.reshape(T, -1), yb.reshape(T, -1),
                                  yc.reshape(T, -1), lp["wo"], lp["g2"], lp["wr"], lp["br"], tri)
    counts = cnt[:, 0, :N_GROUPS].reshape(-1)
    os = _moe(xs, cs, counts, lp["wg"], lp["wu"], lp["wd"])
    y = _unsort(os, h, ri)
    return y.reshape(B, S, D_MODEL)


def kernel(x_prompt, x_sample, norm1_g, w_in, w_out, a_q_norm, a_k_norm, a_lambda_q1, a_lambda_k1,
           a_lambda_q2, a_lambda_k2, a_subln, b_q_norm, b_k_norm, b_rpb, c_q_norm, c_k_norm, c_sink,
           norm2_g, w_group, b_group, w_router, b_router, w_gate, w_up, w_down):
    p = dict(norm1_g=norm1_g, w_in=w_in, w_out=w_out, a_q_norm=a_q_norm, a_k_norm=a_k_norm,
             a_lambda_q1=a_lambda_q1, a_lambda_k1=a_lambda_k1, a_lambda_q2=a_lambda_q2,
             a_lambda_k2=a_lambda_k2, a_subln=a_subln, b_q_norm=b_q_norm, b_k_norm=b_k_norm,
             b_rpb=b_rpb, c_q_norm=c_q_norm, c_k_norm=c_k_norm, c_sink=c_sink, norm2_g=norm2_g,
             w_group=w_group, b_group=b_group, w_router=w_router, b_router=b_router,
             w_gate=w_gate, w_up=w_up, w_down=w_down)
    layers = [_prep_layer(l, p) for l in range(DEPTH)]

    def run(x):
        cos_t, sin_t = _rope_tables_t(x.shape[1])
        for l in range(DEPTH):
            lambda_init = 0.8 - 0.6 * math.exp(-0.3 * l)
            x = _layer(x, layers[l], lambda_init, cos_t, sin_t)
        return x

    return (run(x_prompt), run(x_sample))
```

```python
import functools
import math

import numpy as np
import jax
import jax.numpy as jnp
from jax import lax
from jax.experimental import pallas as pl
from jax.experimental.pallas import tpu as pltpu

F32 = jnp.float32
BF16 = jnp.bfloat16

D_MODEL = 1024
DEPTH = 2
HEAD_DIM = 64
GRID_W = 64
EPS = 1e-6
ROPE_THETA = 10000.0
A_HEADS = 4
A_VDIM = 2 * HEAD_DIM
B_HEADS = 4
NA_KH = 8
NA_KW = 16
C_HEADS = 4
C_KV_HEADS = 2
WINDOW = 128
N_GROUPS = 4
EXPERTS_PER_GROUP = 4
N_EXPERTS = 16
D_EXPERT = 512
QK_SCALE = HEAD_DIM ** -0.5
LOG2E = math.log2(math.e)

_COLS = dict(aq=(0, 512), ak=(512, 1024), av=(1024, 1536), bq=(1536, 1792), bk=(1792, 2048),
             bv=(2048, 2304), cq=(2304, 2560), ck=(2560, 2688), cv=(2688, 2816))
_PT_ORDER = ("aq", "av", "bq", "bv", "cq", "cv")
_KT_ORDER = ("ak", "bk", "ck")
PT_ROWS = 1920
KC_COLS = 896
IN_WIDTH = 2816

LANES = 128
NEG = -1e30

TM_PROJ = 1024
TQ_A = 512
TK_A = 2048
UNROLL_A = 2
NA_Q = 1024
NA_PAIR_Q = 2 * GRID_W
NA_PAIR_ROWS = NA_KH + 2
NA_PAIR_K = NA_PAIR_ROWS * GRID_W
TQ_C = 256
TK_C = TQ_C + 2 * WINDOW
T_MOE = 1024
MOE_SLAB = 512
MOE_STEP = 64
MOE_PAD = 128
MOE_TILES_PER_WEIGHT = 2
MOE_EXPERTS_PER_STEP = 2
FAST_SOFTMAX_BOUND = 30.0


def _cparams(sem, vmem_mb):
    return pltpu.CompilerParams(dimension_semantics=sem, vmem_limit_bytes=vmem_mb * 1024 * 1024)


def _nt_dot(a, b):
    return lax.dot_general(a, b, (((1,), (1,)), ((), ())), preferred_element_type=F32)


def _dot(a, b):
    return jnp.dot(a, b, preferred_element_type=F32)


def _inproj_kernel(x_ref, g1_ref, w_ref, gains_ref, cos_ref, sin_ref, pt_ref, kc_ref):
    tm = x_ref.shape[1]
    x = x_ref[0]
    ms = jnp.mean(x * x, axis=-1, keepdims=True)
    n = (x * lax.rsqrt(ms + EPS) * g1_ref[...]).astype(BF16)
    cos = cos_ref[...]
    sin = sin_ref[...]
    reps = tm // LANES

    def project(r0, r1):
        return _nt_dot(w_ref[r0:r1, :], n)

    def head(chunk, r0, gi, rope):
        xh = chunk[r0:r0 + HEAD_DIM, :]
        ssq = jnp.sum(xh * xh, axis=0, keepdims=True)
        gain = pltpu.repeat(gains_ref[gi], reps, axis=1)
        xg = xh * lax.rsqrt(ssq * (1.0 / HEAD_DIM) + EPS) * gain
        if rope:
            swapped = jnp.concatenate([xg[HEAD_DIM // 2:], xg[:HEAD_DIM // 2]], axis=0)
            xg = xg * cos + swapped * sin
        return xg

    def put(r0, val):
        pt_ref[0, r0:r0 + val.shape[0], :] = val.astype(BF16)

    c = project(0, 512)
    for j in range(8):
        put(j * 64, head(c, j * 64, 0, True))
    put(512, project(512, 1024))
    c = project(1024, 1536)
    for j in range(4):
        put(1024 + j * 64, head(c, j * 64, 2, False))
    put(1280, c[256:512, :])
    c = project(1536, 1920)
    for j in range(4):
        put(1536 + j * 64, head(c, j * 64, 4, True))
    put(1792, c[256:384, :])
    def put_keys(chunk, j0, specs):
        for j, (gi, rope) in enumerate(specs):
            blk = jnp.concatenate([head(chunk, j * 128, gi, rope),
                                   head(chunk, j * 128 + 64, gi, rope)], axis=0)
            kc_ref[0, :, (j0 + j) * 128:(j0 + j + 1) * 128] = blk.T.astype(BF16)

    put_keys(project(PT_ROWS, PT_ROWS + 512), 0, [(1, True)] * 4)
    put_keys(project(PT_ROWS + 512, IN_WIDTH), 4, [(3, False)] * 2 + [(5, True)])


def _inproj(x, g1, w_t, gains, cos_t, sin_t):
    B, S, _ = x.shape
    tm = TM_PROJ
    return pl.pallas_call(
        _inproj_kernel,
        grid=(B, S // tm),
        in_specs=[
            pl.BlockSpec((1, tm, D_MODEL), lambda b, i: (b, i, 0)),
            pl.BlockSpec((1, D_MODEL), lambda b, i: (0, 0)),
            pl.BlockSpec((IN_WIDTH, D_MODEL), lambda b, i: (0, 0)),
            pl.BlockSpec((6, HEAD_DIM, LANES), lambda b, i: (0, 0, 0)),
            pl.BlockSpec((HEAD_DIM, tm), lambda b, i: (0, i)),
            pl.BlockSpec((HEAD_DIM, tm), lambda b, i: (0, i)),
        ],
        out_specs=[
            pl.BlockSpec((1, PT_ROWS, tm), lambda b, i: (b, 0, i)),
            pl.BlockSpec((1, tm, KC_COLS), lambda b, i: (b, i, 0)),
        ],
        out_shape=[
            jax.ShapeDtypeStruct((B, PT_ROWS, S), BF16),
            jax.ShapeDtypeStruct((B, S, KC_COLS), BF16),
        ],
        compiler_params=_cparams(("parallel", "parallel"), 56),
        name="inproj",
    )(x, g1, w_t, gains, cos_t, sin_t)


def _diff_attn_kernel(q_ref, k_ref, v_ref, lam_ref, subg_ref, o_ref, qz_ref, acc_ref, l_ref, m_ref,
                      *, tk, safe, lambda_init):
    tq = q_ref.shape[2]
    S = k_ref.shape[1]
    q = q_ref[0]
    z = jnp.zeros((HEAD_DIM, tq), BF16)
    qz_ref[:, :tq] = jnp.concatenate([q[:HEAD_DIM], z], axis=0)
    qz_ref[:, tq:] = jnp.concatenate([z, q[HEAD_DIM:]], axis=0)
    acc_ref[...] = jnp.zeros_like(acc_ref)
    l_ref[...] = jnp.zeros_like(l_ref)
    if safe:
        m_ref[...] = jnp.full_like(m_ref, NEG)

    def body(i, carry):
        off = pl.multiple_of(i * tk, tk)
        kc = k_ref[0, pl.ds(off, tk), :]
        vc = v_ref[0, :, pl.ds(off, tk)]
        s = _dot(kc, qz_ref[...])
        if safe:
            m_old = m_ref[...]
            m_new = jnp.maximum(m_old, jnp.max(s, axis=0, keepdims=True))
            alpha = jnp.exp2(m_old - m_new)
            p = jnp.exp2(s - m_new)
            m_ref[...] = m_new
            l_ref[...] = alpha * l_ref[...] + jnp.sum(p, axis=0, keepdims=True)
            acc_ref[...] = alpha * acc_ref[...] + _dot(vc, p.astype(BF16))
        else:
            p = jnp.exp2(s)
            l_ref[...] = l_ref[...] + jnp.sum(p, axis=0, keepdims=True)
            acc_ref[...] = acc_ref[...] + _dot(vc, p.astype(BF16))
        return carry

    lax.fori_loop(0, S // tk, body, 0, unroll=UNROLL_A)

    lp = lam_ref[...]
    lam = (jnp.exp(jnp.sum(lp[0:1] * lp[1:2], axis=-1, keepdims=True))
           - jnp.exp(jnp.sum(lp[2:3] * lp[3:4], axis=-1, keepdims=True)) + lambda_init)
    on = acc_ref[...] / l_ref[...]
    o = on[:, :tq] - lam * on[:, tq:]
    ms = jnp.mean(o * o, axis=0, keepdims=True)
    gain = pltpu.repeat(subg_ref[...], tq // LANES, axis=1)
    y = o * lax.rsqrt(ms + EPS) * gain * (1.0 - lambda_init)
    o_ref[0] = y.T.astype(BF16)


def _diff_attn(pt, kc, lam_params, subg, lambda_init, safe):
    B, _, S = pt.shape
    tq, tk = TQ_A, min(TK_A, S)
    kern = functools.partial(_diff_attn_kernel, tk=tk, safe=safe, lambda_init=lambda_init)
    return pl.pallas_call(
        kern,
        grid=(B, A_HEADS, S // tq),
        in_specs=[
            pl.BlockSpec((1, 128, tq), lambda b, h, i: (b, h, i)),
            pl.BlockSpec((1, S, 128), lambda b, h, i: (b, 0, h)),
            pl.BlockSpec((1, 128, S), lambda b, h, i: (b, 4 + h, 0)),
            pl.BlockSpec((4, HEAD_DIM), lambda b, h, i: (0, 0)),
            pl.BlockSpec((A_VDIM, LANES), lambda b, h, i: (0, 0)),
        ],
        out_specs=pl.BlockSpec((1, tq, 128), lambda b, h, i: (b, i, h)),
        out_shape=jax.ShapeDtypeStruct((B, S, A_HEADS * A_VDIM), BF16),
        scratch_shapes=[
            pltpu.VMEM((128, 2 * tq), BF16),
            pltpu.VMEM((A_VDIM, 2 * tq), F32),
            pltpu.VMEM((1, 2 * tq), F32),
            pltpu.VMEM((1, 2 * tq), F32),
        ],
        compiler_params=_cparams(("parallel", "parallel", "arbitrary"), 48),
        name="diff_attn_safe" if safe else "diff_attn",
    )(pt, kc, pt, lam_params, subg)


def _na_kernel(q_ref, k_ref, v_ref, bias_ref, o_ref):
    S = k_ref.shape[1]
    rows = S // GRID_W
    r = pl.program_id(2)
    q = q_ref[0]
    z = jnp.zeros((HEAD_DIM, NA_PAIR_Q), BF16)
    npairs = NA_Q // NA_PAIR_Q
    scores, vals = [], []
    for u in range(npairs):
        mp = r * npairs + u
        srow = jnp.clip(2 * mp - NA_KH // 2, 0, rows - NA_PAIR_ROWS)
        start = pl.multiple_of(srow * GRID_W, 128)
        var = jnp.where(mp == 0, 0, jnp.where(mp == 1, 1, jnp.where(
            mp == rows // 2 - 2, 3, jnp.where(mp == rows // 2 - 1, 4, 2))))
        ks = k_ref[0, pl.ds(start, NA_PAIR_K), :]
        vals.append(v_ref[0, :, pl.ds(start, NA_PAIR_K)])
        qp = q[:, u * NA_PAIR_Q:(u + 1) * NA_PAIR_Q]
        qz = jnp.concatenate([jnp.concatenate([qp[:HEAD_DIM], z], axis=0),
                              jnp.concatenate([z, qp[HEAD_DIM:]], axis=0)], axis=1)
        scores.append(_dot(ks, qz) + bias_ref[0, var])
    probs, sums = [], []
    for s in scores:
        m = jnp.max(s, axis=0, keepdims=True)
        p = jnp.exp(s - m)
        sums.append(jnp.sum(p, axis=0, keepdims=True))
        probs.append(p.astype(BF16))
    outs = []
    for vs, p, l in zip(vals, probs, sums):
        o = _dot(vs, p) / l
        outs.append(jnp.concatenate([o[:HEAD_DIM, :NA_PAIR_Q], o[HEAD_DIM:, NA_PAIR_Q:]], axis=0))
    o_ref[0] = jnp.concatenate(outs, axis=1).T.astype(BF16)


def _na_attn(pt, kc, bias):
    B, _, S = pt.shape
    nr = S // NA_Q
    return pl.pallas_call(
        _na_kernel,
        grid=(B, 2, nr),
        in_specs=[
            pl.BlockSpec((1, 128, NA_Q), lambda b, p, r: (b, 8 + p, r)),
            pl.BlockSpec((1, S, 128), lambda b, p, r: (b, 0, 4 + p)),
            pl.BlockSpec((1, 128, S), lambda b, p, r: (b, 10 + p, 0)),
            pl.BlockSpec((1, 5, NA_PAIR_K, 2 * NA_PAIR_Q), lambda b, p, r: (p, 0, 0, 0)),
        ],
        out_specs=pl.BlockSpec((1, NA_Q, 128), lambda b, p, r: (b, r, p)),
        out_shape=jax.ShapeDtypeStruct((B, S, B_HEADS * HEAD_DIM), BF16),
        compiler_params=_cparams(("parallel", "parallel", "arbitrary"), 48),
        name="na_attn",
    )(pt, kc, pt, bias)


def _na_bias_index():
    rows = 32
    a = np.arange(2)[None, :]
    i = np.arange(NA_PAIR_ROWS)[:, None]
    ridx, rvalid = [], []
    for mp in (0, 1, 5, rows // 2 - 2, rows // 2 - 1):
        srow = np.clip(2 * mp - NA_KH // 2, 0, rows - NA_PAIR_ROWS)
        qrow = 2 * mp + a
        krow = srow + i
        rs = np.clip(qrow - NA_KH // 2, 0, rows - NA_KH)
        rvalid.append((krow >= rs) & (krow < rs + NA_KH))
        ridx.append(np.clip(krow - qrow + NA_KH - 1, 0, 2 * NA_KH - 2))
    ridx = np.stack(ridx)
    rvalid = np.stack(rvalid)
    qc = np.arange(GRID_W)[None, :]
    kcol = np.arange(GRID_W)[:, None]
    cs = np.clip(qc - NA_KW // 2, 0, GRID_W - NA_KW)
    cvalid = (kcol >= cs) & (kcol < cs + NA_KW)
    cidx = np.clip(kcol - qc + NA_KW - 1, 0, 2 * NA_KW - 2)
    valid = rvalid[:, :, None, :, None] & cvalid[None, None, :, None, :]
    col_onehot = (cidx.reshape(-1)[None, :] == np.arange(2 * NA_KW - 1)[:, None])
    return (ridx.astype(np.int32), col_onehot.astype(np.float32),
            valid.reshape(5, NA_PAIR_K, NA_PAIR_Q))


_NA_RIDX, _NA_COL_ONEHOT, _NA_VALID = _na_bias_index()


def _na_bias_table(rpb):
    rows = jnp.take(rpb.astype(F32), jnp.asarray(_NA_RIDX), axis=1)
    tab = jnp.einsum("hvkqc,cn->hvkqn", rows, jnp.asarray(_NA_COL_ONEHOT),
                     precision=lax.Precision.HIGHEST)
    tab = tab.reshape(B_HEADS, 5, NA_PAIR_ROWS, 2, GRID_W, GRID_W).transpose(0, 1, 2, 4, 3, 5)
    tab = tab.reshape(B_HEADS, 5, NA_PAIR_K, NA_PAIR_Q)
    tab = jnp.where(jnp.asarray(_NA_VALID)[None], tab, NEG)
    tab = tab.reshape(2, 2, 5, NA_PAIR_K, NA_PAIR_Q).transpose(0, 2, 3, 1, 4)
    return tab.reshape(2, 5, NA_PAIR_K, 2 * NA_PAIR_Q)


def _win_kernel(sink_ref, q_ref, k_ref, v_ref, o_ref):
    S = k_ref.shape[1]
    i = pl.program_id(1)
    q0 = i * TQ_C
    start = pl.multiple_of(jnp.clip(q0 - WINDOW, 0, S - TK_C), 128)
    ks = k_ref[0, pl.ds(start, TK_C), :]
    vs = v_ref[0, :, pl.ds(start, TK_C)]
    kpos = start + lax.broadcasted_iota(jnp.int32, (TK_C, TQ_C), 0)
    qpos = q0 + lax.broadcasted_iota(jnp.int32, (TK_C, TQ_C), 1)
    mask = jnp.where(jnp.abs(qpos - kpos) <= WINDOW, 0.0, NEG).astype(F32)
    z = jnp.zeros((HEAD_DIM, TQ_C), BF16)
    gq = C_HEADS // C_KV_HEADS
    scores = []
    for hq in range(C_HEADS):
        qh = q_ref[0, hq * HEAD_DIM:(hq + 1) * HEAD_DIM, :]
        qz = jnp.concatenate([qh, z] if hq // gq == 0 else [z, qh], axis=0)
        scores.append(_dot(ks, qz) + mask)
    probs, dens = [], []
    for hq, s in enumerate(scores):
        sk = sink_ref[hq]
        m = jnp.maximum(jnp.max(s, axis=0, keepdims=True), sk)
        e = jnp.exp(s - m)
        dens.append(jnp.sum(e, axis=0, keepdims=True) + jnp.exp(sk - m))
        probs.append(e.astype(BF16))
    outs = []
    for hq, (e, den) in enumerate(zip(probs, dens)):
        kv = hq // gq
        outs.append(_dot(vs[kv * HEAD_DIM:(kv + 1) * HEAD_DIM], e) / den)
    o_ref[0] = jnp.concatenate(outs, axis=0).T.astype(BF16)


def _win_attn(pt, kc, sink):
    B, _, S = pt.shape
    return pl.pallas_call(
        _win_kernel,
        grid=(B, S // TQ_C),
        in_specs=[
            pl.BlockSpec(memory_space=pltpu.SMEM),
            pl.BlockSpec((1, 256, TQ_C), lambda b, i: (b, 6, i)),
            pl.BlockSpec((1, S, 128), lambda b, i: (b, 0, 6)),
            pl.BlockSpec((1, 128, S), lambda b, i: (b, 14, 0)),
        ],
        out_specs=pl.BlockSpec((1, TQ_C, 256), lambda b, i: (b, i, 0)),
        out_shape=jax.ShapeDtypeStruct((B, S, C_HEADS * HEAD_DIM), BF16),
        compiler_params=_cparams(("parallel", "arbitrary"), 48),
        name="win_attn",
    )(sink, pt, kc, pt)


def _outproj_kernel(x_ref, ya_ref, yb_ref, yc_ref, wo_ref, g2_ref, wr_ref, br_ref, tri_ref,
                    h_ref, xs_ref, cs_ref, ri_ref, cnt_ref):
    tm = x_ref.shape[0]
    h = (x_ref[...]
         + _dot(ya_ref[...], wo_ref[0:512, :])
         + _dot(yb_ref[...], wo_ref[512:768, :])
         + _dot(yc_ref[...], wo_ref[768:1024, :]))
    h_ref[...] = h
    ms = jnp.mean(h * h, axis=-1, keepdims=True)
    n2 = h * lax.rsqrt(ms + EPS) * g2_ref[...]
    n2_hi = n2.astype(BF16)
    n2_lo = (n2 - n2_hi.astype(F32)).astype(BF16)
    both = _dot(n2_hi, wr_ref[...])
    logits = both[:, :LANES] + both[:, LANES:] + _dot(n2_lo, wr_ref[:, :LANES]) + br_ref[...]
    lane = lax.broadcasted_iota(jnp.int32, logits.shape, 1).astype(F32)
    big = float(LANES)
    ninf = -jnp.inf

    def first_argmax(vals):
        vmax = jnp.max(vals, axis=-1, keepdims=True)
        idx = jnp.min(jnp.where(vals == vmax, lane, big), axis=-1, keepdims=True)
        return vmax, idx

    gl = jnp.where(lane < N_GROUPS, logits, ninf)
    gmax, gidx = first_argmax(gl)
    gp = 1.0 / jnp.sum(jnp.exp(gl - gmax), axis=-1, keepdims=True)
    lo = N_GROUPS + EXPERTS_PER_GROUP * gidx
    el = jnp.where((lane >= lo) & (lane < lo + EXPERTS_PER_GROUP), logits, ninf)
    v1, i1 = first_argmax(el)
    el2 = jnp.where(lane == i1, ninf, el)
    v2, i2 = first_argmax(el2)
    t = jnp.exp(v2 - v1)
    w1 = 1.0 / (1.0 + t)
    w2 = t / (1.0 + t)
    comb = jnp.where(lane == i1, gp * w1, jnp.where(lane == i2, gp * w2, 0.0))

    onehot = jnp.where(lane == gidx, 1.0, 0.0)
    before = _dot(tri_ref[...], onehot.astype(BF16))
    cnt = jnp.sum(onehot, axis=0, keepdims=True)
    pos = jnp.sum(jnp.where(lane < gidx, cnt, 0.0) + jnp.where(lane == gidx, before, 0.0),
                  axis=-1, keepdims=True)
    ri = jnp.where(lane == 0.0, pos, comb)
    ri_ref[...] = ri
    hi = ri.astype(BF16)
    r1 = ri - hi.astype(F32)
    mid = r1.astype(BF16)
    low = (r1 - mid.astype(F32)).astype(BF16)
    rs = jnp.concatenate([hi, mid, low], axis=1)
    pos_t = jnp.broadcast_to(pos, (tm, LANES)).T
    want = lax.broadcasted_iota(jnp.int32, (tm, tm), 0).astype(F32)
    perm = jnp.where(pos_t[0:1, :] == want, 1.0, 0.0).astype(BF16)
    xs_ref[0, 0:tm, :] = _dot(perm, n2_hi).astype(BF16)
    c3 = _dot(perm, rs)
    cs_ref[0, 0:tm, :] = c3[:, :LANES] + c3[:, LANES:2 * LANES] + c3[:, 2 * LANES:]
    xs_ref[0, tm:, :] = jnp.zeros((MOE_PAD, D_MODEL), BF16)
    cs_ref[0, tm:, :] = jnp.zeros((MOE_PAD, LANES), F32)
    cnt_ref[0] = jnp.broadcast_to(cnt, (8, LANES)).astype(jnp.int32)


def _outproj(x2, ya, yb, yc, wo, g2, wr, br, tri):
    T = x2.shape[0]
    tm = T_MOE
    nt = T // tm
    rows = tm + MOE_PAD
    row = lambda i: (i, 0)
    fixed = lambda i: (0, 0)
    return pl.pallas_call(
        _outproj_kernel,
        grid=(nt,),
        in_specs=[
            pl.BlockSpec((tm, D_MODEL), row),
            pl.BlockSpec((tm, 512), row),
            pl.BlockSpec((tm, 256), row),
            pl.BlockSpec((tm, 256), row),
            pl.BlockSpec((D_MODEL, D_MODEL), fixed),
            pl.BlockSpec((1, D_MODEL), fixed),
            pl.BlockSpec((D_MODEL, 2 * LANES), fixed),
            pl.BlockSpec((1, LANES), fixed),
            pl.BlockSpec((tm, tm), fixed),
        ],
        out_specs=[
            pl.BlockSpec((tm, D_MODEL), row),
            pl.BlockSpec((1, rows, D_MODEL), lambda i: (i, 0, 0)),
            pl.BlockSpec((1, rows, LANES), lambda i: (i, 0, 0)),
            pl.BlockSpec((tm, LANES), row),
            pl.BlockSpec((1, 8, LANES), lambda i: (i, 0, 0)),
        ],
        out_shape=[
            jax.ShapeDtypeStruct((T, D_MODEL), F32),
            jax.ShapeDtypeStruct((nt, rows, D_MODEL), BF16),
            jax.ShapeDtypeStruct((nt, rows, LANES), F32),
            jax.ShapeDtypeStruct((T, LANES), F32),
            jax.ShapeDtypeStruct((nt, 8, LANES), jnp.int32),
        ],
        compiler_params=_cparams(("parallel",), 56),
        name="outproj",
    )(x2, ya, yb, yc, wo, g2, wr, br, tri)


def _moe_kernel(cnt_ref, xs_ref, cs_ref, wg_ref, wu_ref, wd_ref, o_ref, acc_ref):
    tiles = xs_ref.shape[0]
    nexp = wg_ref.shape[0]
    step = pl.program_id(1)
    e0 = step * nexp
    g = lax.shift_right_logical(e0, 2)

    @pl.when(step == 0)
    def _():
        acc_ref[...] = jnp.zeros_like(acc_ref)

    def ffn_pass(t, r0, size):
        r0 = pl.multiple_of(r0, 16)
        xs = xs_ref[t, pl.ds(r0, size), :]
        cs = cs_ref[t, pl.ds(r0, size), :]
        lane = lax.broadcasted_iota(jnp.int32, cs.shape, 1)
        out = None
        for j in range(nexp):
            hg = _dot(xs, wg_ref[j])
            hu = _dot(xs, wu_ref[j])
            w = jnp.sum(jnp.where(lane == N_GROUPS + e0 + j, cs, 0.0), axis=-1, keepdims=True)
            hh = (hg * jax.nn.sigmoid(hg) * hu * w).astype(BF16)
            d = _dot(hh, wd_ref[j])
            out = d if out is None else out + d
        acc_ref[t, pl.ds(r0, size), :] = acc_ref[t, pl.ds(r0, size), :] + out

    for t in range(tiles):
        i = pl.program_id(0) * tiles + t
        n = [cnt_ref[i * N_GROUPS + k] for k in range(N_GROUPS)]
        off = (jnp.where(g > 0, n[0], 0) + jnp.where(g > 1, n[1], 0) + jnp.where(g > 2, n[2], 0))
        ng = jnp.where(g == 0, n[0], jnp.where(g == 1, n[1], jnp.where(g == 2, n[2], n[3])))
        start = lax.shift_left(lax.shift_right_logical(off, 4), 4)
        need = off - start + ng
        nfull = jnp.maximum(lax.div(need - 1, MOE_SLAB), 0)
        rem_steps = lax.div(need - nfull * MOE_SLAB + (MOE_STEP - 1), MOE_STEP)

        def full_pass(s, carry, t=t, start=start):
            ffn_pass(t, start + s * MOE_SLAB, MOE_SLAB)
            return carry
        lax.fori_loop(0, nfull, full_pass, 0)

        for k in range(1, MOE_SLAB // MOE_STEP + 1):
            @pl.when(rem_steps == k)
            def _(k=k, t=t, start=start, nfull=nfull):
                ffn_pass(t, start + nfull * MOE_SLAB, k * MOE_STEP)

    @pl.when(step == pl.num_programs(1) - 1)
    def _():
        o_ref[...] = acc_ref[...].astype(BF16)


def _moe(xs, cs, cnt, wg, wu, wd):
    nt, rows, _ = xs.shape
    tpw = MOE_TILES_PER_WEIGHT if nt % MOE_TILES_PER_WEIGHT == 0 else 1
    nexp = MOE_EXPERTS_PER_STEP
    tmap = lambda s, e, c: (s, 0, 0)
    wmap = lambda s, e, c: (e, 0, 0)
    return pl.pallas_call(
        _moe_kernel,
        grid_spec=pltpu.PrefetchScalarGridSpec(
            num_scalar_prefetch=1,
            grid=(nt // tpw, N_EXPERTS // nexp),
            in_specs=[
                pl.BlockSpec((tpw, rows, D_MODEL), tmap),
                pl.BlockSpec((tpw, rows, LANES), tmap),
                pl.BlockSpec((nexp, D_MODEL, D_EXPERT), wmap),
                pl.BlockSpec((nexp, D_MODEL, D_EXPERT), wmap),
                pl.BlockSpec((nexp, D_EXPERT, D_MODEL), wmap),
            ],
            out_specs=pl.BlockSpec((tpw, rows, D_MODEL), tmap),
            scratch_shapes=[pltpu.VMEM((tpw, rows, D_MODEL), F32)],
        ),
        out_shape=jax.ShapeDtypeStruct((nt, rows, D_MODEL), BF16),
        compiler_params=_cparams(("parallel", "arbitrary"), 56),
        name="moe",
    )(cnt, xs, cs, wg, wu, wd)


def _unsort_kernel(os_ref, h_ref, ri_ref, o_ref):
    tm = h_ref.shape[0]
    want = lax.broadcasted_iota(jnp.int32, (tm, tm), 1).astype(F32)
    perm_t = jnp.where(ri_ref[:, 0:1] == want, 1.0, 0.0).astype(BF16)
    o_ref[...] = h_ref[...] + _dot(perm_t, os_ref[0, 0:tm, :])


def _unsort(os, h, ri):
    T = h.shape[0]
    tm = T_MOE
    rows = os.shape[1]
    row = lambda i: (i, 0)
    return pl.pallas_call(
        _unsort_kernel,
        grid=(T // tm,),
        in_specs=[
            pl.BlockSpec((1, rows, D_MODEL), lambda i: (i, 0, 0)),
            pl.BlockSpec((tm, D_MODEL), row),
            pl.BlockSpec((tm, LANES), row),
        ],
        out_specs=pl.BlockSpec((tm, D_MODEL), row),
        out_shape=jax.ShapeDtypeStruct((T, D_MODEL), F32),
        compiler_params=_cparams(("parallel",), 48),
        name="unsort",
    )(os, h, ri)


def _rope_tables_t(seq_len):
    pos = jnp.arange(seq_len, dtype=F32)
    inv = ROPE_THETA ** (-jnp.arange(0, HEAD_DIM, 2, dtype=F32) / HEAD_DIM)
    ang = pos[:, None] * inv[None, :]
    ang = jnp.concatenate([ang, ang], axis=-1)
    sign = jnp.where(jnp.arange(HEAD_DIM) < HEAD_DIM // 2, -1.0, 1.0).astype(F32)
    return jnp.cos(ang).T, (jnp.sin(ang) * sign[None, :]).T


def _lane_bcast(v):
    return jnp.broadcast_to(v.astype(F32)[:, None], (v.shape[0], LANES))


def _prep_layer(l, p):
    w_in = p["w_in"][l]
    cols = [w_in[:, _COLS[n][0]:_COLS[n][1]] for n in _PT_ORDER + _KT_ORDER]
    w_t = jnp.concatenate(cols, axis=1).T.astype(BF16)
    gains = jnp.stack([
        _lane_bcast(p["a_q_norm"][l] * (QK_SCALE * LOG2E)), _lane_bcast(p["a_k_norm"][l]),
        _lane_bcast(p["b_q_norm"][l] * QK_SCALE), _lane_bcast(p["b_k_norm"][l]),
        _lane_bcast(p["c_q_norm"][l] * QK_SCALE), _lane_bcast(p["c_k_norm"][l]),
    ])
    wr = jnp.zeros((D_MODEL, LANES), F32)
    wr = wr.at[:, :N_GROUPS].set(p["w_group"][l]).at[:, N_GROUPS:N_GROUPS + N_EXPERTS].set(p["w_router"][l])
    wr_hi = wr.astype(BF16)
    wr_lo = (wr - wr_hi.astype(F32)).astype(BF16)
    wr2 = jnp.concatenate([wr_hi, wr_lo], axis=1)
    br = jnp.zeros((1, LANES), F32)
    br = br.at[0, :N_GROUPS].set(p["b_group"][l]).at[0, N_GROUPS:N_GROUPS + N_EXPERTS].set(p["b_router"][l])
    bound = 1.02 * HEAD_DIM * QK_SCALE * jnp.max(jnp.abs(p["a_q_norm"][l])) * jnp.max(jnp.abs(p["a_k_norm"][l]))
    return dict(
        g1=p["norm1_g"][l].astype(F32)[None, :], w_t=w_t, gains=gains,
        lam=jnp.stack([p["a_lambda_q1"][l], p["a_lambda_k1"][l],
                       p["a_lambda_q2"][l], p["a_lambda_k2"][l]]).astype(F32),
        subg=_lane_bcast(p["a_subln"][l]),
        na_bias=_na_bias_table(p["b_rpb"][l]),
        sink=p["c_sink"][l].astype(F32),
        wo=p["w_out"][l].astype(BF16), g2=p["norm2_g"][l].astype(F32)[None, :],
        wr=wr2, br=br,
        wg=p["w_gate"][l].astype(BF16), wu=p["w_up"][l].astype(BF16), wd=p["w_down"][l].astype(BF16),
        fast_ok=bound <= FAST_SOFTMAX_BOUND,
    )


def _layer(x, lp, lambda_init, cos_t, sin_t):
    B, S, _ = x.shape
    pt, kc = _inproj(x, lp["g1"], lp["w_t"], lp["gains"], cos_t, sin_t)
    ya = lax.cond(
        lp["fast_ok"],
        lambda a, b: _diff_attn(a, b, lp["lam"], lp["subg"], lambda_init, False),
        lambda a, b: _diff_attn(a, b, lp["lam"], lp["subg"], lambda_init, True),
        pt, kc)
    yb = _na_attn(pt, kc, lp["na_bias"])
    yc = _win_attn(pt, kc, lp["sink"])
    T = B * S
    tri = jnp.tril(jnp.ones((T_MOE, T_MOE), BF16), -1)
    h, xs, cs, ri, cnt = _outproj(x.reshape(T, D_MODEL), ya.reshape(T, -1), yb.reshape(T, -1),
                                  yc.reshape(T, -1), lp["wo"], lp["g2"], lp["wr"], lp["br"], tri)
    counts = cnt[:, 0, :N_GROUPS].reshape(-1)
    os = _moe(xs, cs, counts, lp["wg"], lp["wu"], lp["wd"])
    y = _unsort(os, h, ri)
    return y.reshape(B, S, D_MODEL)


def kernel(x_prompt, x_sample, norm1_g, w_in, w_out, a_q_norm, a_k_norm, a_lambda_q1, a_lambda_k1,
           a_lambda_q2, a_lambda_k2, a_subln, b_q_norm, b_k_norm, b_rpb, c_q_norm, c_k_norm, c_sink,
           norm2_g, w_group, b_group, w_router, b_router, w_gate, w_up, w_down):
    p = dict(norm1_g=norm1_g, w_in=w_in, w_out=w_out, a_q_norm=a_q_norm, a_k_norm=a_k_norm,
             a_lambda_q1=a_lambda_q1, a_lambda_k1=a_lambda_k1, a_lambda_q2=a_lambda_q2,
             a_lambda_k2=a_lambda_k2, a_subln=a_subln, b_q_norm=b_q_norm, b_k_norm=b_k_norm,
             b_rpb=b_rpb, c_q_norm=c_q_norm, c_k_norm=c_k_norm, c_sink=c_sink, norm2_g=norm2_g,
             w_group=w_group, b_group=b_group, w_router=w_router, b_router=b_router,
             w_gate=w_gate, w_up=w_up, w_down=w_down)
    layers = [_prep_layer(l, p) for l in range(DEPTH)]

    def run(x):
        cos_t, sin_t = _rope_tables_t(x.shape[1])
        for l in range(DEPTH):
            lambda_init = 0.8 - 0.6 * math.exp(-0.3 * l)
            x = _layer(x, layers[l], lambda_init, cos_t, sin_t)
        return x

    return (run(x_prompt), run(x_sample))
```

```python
import functools
import math

import numpy as np
import jax
import jax.numpy as jnp
from jax import lax
from jax.experimental import pallas as pl
from jax.experimental.pallas import tpu as pltpu

F32 = jnp.float32
BF16 = jnp.bfloat16

D_MODEL = 1024
DEPTH = 2
HEAD_DIM = 64
GRID_W = 64
EPS = 1e-6
ROPE_THETA = 10000.0
A_HEADS = 4
A_VDIM = 2 * HEAD_DIM
B_HEADS = 4
NA_KH = 8
NA_KW = 16
C_HEADS = 4
C_KV_HEADS = 2
WINDOW = 128
N_GROUPS = 4
EXPERTS_PER_GROUP = 4
N_EXPERTS = 16
D_EXPERT = 512
QK_SCALE = HEAD_DIM ** -0.5
LOG2E = math.log2(math.e)

_COLS = dict(aq=(0, 512), ak=(512, 1024), av=(1024, 1536), bq=(1536, 1792), bk=(1792, 2048),
             bv=(2048, 2304), cq=(2304, 2560), ck=(2560, 2688), cv=(2688, 2816))
_PT_ORDER = ("aq", "av", "bq", "bv", "cq", "cv")
_KT_ORDER = ("ak", "bk", "ck")
PT_ROWS = 1920
KC_COLS = 896
IN_WIDTH = 2816

LANES = 128
NEG = -1e30

TM_PROJ = 1024
TQ_A = 1024
TK_A = 2048
UNROLL_A = 2
NA_Q = 1024
NA_PAIR_Q = 2 * GRID_W
NA_PAIR_ROWS = NA_KH + 2
NA_PAIR_K = NA_PAIR_ROWS * GRID_W
TQ_C = 256
TK_C = TQ_C + 2 * WINDOW
T_MOE = 1024
MOE_SLAB = 512
MOE_STEP = 64
MOE_PAD = 128
MOE_TILES_PER_WEIGHT = 2
MOE_EXPERTS_PER_STEP = 4
FAST_SOFTMAX_BOUND = 30.0


def _cparams(sem, vmem_mb):
    return pltpu.CompilerParams(dimension_semantics=sem, vmem_limit_bytes=vmem_mb * 1024 * 1024)


def _nt_dot(a, b):
    return lax.dot_general(a, b, (((1,), (1,)), ((), ())), preferred_element_type=F32)


def _dot(a, b):
    return jnp.dot(a, b, preferred_element_type=F32)


def _inproj_kernel(x_ref, g1_ref, w_ref, gains_ref, cos_ref, sin_ref, pt_ref, kc_ref):
    tm = x_ref.shape[1]
    x = x_ref[0]
    ms = jnp.mean(x * x, axis=-1, keepdims=True)
    n = (x * lax.rsqrt(ms + EPS) * g1_ref[...]).astype(BF16)
    cos = cos_ref[...]
    sin = sin_ref[...]
    reps = tm // LANES

    def project(r0, r1):
        return _nt_dot(w_ref[r0:r1, :], n)

    def head(chunk, r0, gi, rope):
        xh = chunk[r0:r0 + HEAD_DIM, :]
        ssq = jnp.sum(xh * xh, axis=0, keepdims=True)
        gain = pltpu.repeat(gains_ref[gi], reps, axis=1)
        xg = xh * lax.rsqrt(ssq * (1.0 / HEAD_DIM) + EPS) * gain
        if rope:
            swapped = jnp.concatenate([xg[HEAD_DIM // 2:], xg[:HEAD_DIM // 2]], axis=0)
            xg = xg * cos + swapped * sin
        return xg

    def put(r0, val):
        pt_ref[0, r0:r0 + val.shape[0], :] = val.astype(BF16)

    c = project(0, 512)
    for j in range(8):
        put(j * 64, head(c, j * 64, 0, True))
    put(512, project(512, 1024))
    c = project(1024, 1536)
    for j in range(4):
        put(1024 + j * 64, head(c, j * 64, 2, False))
    put(1280, c[256:512, :])
    c = project(1536, 1920)
    for j in range(4):
        put(1536 + j * 64, head(c, j * 64, 4, True))
    put(1792, c[256:384, :])
    def put_keys(chunk, j0, specs):
        for j, (gi, rope) in enumerate(specs):
            blk = jnp.concatenate([head(chunk, j * 128, gi, rope),
                                   head(chunk, j * 128 + 64, gi, rope)], axis=0)
            kc_ref[0, :, (j0 + j) * 128:(j0 + j + 1) * 128] = blk.T.astype(BF16)

    put_keys(project(PT_ROWS, PT_ROWS + 512), 0, [(1, True)] * 4)
    put_keys(project(PT_ROWS + 512, IN_WIDTH), 4, [(3, False)] * 2 + [(5, True)])


def _inproj(x, g1, w_t, gains, cos_t, sin_t):
    B, S, _ = x.shape
    tm = TM_PROJ
    return pl.pallas_call(
        _inproj_kernel,
        grid=(B, S // tm),
        in_specs=[
            pl.BlockSpec((1, tm, D_MODEL), lambda b, i: (b, i, 0)),
            pl.BlockSpec((1, D_MODEL), lambda b, i: (0, 0)),
            pl.BlockSpec((IN_WIDTH, D_MODEL), lambda b, i: (0, 0)),
            pl.BlockSpec((6, HEAD_DIM, LANES), lambda b, i: (0, 0, 0)),
            pl.BlockSpec((HEAD_DIM, tm), lambda b, i: (0, i)),
            pl.BlockSpec((HEAD_DIM, tm), lambda b, i: (0, i)),
        ],
        out_specs=[
            pl.BlockSpec((1, PT_ROWS, tm), lambda b, i: (b, 0, i)),
            pl.BlockSpec((1, tm, KC_COLS), lambda b, i: (b, i, 0)),
        ],
        out_shape=[
            jax.ShapeDtypeStruct((B, PT_ROWS, S), BF16),
            jax.ShapeDtypeStruct((B, S, KC_COLS), BF16),
        ],
        compiler_params=_cparams(("parallel", "parallel"), 56),
        name="inproj",
    )(x, g1, w_t, gains, cos_t, sin_t)


def _diff_attn_kernel(q_ref, k_ref, v_ref, lam_ref, subg_ref, o_ref, qz_ref, acc_ref, l_ref, m_ref,
                      *, tk, safe, lambda_init):
    tq = q_ref.shape[2]
    S = k_ref.shape[1]
    q = q_ref[0]
    z = jnp.zeros((HEAD_DIM, tq), BF16)
    qz_ref[:, :tq] = jnp.concatenate([q[:HEAD_DIM], z], axis=0)
    qz_ref[:, tq:] = jnp.concatenate([z, q[HEAD_DIM:]], axis=0)
    acc_ref[...] = jnp.zeros_like(acc_ref)
    l_ref[...] = jnp.zeros_like(l_ref)
    if safe:
        m_ref[...] = jnp.full_like(m_ref, NEG)

    def body(i, carry):
        off = pl.multiple_of(i * tk, tk)
        kc = k_ref[0, pl.ds(off, tk), :]
        vc = v_ref[0, :, pl.ds(off, tk)]
        s = _dot(kc, qz_ref[...])
        if safe:
            m_old = m_ref[...]
            m_new = jnp.maximum(m_old, jnp.max(s, axis=0, keepdims=True))
            alpha = jnp.exp2(m_old - m_new)
            p = jnp.exp2(s - m_new)
            m_ref[...] = m_new
            l_ref[...] = alpha * l_ref[...] + jnp.sum(p, axis=0, keepdims=True)
            acc_ref[...] = alpha * acc_ref[...] + _dot(vc, p.astype(BF16))
        else:
            p = jnp.exp2(s)
            l_ref[...] = l_ref[...] + jnp.sum(p, axis=0, keepdims=True)
            acc_ref[...] = acc_ref[...] + _dot(vc, p.astype(BF16))
        return carry

    lax.fori_loop(0, S // tk, body, 0, unroll=UNROLL_A)

    lp = lam_ref[...]
    lam = (jnp.exp(jnp.sum(lp[0:1] * lp[1:2], axis=-1, keepdims=True))
           - jnp.exp(jnp.sum(lp[2:3] * lp[3:4], axis=-1, keepdims=True)) + lambda_init)
    on = acc_ref[...] / l_ref[...]
    o = on[:, :tq] - lam * on[:, tq:]
    ms = jnp.mean(o * o, axis=0, keepdims=True)
    gain = pltpu.repeat(subg_ref[...], tq // LANES, axis=1)
    y = o * lax.rsqrt(ms + EPS) * gain * (1.0 - lambda_init)
    o_ref[0] = y.T.astype(BF16)


def _diff_attn(pt, kc, lam_params, subg, lambda_init, safe):
    B, _, S = pt.shape
    tq, tk = TQ_A, min(TK_A, S)
    kern = functools.partial(_diff_attn_kernel, tk=tk, safe=safe, lambda_init=lambda_init)
    return pl.pallas_call(
        kern,
        grid=(B, A_HEADS, S // tq),
        in_specs=[
            pl.BlockSpec((1, 128, tq), lambda b, h, i: (b, h, i)),
            pl.BlockSpec((1, S, 128), lambda b, h, i: (b, 0, h)),
            pl.BlockSpec((1, 128, S), lambda b, h, i: (b, 4 + h, 0)),
            pl.BlockSpec((4, HEAD_DIM), lambda b, h, i: (0, 0)),
            pl.BlockSpec((A_VDIM, LANES), lambda b, h, i: (0, 0)),
        ],
        out_specs=pl.BlockSpec((1, tq, 128), lambda b, h, i: (b, i, h)),
        out_shape=jax.ShapeDtypeStruct((B, S, A_HEADS * A_VDIM), BF16),
        scratch_shapes=[
            pltpu.VMEM((128, 2 * tq), BF16),
            pltpu.VMEM((A_VDIM, 2 * tq), F32),
            pltpu.VMEM((1, 2 * tq), F32),
            pltpu.VMEM((1, 2 * tq), F32),
        ],
        compiler_params=_cparams(("parallel", "parallel", "arbitrary"), 48),
        name="diff_attn_safe" if safe else "diff_attn",
    )(pt, kc, pt, lam_params, subg)


def _na_kernel(q_ref, k_ref, v_ref, bias_ref, o_ref):
    S = k_ref.shape[1]
    rows = S // GRID_W
    r = pl.program_id(2)
    q = q_ref[0]
    z = jnp.zeros((HEAD_DIM, NA_PAIR_Q), BF16)
    npairs = NA_Q // NA_PAIR_Q
    scores, vals = [], []
    for u in range(npairs):
        mp = r * npairs + u
        srow = jnp.clip(2 * mp - NA_KH // 2, 0, rows - NA_PAIR_ROWS)
        start = pl.multiple_of(srow * GRID_W, 128)
        var = jnp.where(mp == 0, 0, jnp.where(mp == 1, 1, jnp.where(
            mp == rows // 2 - 2, 3, jnp.where(mp == rows // 2 - 1, 4, 2))))
        ks = k_ref[0, pl.ds(start, NA_PAIR_K), :]
        vals.append(v_ref[0, :, pl.ds(start, NA_PAIR_K)])
        qp = q[:, u * NA_PAIR_Q:(u + 1) * NA_PAIR_Q]
        qz = jnp.concatenate([jnp.concatenate([qp[:HEAD_DIM], z], axis=0),
                              jnp.concatenate([z, qp[HEAD_DIM:]], axis=0)], axis=1)
        scores.append(_dot(ks, qz) + bias_ref[0, var])
    probs, sums = [], []
    for s in scores:
        m = jnp.max(s, axis=0, keepdims=True)
        p = jnp.exp(s - m)
        sums.append(jnp.sum(p, axis=0, keepdims=True))
        probs.append(p.astype(BF16))
    outs = []
    for vs, p, l in zip(vals, probs, sums):
        o = _dot(vs, p) / l
        outs.append(jnp.concatenate([o[:HEAD_DIM, :NA_PAIR_Q], o[HEAD_DIM:, NA_PAIR_Q:]], axis=0))
    o_ref[0] = jnp.concatenate(outs, axis=1).T.astype(BF16)


def _na_attn(pt, kc, bias):
    B, _, S = pt.shape
    nr = S // NA_Q
    return pl.pallas_call(
        _na_kernel,
        grid=(B, 2, nr),
        in_specs=[
            pl.BlockSpec((1, 128, NA_Q), lambda b, p, r: (b, 8 + p, r)),
            pl.BlockSpec((1, S, 128), lambda b, p, r: (b, 0, 4 + p)),
            pl.BlockSpec((1, 128, S), lambda b, p, r: (b, 10 + p, 0)),
            pl.BlockSpec((1, 5, NA_PAIR_K, 2 * NA_PAIR_Q), lambda b, p, r: (p, 0, 0, 0)),
        ],
        out_specs=pl.BlockSpec((1, NA_Q, 128), lambda b, p, r: (b, r, p)),
        out_shape=jax.ShapeDtypeStruct((B, S, B_HEADS * HEAD_DIM), BF16),
        compiler_params=_cparams(("parallel", "parallel", "arbitrary"), 48),
        name="na_attn",
    )(pt, kc, pt, bias)


def _na_bias_index():
    rows = 32
    a = np.arange(2)[None, :]
    i = np.arange(NA_PAIR_ROWS)[:, None]
    ridx, rvalid = [], []
    for mp in (0, 1, 5, rows // 2 - 2, rows // 2 - 1):
        srow = np.clip(2 * mp - NA_KH // 2, 0, rows - NA_PAIR_ROWS)
        qrow = 2 * mp + a
        krow = srow + i
        rs = np.clip(qrow - NA_KH // 2, 0, rows - NA_KH)
        rvalid.append((krow >= rs) & (krow < rs + NA_KH))
        ridx.append(np.clip(krow - qrow + NA_KH - 1, 0, 2 * NA_KH - 2))
    ridx = np.stack(ridx)
    rvalid = np.stack(rvalid)
    qc = np.arange(GRID_W)[None, :]
    kcol = np.arange(GRID_W)[:, None]
    cs = np.clip(qc - NA_KW // 2, 0, GRID_W - NA_KW)
    cvalid = (kcol >= cs) & (kcol < cs + NA_KW)
    cidx = np.clip(kcol - qc + NA_KW - 1, 0, 2 * NA_KW - 2)
    valid = rvalid[:, :, None, :, None] & cvalid[None, None, :, None, :]
    col_onehot = (cidx.reshape(-1)[None, :] == np.arange(2 * NA_KW - 1)[:, None])
    return (ridx.astype(np.int32), col_onehot.astype(np.float32),
            valid.reshape(5, NA_PAIR_K, NA_PAIR_Q))


_NA_RIDX, _NA_COL_ONEHOT, _NA_VALID = _na_bias_index()


def _na_bias_table(rpb):
    rows = jnp.take(rpb.astype(F32), jnp.asarray(_NA_RIDX), axis=1)
    tab = jnp.einsum("hvkqc,cn->hvkqn", rows, jnp.asarray(_NA_COL_ONEHOT),
                     precision=lax.Precision.HIGHEST)
    tab = tab.reshape(B_HEADS, 5, NA_PAIR_ROWS, 2, GRID_W, GRID_W).transpose(0, 1, 2, 4, 3, 5)
    tab = tab.reshape(B_HEADS, 5, NA_PAIR_K, NA_PAIR_Q)
    tab = jnp.where(jnp.asarray(_NA_VALID)[None], tab, NEG)
    tab = tab.reshape(2, 2, 5, NA_PAIR_K, NA_PAIR_Q).transpose(0, 2, 3, 1, 4)
    return tab.reshape(2, 5, NA_PAIR_K, 2 * NA_PAIR_Q)


def _win_kernel(sink_ref, q_ref, k_ref, v_ref, o_ref):
    S = k_ref.shape[1]
    i = pl.program_id(1)
    q0 = i * TQ_C
    start = pl.multiple_of(jnp.clip(q0 - WINDOW, 0, S - TK_C), 128)
    ks = k_ref[0, pl.ds(start, TK_C), :]
    vs = v_ref[0, :, pl.ds(start, TK_C)]
    kpos = start + lax.broadcasted_iota(jnp.int32, (TK_C, TQ_C), 0)
    qpos = q0 + lax.broadcasted_iota(jnp.int32, (TK_C, TQ_C), 1)
    mask = jnp.where(jnp.abs(qpos - kpos) <= WINDOW, 0.0, NEG).astype(F32)
    z = jnp.zeros((HEAD_DIM, TQ_C), BF16)
    gq = C_HEADS // C_KV_HEADS
    scores = []
    for hq in range(C_HEADS):
        qh = q_ref[0, hq * HEAD_DIM:(hq + 1) * HEAD_DIM, :]
        qz = jnp.concatenate([qh, z] if hq // gq == 0 else [z, qh], axis=0)
        scores.append(_dot(ks, qz) + mask)
    probs, dens = [], []
    for hq, s in enumerate(scores):
        sk = sink_ref[hq]
        m = jnp.maximum(jnp.max(s, axis=0, keepdims=True), sk)
        e = jnp.exp(s - m)
        dens.append(jnp.sum(e, axis=0, keepdims=True) + jnp.exp(sk - m))
        probs.append(e.astype(BF16))
    outs = []
    for hq, (e, den) in enumerate(zip(probs, dens)):
        kv = hq // gq
        outs.append(_dot(vs[kv * HEAD_DIM:(kv + 1) * HEAD_DIM], e) / den)
    o_ref[0] = jnp.concatenate(outs, axis=0).T.astype(BF16)


def _win_attn(pt, kc, sink):
    B, _, S = pt.shape
    return pl.pallas_call(
        _win_kernel,
        grid=(B, S // TQ_C),
        in_specs=[
            pl.BlockSpec(memory_space=pltpu.SMEM),
            pl.BlockSpec((1, 256, TQ_C), lambda b, i: (b, 6, i)),
            pl.BlockSpec((1, S, 128), lambda b, i: (b, 0, 6)),
            pl.BlockSpec((1, 128, S), lambda b, i: (b, 14, 0)),
        ],
        out_specs=pl.BlockSpec((1, TQ_C, 256), lambda b, i: (b, i, 0)),
        out_shape=jax.ShapeDtypeStruct((B, S, C_HEADS * HEAD_DIM), BF16),
        compiler_params=_cparams(("parallel", "arbitrary"), 48),
        name="win_attn",
    )(sink, pt, kc, pt)


def _outproj_kernel(x_ref, ya_ref, yb_ref, yc_ref, wo_ref, g2_ref, wr_ref, br_ref, tri_ref,
                    h_ref, xs_ref, cs_ref, ri_ref, cnt_ref):
    tm = x_ref.shape[0]
    h = (x_ref[...]
         + _dot(ya_ref[...], wo_ref[0:512, :])
         + _dot(yb_ref[...], wo_ref[512:768, :])
         + _dot(yc_ref[...], wo_ref[768:1024, :]))
    h_ref[...] = h
    ms = jnp.mean(h * h, axis=-1, keepdims=True)
    n2 = h * lax.rsqrt(ms + EPS) * g2_ref[...]
    n2_hi = n2.astype(BF16)
    n2_lo = (n2 - n2_hi.astype(F32)).astype(BF16)
    both = _dot(n2_hi, wr_ref[...])
    logits = both[:, :LANES] + both[:, LANES:] + _dot(n2_lo, wr_ref[:, :LANES]) + br_ref[...]
    lane = lax.broadcasted_iota(jnp.int32, logits.shape, 1).astype(F32)
    big = float(LANES)
    ninf = -jnp.inf

    def first_argmax(vals):
        vmax = jnp.max(vals, axis=-1, keepdims=True)
        idx = jnp.min(jnp.where(vals == vmax, lane, big), axis=-1, keepdims=True)
        return vmax, idx

    gl = jnp.where(lane < N_GROUPS, logits, ninf)
    gmax, gidx = first_argmax(gl)
    gp = 1.0 / jnp.sum(jnp.exp(gl - gmax), axis=-1, keepdims=True)
    lo = N_GROUPS + EXPERTS_PER_GROUP * gidx
    el = jnp.where((lane >= lo) & (lane < lo + EXPERTS_PER_GROUP), logits, ninf)
    v1, i1 = first_argmax(el)
    el2 = jnp.where(lane == i1, ninf, el)
    v2, i2 = first_argmax(el2)
    t = jnp.exp(v2 - v1)
    w1 = 1.0 / (1.0 + t)
    w2 = t / (1.0 + t)
    comb = jnp.where(lane == i1, gp * w1, jnp.where(lane == i2, gp * w2, 0.0))

    onehot = jnp.where(lane == gidx, 1.0, 0.0)
    before = _dot(tri_ref[...], onehot.astype(BF16))
    cnt = jnp.sum(onehot, axis=0, keepdims=True)
    pos = jnp.sum(jnp.where(lane < gidx, cnt, 0.0) + jnp.where(lane == gidx, before, 0.0),
                  axis=-1, keepdims=True)
    ri = jnp.where(lane == 0.0, pos, comb)
    ri_ref[...] = ri
    hi = ri.astype(BF16)
    r1 = ri - hi.astype(F32)
    mid = r1.astype(BF16)
    low = (r1 - mid.astype(F32)).astype(BF16)
    rs = jnp.concatenate([hi, mid, low], axis=1)
    pos_t = jnp.broadcast_to(pos, (tm, LANES)).T
    want = lax.broadcasted_iota(jnp.int32, (tm, tm), 0).astype(F32)
    perm = jnp.where(pos_t[0:1, :] == want, 1.0, 0.0).astype(BF16)
    xs_ref[0, 0:tm, :] = _dot(perm, n2_hi).astype(BF16)
    c3 = _dot(perm, rs)
    cs_ref[0, 0:tm, :] = c3[:, :LANES] + c3[:, LANES:2 * LANES] + c3[:, 2 * LANES:]
    xs_ref[0, tm:, :] = jnp.zeros((MOE_PAD, D_MODEL), BF16)
    cs_ref[0, tm:, :] = jnp.zeros((MOE_PAD, LANES), F32)
    cnt_ref[0] = jnp.broadcast_to(cnt, (8, LANES)).astype(jnp.int32)


def _outproj(x2, ya, yb, yc, wo, g2, wr, br, tri):
    T = x2.shape[0]
    tm = T_MOE
    nt = T // tm
    rows = tm + MOE_PAD
    row = lambda i: (i, 0)
    fixed = lambda i: (0, 0)
    return pl.pallas_call(
        _outproj_kernel,
        grid=(nt,),
        in_specs=[
            pl.BlockSpec((tm, D_MODEL), row),
            pl.BlockSpec((tm, 512), row),
            pl.BlockSpec((tm, 256), row),
            pl.BlockSpec((tm, 256), row),
            pl.BlockSpec((D_MODEL, D_MODEL), fixed),
            pl.BlockSpec((1, D_MODEL), fixed),
            pl.BlockSpec((D_MODEL, 2 * LANES), fixed),
            pl.BlockSpec((1, LANES), fixed),
            pl.BlockSpec((tm, tm), fixed),
        ],
        out_specs=[
            pl.BlockSpec((tm, D_MODEL), row),
            pl.BlockSpec((1, rows, D_MODEL), lambda i: (i, 0, 0)),
            pl.BlockSpec((1, rows, LANES), lambda i: (i, 0, 0)),
            pl.BlockSpec((tm, LANES), row),
            pl.BlockSpec((1, 8, LANES), lambda i: (i, 0, 0)),
        ],
        out_shape=[
            jax.ShapeDtypeStruct((T, D_MODEL), F32),
            jax.ShapeDtypeStruct((nt, rows, D_MODEL), BF16),
            jax.ShapeDtypeStruct((nt, rows, LANES), F32),
            jax.ShapeDtypeStruct((T, LANES), F32),
            jax.ShapeDtypeStruct((nt, 8, LANES), jnp.int32),
        ],
        compiler_params=_cparams(("parallel",), 56),
        name="outproj",
    )(x2, ya, yb, yc, wo, g2, wr, br, tri)


def _moe_kernel(cnt_ref, xs_ref, cs_ref, wg_ref, wu_ref, wd_ref, o_ref, acc_ref):
    tiles = xs_ref.shape[0]
    nexp = wg_ref.shape[0]
    step = pl.program_id(1)
    e0 = step * nexp
    g = lax.shift_right_logical(e0, 2)

    @pl.when(step == 0)
    def _():
        acc_ref[...] = jnp.zeros_like(acc_ref)

    def ffn_pass(t, r0, size):
        r0 = pl.multiple_of(r0, 16)
        xs = xs_ref[t, pl.ds(r0, size), :]
        cs = cs_ref[t, pl.ds(r0, size), :]
        lane = lax.broadcasted_iota(jnp.int32, cs.shape, 1)
        out = None
        for j in range(nexp):
            hg = _dot(xs, wg_ref[j])
            hu = _dot(xs, wu_ref[j])
            w = jnp.sum(jnp.where(lane == N_GROUPS + e0 + j, cs, 0.0), axis=-1, keepdims=True)
            hh = (hg * jax.nn.sigmoid(hg) * hu * w).astype(BF16)
            d = _dot(hh, wd_ref[j])
            out = d if out is None else out + d
        acc_ref[t, pl.ds(r0, size), :] = acc_ref[t, pl.ds(r0, size), :] + out

    for t in range(tiles):
        i = pl.program_id(0) * tiles + t
        n = [cnt_ref[i * N_GROUPS + k] for k in range(N_GROUPS)]
        off = (jnp.where(g > 0, n[0], 0) + jnp.where(g > 1, n[1], 0) + jnp.where(g > 2, n[2], 0))
        ng = jnp.where(g == 0, n[0], jnp.where(g == 1, n[1], jnp.where(g == 2, n[2], n[3])))
        start = lax.shift_left(lax.shift_right_logical(off, 4), 4)
        need = off - start + ng
        nfull = jnp.maximum(lax.div(need - 1, MOE_SLAB), 0)
        rem_steps = lax.div(need - nfull * MOE_SLAB + (MOE_STEP - 1), MOE_STEP)

        def full_pass(s, carry, t=t, start=start):
            ffn_pass(t, start + s * MOE_SLAB, MOE_SLAB)
            return carry
        lax.fori_loop(0, nfull, full_pass, 0)

        for k in range(1, MOE_SLAB // MOE_STEP + 1):
            @pl.when(rem_steps == k)
            def _(k=k, t=t, start=start, nfull=nfull):
                ffn_pass(t, start + nfull * MOE_SLAB, k * MOE_STEP)

    @pl.when(step == pl.num_programs(1) - 1)
    def _():
        o_ref[...] = acc_ref[...].astype(BF16)


def _moe(xs, cs, cnt, wg, wu, wd):
    nt, rows, _ = xs.shape
    tpw = MOE_TILES_PER_WEIGHT if nt % MOE_TILES_PER_WEIGHT == 0 else 1
    nexp = MOE_EXPERTS_PER_STEP
    tmap = lambda s, e, c: (s, 0, 0)
    wmap = lambda s, e, c: (e, 0, 0)
    return pl.pallas_call(
        _moe_kernel,
        grid_spec=pltpu.PrefetchScalarGridSpec(
            num_scalar_prefetch=1,
            grid=(nt // tpw, N_EXPERTS // nexp),
            in_specs=[
                pl.BlockSpec((tpw, rows, D_MODEL), tmap),
                pl.BlockSpec((tpw, rows, LANES), tmap),
                pl.BlockSpec((nexp, D_MODEL, D_EXPERT), wmap),
                pl.BlockSpec((nexp, D_MODEL, D_EXPERT), wmap),
                pl.BlockSpec((nexp, D_EXPERT, D_MODEL), wmap),
            ],
            out_specs=pl.BlockSpec((tpw, rows, D_MODEL), tmap),
            scratch_shapes=[pltpu.VMEM((tpw, rows, D_MODEL), F32)],
        ),
        out_shape=jax.ShapeDtypeStruct((nt, rows, D_MODEL), BF16),
        compiler_params=_cparams(("parallel", "arbitrary"), 60),
        name="moe",
    )(cnt, xs, cs, wg, wu, wd)


def _unsort_kernel(os_ref, h_ref, ri_ref, o_ref):
    tm = h_ref.shape[0]
    want = lax.broadcasted_iota(jnp.int32, (tm, tm), 1).astype(F32)
    perm_t = jnp.where(ri_ref[:, 0:1] == want, 1.0, 0.0).astype(BF16)
    o_ref[...] = h_ref[...] + _dot(perm_t, os_ref[0, 0:tm, :])


def _unsort(os, h, ri):
    T = h.shape[0]
    tm = T_MOE
    rows = os.shape[1]
    row = lambda i: (i, 0)
    return pl.pallas_call(
        _unsort_kernel,
        grid=(T // tm,),
        in_specs=[
            pl.BlockSpec((1, rows, D_MODEL), lambda i: (i, 0, 0)),
            pl.BlockSpec((tm, D_MODEL), row),
            pl.BlockSpec((tm, LANES), row),
        ],
        out_specs=pl.BlockSpec((tm, D_MODEL), row),
        out_shape=jax.ShapeDtypeStruct((T, D_MODEL), F32),
        compiler_params=_cparams(("parallel",), 48),
        name="unsort",
    )(os, h, ri)


def _rope_tables_t(seq_len):
    pos = jnp.arange(seq_len, dtype=F32)
    inv = ROPE_THETA ** (-jnp.arange(0, HEAD_DIM, 2, dtype=F32) / HEAD_DIM)
    ang = pos[:, None] * inv[None, :]
    ang = jnp.concatenate([ang, ang], axis=-1)
    sign = jnp.where(jnp.arange(HEAD_DIM) < HEAD_DIM // 2, -1.0, 1.0).astype(F32)
    return jnp.cos(ang).T, (jnp.sin(ang) * sign[None, :]).T


def _lane_bcast(v):
    return jnp.broadcast_to(v.astype(F32)[:, None], (v.shape[0], LANES))


def _prep_layer(l, p):
    w_in = p["w_in"][l]
    cols = [w_in[:, _COLS[n][0]:_COLS[n][1]] for n in _PT_ORDER + _KT_ORDER]
    w_t = jnp.concatenate(cols, axis=1).T.astype(BF16)
    gains = jnp.stack([
        _lane_bcast(p["a_q_norm"][l] * (QK_SCALE * LOG2E)), _lane_bcast(p["a_k_norm"][l]),
        _lane_bcast(p["b_q_norm"][l] * QK_SCALE), _lane_bcast(p["b_k_norm"][l]),
        _lane_bcast(p["c_q_norm"][l] * QK_SCALE), _lane_bcast(p["c_k_norm"][l]),
    ])
    wr = jnp.zeros((D_MODEL, LANES), F32)
    wr = wr.at[:, :N_GROUPS].set(p["w_group"][l]).at[:, N_GROUPS:N_GROUPS + N_EXPERTS].set(p["w_router"][l])
    wr_hi = wr.astype(BF16)
    wr_lo = (wr - wr_hi.astype(F32)).astype(BF16)
    wr2 = jnp.concatenate([wr_hi, wr_lo], axis=1)
    br = jnp.zeros((1, LANES), F32)
    br = br.at[0, :N_GROUPS].set(p["b_group"][l]).at[0, N_GROUPS:N_GROUPS + N_EXPERTS].set(p["b_router"][l])
    bound = 1.02 * HEAD_DIM * QK_SCALE * jnp.max(jnp.abs(p["a_q_norm"][l])) * jnp.max(jnp.abs(p["a_k_norm"][l]))
    return dict(
        g1=p["norm1_g"][l].astype(F32)[None, :], w_t=w_t, gains=gains,
        lam=jnp.stack([p["a_lambda_q1"][l], p["a_lambda_k1"][l],
                       p["a_lambda_q2"][l], p["a_lambda_k2"][l]]).astype(F32),
        subg=_lane_bcast(p["a_subln"][l]),
        na_bias=_na_bias_table(p["b_rpb"][l]),
        sink=p["c_sink"][l].astype(F32),
        wo=p["w_out"][l].astype(BF16), g2=p["norm2_g"][l].astype(F32)[None, :],
        wr=wr2, br=br,
        wg=p["w_gate"][l].astype(BF16), wu=p["w_up"][l].astype(BF16), wd=p["w_down"][l].astype(BF16),
        fast_ok=bound <= FAST_SOFTMAX_BOUND,
    )


def _layer(x, lp, lambda_init, cos_t, sin_t):
    B, S, _ = x.shape
    pt, kc = _inproj(x, lp["g1"], lp["w_t"], lp["gains"], cos_t, sin_t)
    ya = lax.cond(
        lp["fast_ok"],
        lambda a, b: _diff_attn(a, b, lp["lam"], lp["subg"], lambda_init, False),
        lambda a, b: _diff_attn(a, b, lp["lam"], lp["subg"], lambda_init, True),
        pt, kc)
    yb = _na_attn(pt, kc, lp["na_bias"])
    yc = _win_attn(pt, kc, lp["sink"])
    T = B * S
    tri = jnp.tril(jnp.ones((T_MOE, T_MOE), BF16), -1)
    h, xs, cs, ri, cnt = _outproj(x.reshape(T, D_MODEL), ya.reshape(T, -1), yb.reshape(T, -1),
                                  yc.reshape(T, -1), lp["wo"], lp["g2"], lp["wr"], lp["br"], tri)
    counts = cnt[:, 0, :N_GROUPS].reshape(-1)
    os = _moe(xs, cs, counts, lp["wg"], lp["wu"], lp["wd"])
    y = _unsort(os, h, ri)
    return y.reshape(B, S, D_MODEL)


def kernel(x_prompt, x_sample, norm1_g, w_in, w_out, a_q_norm, a_k_norm, a_lambda_q1, a_lambda_k1,
           a_lambda_q2, a_lambda_k2, a_subln, b_q_norm, b_k_norm, b_rpb, c_q_norm, c_k_norm, c_sink,
           norm2_g, w_group, b_group, w_router, b_router, w_gate, w_up, w_down):
    p = dict(norm1_g=norm1_g, w_in=w_in, w_out=w_out, a_q_norm=a_q_norm, a_k_norm=a_k_norm,
             a_lambda_q1=a_lambda_q1, a_lambda_k1=a_lambda_k1, a_lambda_q2=a_lambda_q2,
             a_lambda_k2=a_lambda_k2, a_subln=a_subln, b_q_norm=b_q_norm, b_k_norm=b_k_norm,
             b_rpb=b_rpb, c_q_norm=c_q_norm, c_k_norm=c_k_norm, c_sink=c_sink, norm2_g=norm2_g,
             w_group=w_group, b_group=b_group, w_router=w_router, b_router=b_router,
             w_gate=w_gate, w_up=w_up, w_down=w_down)
    layers = [_prep_layer(l, p) for l in range(DEPTH)]

    def run(x):
        cos_t, sin_t = _rope_tables_t(x.shape[1])
        for l in range(DEPTH):
            lambda_init = 0.8 - 0.6 * math.exp(-0.3 * l)
            x = _layer(x, layers[l], lambda_init, cos_t, sin_t)
        return x

    return (run(x_prompt), run(x_sample))
```

```python
import functools
import math

import numpy as np
import jax
import jax.numpy as jnp
from jax import lax
from jax.experimental import pallas as pl
from jax.experimental.pallas import tpu as pltpu

F32 = jnp.float32
BF16 = jnp.bfloat16

D_MODEL = 1024
DEPTH = 2
HEAD_DIM = 64
GRID_W = 64
EPS = 1e-6
ROPE_THETA = 10000.0
A_HEADS = 4
A_VDIM = 2 * HEAD_DIM
B_HEADS = 4
NA_KH = 8
NA_KW = 16
C_HEADS = 4
C_KV_HEADS = 2
WINDOW = 128
N_GROUPS = 4
EXPERTS_PER_GROUP = 4
N_EXPERTS = 16
D_EXPERT = 512
QK_SCALE = HEAD_DIM ** -0.5
LOG2E = math.log2(math.e)

_COLS = dict(aq=(0, 512), ak=(512, 1024), av=(1024, 1536), bq=(1536, 1792), bk=(1792, 2048),
             bv=(2048, 2304), cq=(2304, 2560), ck=(2560, 2688), cv=(2688, 2816))
_PT_ORDER = ("aq", "av", "bq", "bv", "cq", "cv")
_KT_ORDER = ("ak", "bk", "ck")
PT_ROWS = 1920
KC_COLS = 896
IN_WIDTH = 2816

LANES = 128
NEG = -1e30

TM_PROJ = 1024
TQ_A = 1024
TK_A = 2048
UNROLL_A = 2
NA_Q = 1024
NA_PAIR_Q = 2 * GRID_W
NA_PAIR_ROWS = NA_KH + 2
NA_PAIR_K = NA_PAIR_ROWS * GRID_W
TQ_C = 256
TK_C = TQ_C + 2 * WINDOW
T_MOE = 1024
MOE_SLAB = 512
MOE_STEP = 128
MOE_PAD = 144
MOE_TILES_PER_WEIGHT = 2
MOE_EXPERTS_PER_STEP = 2
FAST_SOFTMAX_BOUND = 30.0


def _cparams(sem, vmem_mb):
    return pltpu.CompilerParams(dimension_semantics=sem, vmem_limit_bytes=vmem_mb * 1024 * 1024)


def _nt_dot(a, b):
    return lax.dot_general(a, b, (((1,), (1,)), ((), ())), preferred_element_type=F32)


def _dot(a, b):
    return jnp.dot(a, b, preferred_element_type=F32)


def _inproj_kernel(x_ref, g1_ref, w_ref, gains_ref, cos_ref, sin_ref, pt_ref, kc_ref):
    tm = x_ref.shape[1]
    x = x_ref[0]
    ms = jnp.mean(x * x, axis=-1, keepdims=True)
    n = (x * lax.rsqrt(ms + EPS) * g1_ref[...]).astype(BF16)
    cos = cos_ref[...]
    sin = sin_ref[...]
    reps = tm // LANES

    def project(r0, r1):
        return _nt_dot(w_ref[r0:r1, :], n)

    def head(chunk, r0, gi, rope):
        xh = chunk[r0:r0 + HEAD_DIM, :]
        ssq = jnp.sum(xh * xh, axis=0, keepdims=True)
        gain = pltpu.repeat(gains_ref[gi], reps, axis=1)
        xg = xh * lax.rsqrt(ssq * (1.0 / HEAD_DIM) + EPS) * gain
        if rope:
            swapped = jnp.concatenate([xg[HEAD_DIM // 2:], xg[:HEAD_DIM // 2]], axis=0)
            xg = xg * cos + swapped * sin
        return xg

    def put(r0, val):
        pt_ref[0, r0:r0 + val.shape[0], :] = val.astype(BF16)

    c = project(0, 512)
    for j in range(8):
        put(j * 64, head(c, j * 64, 0, True))
    put(512, project(512, 1024))
    c = project(1024, 1536)
    for j in range(4):
        put(1024 + j * 64, head(c, j * 64, 2, False))
    put(1280, c[256:512, :])
    c = project(1536, 1920)
    for j in range(4):
        put(1536 + j * 64, head(c, j * 64, 4, True))
    put(1792, c[256:384, :])
    def put_keys(chunk, j0, specs):
        for j, (gi, rope) in enumerate(specs):
            blk = jnp.concatenate([head(chunk, j * 128, gi, rope),
                                   head(chunk, j * 128 + 64, gi, rope)], axis=0)
            kc_ref[0, :, (j0 + j) * 128:(j0 + j + 1) * 128] = blk.T.astype(BF16)

    put_keys(project(PT_ROWS, PT_ROWS + 512), 0, [(1, True)] * 4)
    put_keys(project(PT_ROWS + 512, IN_WIDTH), 4, [(3, False)] * 2 + [(5, True)])


def _inproj(x, g1, w_t, gains, cos_t, sin_t):
    B, S, _ = x.shape
    tm = TM_PROJ
    return pl.pallas_call(
        _inproj_kernel,
        grid=(B, S // tm),
        in_specs=[
            pl.BlockSpec((1, tm, D_MODEL), lambda b, i: (b, i, 0)),
            pl.BlockSpec((1, D_MODEL), lambda b, i: (0, 0)),
            pl.BlockSpec((IN_WIDTH, D_MODEL), lambda b, i: (0, 0)),
            pl.BlockSpec((6, HEAD_DIM, LANES), lambda b, i: (0, 0, 0)),
            pl.BlockSpec((HEAD_DIM, tm), lambda b, i: (0, i)),
            pl.BlockSpec((HEAD_DIM, tm), lambda b, i: (0, i)),
        ],
        out_specs=[
            pl.BlockSpec((1, PT_ROWS, tm), lambda b, i: (b, 0, i)),
            pl.BlockSpec((1, tm, KC_COLS), lambda b, i: (b, i, 0)),
        ],
        out_shape=[
            jax.ShapeDtypeStruct((B, PT_ROWS, S), BF16),
            jax.ShapeDtypeStruct((B, S, KC_COLS), BF16),
        ],
        compiler_params=_cparams(("parallel", "parallel"), 56),
        name="inproj",
    )(x, g1, w_t, gains, cos_t, sin_t)


def _diff_attn_kernel(q_ref, k_ref, v_ref, lam_ref, subg_ref, o_ref, qz_ref, acc_ref, l_ref, m_ref,
                      *, tk, safe, lambda_init):
    tq = q_ref.shape[2]
    S = k_ref.shape[1]
    q = q_ref[0]
    z = jnp.zeros((HEAD_DIM, tq), BF16)
    qz_ref[:, :tq] = jnp.concatenate([q[:HEAD_DIM], z], axis=0)
    qz_ref[:, tq:] = jnp.concatenate([z, q[HEAD_DIM:]], axis=0)
    acc_ref[...] = jnp.zeros_like(acc_ref)
    l_ref[...] = jnp.zeros_like(l_ref)
    if safe:
        m_ref[...] = jnp.full_like(m_ref, NEG)

    def body(i, carry):
        off = pl.multiple_of(i * tk, tk)
        kc = k_ref[0, pl.ds(off, tk), :]
        vc = v_ref[0, :, pl.ds(off, tk)]
        s = _dot(kc, qz_ref[...])
        if safe:
            m_old = m_ref[...]
            m_new = jnp.maximum(m_old, jnp.max(s, axis=0, keepdims=True))
            alpha = jnp.exp2(m_old - m_new)
            p = jnp.exp2(s - m_new)
            m_ref[...] = m_new
            l_ref[...] = alpha * l_ref[...] + jnp.sum(p, axis=0, keepdims=True)
            acc_ref[...] = alpha * acc_ref[...] + _dot(vc, p.astype(BF16))
        else:
            p = jnp.exp2(s)
            l_ref[...] = l_ref[...] + jnp.sum(p, axis=0, keepdims=True)
            acc_ref[...] = acc_ref[...] + _dot(vc, p.astype(BF16))
        return carry

    lax.fori_loop(0, S // tk, body, 0, unroll=UNROLL_A)

    lp = lam_ref[...]
    lam = (jnp.exp(jnp.sum(lp[0:1] * lp[1:2], axis=-1, keepdims=True))
           - jnp.exp(jnp.sum(lp[2:3] * lp[3:4], axis=-1, keepdims=True)) + lambda_init)
    on = acc_ref[...] / l_ref[...]
    o = on[:, :tq] - lam * on[:, tq:]
    ms = jnp.mean(o * o, axis=0, keepdims=True)
    gain = pltpu.repeat(subg_ref[...], tq // LANES, axis=1)
    y = o * lax.rsqrt(ms + EPS) * gain * (1.0 - lambda_init)
    o_ref[0] = y.T.astype(BF16)


def _diff_attn(pt, kc, lam_params, subg, lambda_init, safe):
    B, _, S = pt.shape
    tq, tk = TQ_A, min(TK_A, S)
    kern = functools.partial(_diff_attn_kernel, tk=tk, safe=safe, lambda_init=lambda_init)
    return pl.pallas_call(
        kern,
        grid=(B, A_HEADS, S // tq),
        in_specs=[
            pl.BlockSpec((1, 128, tq), lambda b, h, i: (b, h, i)),
            pl.BlockSpec((1, S, 128), lambda b, h, i: (b, 0, h)),
            pl.BlockSpec((1, 128, S), lambda b, h, i: (b, 4 + h, 0)),
            pl.BlockSpec((4, HEAD_DIM), lambda b, h, i: (0, 0)),
            pl.BlockSpec((A_VDIM, LANES), lambda b, h, i: (0, 0)),
        ],
        out_specs=pl.BlockSpec((1, tq, 128), lambda b, h, i: (b, i, h)),
        out_shape=jax.ShapeDtypeStruct((B, S, A_HEADS * A_VDIM), BF16),
        scratch_shapes=[
            pltpu.VMEM((128, 2 * tq), BF16),
            pltpu.VMEM((A_VDIM, 2 * tq), F32),
            pltpu.VMEM((1, 2 * tq), F32),
            pltpu.VMEM((1, 2 * tq), F32),
        ],
        compiler_params=_cparams(("parallel", "parallel", "arbitrary"), 48),
        name="diff_attn_safe" if safe else "diff_attn",
    )(pt, kc, pt, lam_params, subg)


def _na_kernel(q_ref, k_ref, v_ref, bias_ref, o_ref):
    S = k_ref.shape[1]
    rows = S // GRID_W
    r = pl.program_id(2)
    q = q_ref[0]
    z = jnp.zeros((HEAD_DIM, NA_PAIR_Q), BF16)
    npairs = NA_Q // NA_PAIR_Q
    scores, vals = [], []
    for u in range(npairs):
        mp = r * npairs + u
        srow = jnp.clip(2 * mp - NA_KH // 2, 0, rows - NA_PAIR_ROWS)
        start = pl.multiple_of(srow * GRID_W, 128)
        var = jnp.where(mp == 0, 0, jnp.where(mp == 1, 1, jnp.where(
            mp == rows // 2 - 2, 3, jnp.where(mp == rows // 2 - 1, 4, 2))))
        ks = k_ref[0, pl.ds(start, NA_PAIR_K), :]
        vals.append(v_ref[0, :, pl.ds(start, NA_PAIR_K)])
        qp = q[:, u * NA_PAIR_Q:(u + 1) * NA_PAIR_Q]
        qz = jnp.concatenate([jnp.concatenate([qp[:HEAD_DIM], z], axis=0),
                              jnp.concatenate([z, qp[HEAD_DIM:]], axis=0)], axis=1)
        scores.append(_dot(ks, qz) + bias_ref[0, var])
    probs, sums = [], []
    for s in scores:
        m = jnp.max(s, axis=0, keepdims=True)
        p = jnp.exp(s - m)
        sums.append(jnp.sum(p, axis=0, keepdims=True))
        probs.append(p.astype(BF16))
    outs = []
    for vs, p, l in zip(vals, probs, sums):
        o = _dot(vs, p) / l
        outs.append(jnp.concatenate([o[:HEAD_DIM, :NA_PAIR_Q], o[HEAD_DIM:, NA_PAIR_Q:]], axis=0))
    o_ref[0] = jnp.concatenate(outs, axis=1).T.astype(BF16)


def _na_attn(pt, kc, bias):
    B, _, S = pt.shape
    nr = S // NA_Q
    return pl.pallas_call(
        _na_kernel,
        grid=(B, 2, nr),
        in_specs=[
            pl.BlockSpec((1, 128, NA_Q), lambda b, p, r: (b, 8 + p, r)),
            pl.BlockSpec((1, S, 128), lambda b, p, r: (b, 0, 4 + p)),
            pl.BlockSpec((1, 128, S), lambda b, p, r: (b, 10 + p, 0)),
            pl.BlockSpec((1, 5, NA_PAIR_K, 2 * NA_PAIR_Q), lambda b, p, r: (p, 0, 0, 0)),
        ],
        out_specs=pl.BlockSpec((1, NA_Q, 128), lambda b, p, r: (b, r, p)),
        out_shape=jax.ShapeDtypeStruct((B, S, B_HEADS * HEAD_DIM), BF16),
        compiler_params=_cparams(("parallel", "parallel", "arbitrary"), 48),
        name="na_attn",
    )(pt, kc, pt, bias)


def _na_bias_index():
    rows = 32
    a = np.arange(2)[None, :]
    i = np.arange(NA_PAIR_ROWS)[:, None]
    ridx, rvalid = [], []
    for mp in (0, 1, 5, rows // 2 - 2, rows // 2 - 1):
        srow = np.clip(2 * mp - NA_KH // 2, 0, rows - NA_PAIR_ROWS)
        qrow = 2 * mp + a
        krow = srow + i
        rs = np.clip(qrow - NA_KH // 2, 0, rows - NA_KH)
        rvalid.append((krow >= rs) & (krow < rs + NA_KH))
        ridx.append(np.clip(krow - qrow + NA_KH - 1, 0, 2 * NA_KH - 2))
    ridx = np.stack(ridx)
    rvalid = np.stack(rvalid)
    qc = np.arange(GRID_W)[None, :]
    kcol = np.arange(GRID_W)[:, None]
    cs = np.clip(qc - NA_KW // 2, 0, GRID_W - NA_KW)
    cvalid = (kcol >= cs) & (kcol < cs + NA_KW)
    cidx = np.clip(kcol - qc + NA_KW - 1, 0, 2 * NA_KW - 2)
    valid = rvalid[:, :, None, :, None] & cvalid[None, None, :, None, :]
    col_onehot = (cidx.reshape(-1)[None, :] == np.arange(2 * NA_KW - 1)[:, None])
    return (ridx.astype(np.int32), col_onehot.astype(np.float32),
            valid.reshape(5, NA_PAIR_K, NA_PAIR_Q))


_NA_RIDX, _NA_COL_ONEHOT, _NA_VALID = _na_bias_index()


def _na_bias_table(rpb):
    rows = jnp.take(rpb.astype(F32), jnp.asarray(_NA_RIDX), axis=1)
    tab = jnp.einsum("hvkqc,cn->hvkqn", rows, jnp.asarray(_NA_COL_ONEHOT),
                     precision=lax.Precision.HIGHEST)
    tab = tab.reshape(B_HEADS, 5, NA_PAIR_ROWS, 2, GRID_W, GRID_W).transpose(0, 1, 2, 4, 3, 5)
    tab = tab.reshape(B_HEADS, 5, NA_PAIR_K, NA_PAIR_Q)
    tab = jnp.where(jnp.asarray(_NA_VALID)[None], tab, NEG)
    tab = tab.reshape(2, 2, 5, NA_PAIR_K, NA_PAIR_Q).transpose(0, 2, 3, 1, 4)
    return tab.reshape(2, 5, NA_PAIR_K, 2 * NA_PAIR_Q)


def _win_kernel(sink_ref, q_ref, k_ref, v_ref, o_ref):
    S = k_ref.shape[1]
    i = pl.program_id(1)
    q0 = i * TQ_C
    start = pl.multiple_of(jnp.clip(q0 - WINDOW, 0, S - TK_C), 128)
    ks = k_ref[0, pl.ds(start, TK_C), :]
    vs = v_ref[0, :, pl.ds(start, TK_C)]
    kpos = start + lax.broadcasted_iota(jnp.int32, (TK_C, TQ_C), 0)
    qpos = q0 + lax.broadcasted_iota(jnp.int32, (TK_C, TQ_C), 1)
    mask = jnp.where(jnp.abs(qpos - kpos) <= WINDOW, 0.0, NEG).astype(F32)
    z = jnp.zeros((HEAD_DIM, TQ_C), BF16)
    gq = C_HEADS // C_KV_HEADS
    scores = []
    for hq in range(C_HEADS):
        qh = q_ref[0, hq * HEAD_DIM:(hq + 1) * HEAD_DIM, :]
        qz = jnp.concatenate([qh, z] if hq // gq == 0 else [z, qh], axis=0)
        scores.append(_dot(ks, qz) + mask)
    probs, dens = [], []
    for hq, s in enumerate(scores):
        sk = sink_ref[hq]
        m = jnp.maximum(jnp.max(s, axis=0, keepdims=True), sk)
        e = jnp.exp(s - m)
        dens.append(jnp.sum(e, axis=0, keepdims=True) + jnp.exp(sk - m))
        probs.append(e.astype(BF16))
    outs = []
    for hq, (e, den) in enumerate(zip(probs, dens)):
        kv = hq // gq
        outs.append(_dot(vs[kv * HEAD_DIM:(kv + 1) * HEAD_DIM], e) / den)
    o_ref[0] = jnp.concatenate(outs, axis=0).T.astype(BF16)


def _win_attn(pt, kc, sink):
    B, _, S = pt.shape
    return pl.pallas_call(
        _win_kernel,
        grid=(B, S // TQ_C),
        in_specs=[
            pl.BlockSpec(memory_space=pltpu.SMEM),
            pl.BlockSpec((1, 256, TQ_C), lambda b, i: (b, 6, i)),
            pl.BlockSpec((1, S, 128), lambda b, i: (b, 0, 6)),
            pl.BlockSpec((1, 128, S), lambda b, i: (b, 14, 0)),
        ],
        out_specs=pl.BlockSpec((1, TQ_C, 256), lambda b, i: (b, i, 0)),
        out_shape=jax.ShapeDtypeStruct((B, S, C_HEADS * HEAD_DIM), BF16),
        compiler_params=_cparams(("parallel", "arbitrary"), 48),
        name="win_attn",
    )(sink, pt, kc, pt)


def _outproj_kernel(x_ref, ya_ref, yb_ref, yc_ref, wo_ref, g2_ref, wr_ref, br_ref, tri_ref,
                    h_ref, xs_ref, cs_ref, ri_ref, cnt_ref):
    tm = x_ref.shape[0]
    h = (x_ref[...]
         + _dot(ya_ref[...], wo_ref[0:512, :])
         + _dot(yb_ref[...], wo_ref[512:768, :])
         + _dot(yc_ref[...], wo_ref[768:1024, :]))
    h_ref[...] = h
    ms = jnp.mean(h * h, axis=-1, keepdims=True)
    n2 = h * lax.rsqrt(ms + EPS) * g2_ref[...]
    n2_hi = n2.astype(BF16)
    n2_lo = (n2 - n2_hi.astype(F32)).astype(BF16)
    both = _dot(n2_hi, wr_ref[...])
    logits = both[:, :LANES] + both[:, LANES:] + _dot(n2_lo, wr_ref[:, :LANES]) + br_ref[...]
    lane = lax.broadcasted_iota(jnp.int32, logits.shape, 1).astype(F32)
    big = float(LANES)
    ninf = -jnp.inf

    def first_argmax(vals):
        vmax = jnp.max(vals, axis=-1, keepdims=True)
        idx = jnp.min(jnp.where(vals == vmax, lane, big), axis=-1, keepdims=True)
        return vmax, idx

    gl = jnp.where(lane < N_GROUPS, logits, ninf)
    gmax, gidx = first_argmax(gl)
    gp = 1.0 / jnp.sum(jnp.exp(gl - gmax), axis=-1, keepdims=True)
    lo = N_GROUPS + EXPERTS_PER_GROUP * gidx
    el = jnp.where((lane >= lo) & (lane < lo + EXPERTS_PER_GROUP), logits, ninf)
    v1, i1 = first_argmax(el)
    el2 = jnp.where(lane == i1, ninf, el)
    v2, i2 = first_argmax(el2)
    t = jnp.exp(v2 - v1)
    w1 = 1.0 / (1.0 + t)
    w2 = t / (1.0 + t)
    comb = jnp.where(lane == i1, gp * w1, jnp.where(lane == i2, gp * w2, 0.0))

    onehot = jnp.where(lane == gidx, 1.0, 0.0)
    before = _dot(tri_ref[...], onehot.astype(BF16))
    cnt = jnp.sum(onehot, axis=0, keepdims=True)
    pos = jnp.sum(jnp.where(lane < gidx, cnt, 0.0) + jnp.where(lane == gidx, before, 0.0),
                  axis=-1, keepdims=True)
    ri = jnp.where(lane == 0.0, pos, comb)
    ri_ref[...] = ri
    hi = ri.astype(BF16)
    r1 = ri - hi.astype(F32)
    mid = r1.astype(BF16)
    low = (r1 - mid.astype(F32)).astype(BF16)
    rs = jnp.concatenate([hi, mid, low], axis=1)
    pos_t = jnp.broadcast_to(pos, (tm, LANES)).T
    want = lax.broadcasted_iota(jnp.int32, (tm, tm), 0).astype(F32)
    perm = jnp.where(pos_t[0:1, :] == want, 1.0, 0.0).astype(BF16)
    xs_ref[0, 0:tm, :] = _dot(perm, n2_hi).astype(BF16)
    c3 = _dot(perm, rs)
    cs_ref[0, 0:tm, :] = c3[:, :LANES] + c3[:, LANES:2 * LANES] + c3[:, 2 * LANES:]
    xs_ref[0, tm:, :] = jnp.zeros((MOE_PAD, D_MODEL), BF16)
    cs_ref[0, tm:, :] = jnp.zeros((MOE_PAD, LANES), F32)
    cnt_ref[0] = jnp.broadcast_to(cnt, (8, LANES)).astype(jnp.int32)


def _outproj(x2, ya, yb, yc, wo, g2, wr, br, tri):
    T = x2.shape[0]
    tm = T_MOE
    nt = T // tm
    rows = tm + MOE_PAD
    row = lambda i: (i, 0)
    fixed = lambda i: (0, 0)
    return pl.pallas_call(
        _outproj_kernel,
        grid=(nt,),
        in_specs=[
            pl.BlockSpec((tm, D_MODEL), row),
            pl.BlockSpec((tm, 512), row),
            pl.BlockSpec((tm, 256), row),
            pl.BlockSpec((tm, 256), row),
            pl.BlockSpec((D_MODEL, D_MODEL), fixed),
            pl.BlockSpec((1, D_MODEL), fixed),
            pl.BlockSpec((D_MODEL, 2 * LANES), fixed),
            pl.BlockSpec((1, LANES), fixed),
            pl.BlockSpec((tm, tm), fixed),
        ],
        out_specs=[
            pl.BlockSpec((tm, D_MODEL), row),
            pl.BlockSpec((1, rows, D_MODEL), lambda i: (i, 0, 0)),
            pl.BlockSpec((1, rows, LANES), lambda i: (i, 0, 0)),
            pl.BlockSpec((tm, LANES), row),
            pl.BlockSpec((1, 8, LANES), lambda i: (i, 0, 0)),
        ],
        out_shape=[
            jax.ShapeDtypeStruct((T, D_MODEL), F32),
            jax.ShapeDtypeStruct((nt, rows, D_MODEL), BF16),
            jax.ShapeDtypeStruct((nt, rows, LANES), F32),
            jax.ShapeDtypeStruct((T, LANES), F32),
            jax.ShapeDtypeStruct((nt, 8, LANES), jnp.int32),
        ],
        compiler_params=_cparams(("parallel",), 56),
        name="outproj",
    )(x2, ya, yb, yc, wo, g2, wr, br, tri)


def _moe_kernel(cnt_ref, xs_ref, cs_ref, wg_ref, wu_ref, wd_ref, o_ref, acc_ref):
    tiles = xs_ref.shape[0]
    nexp = wg_ref.shape[0]
    step = pl.program_id(1)
    e0 = step * nexp
    g = lax.shift_right_logical(e0, 2)

    @pl.when(step == 0)
    def _():
        acc_ref[...] = jnp.zeros_like(acc_ref)

    def ffn_pass(t, r0, size):
        r0 = pl.multiple_of(r0, 16)
        xs = xs_ref[t, pl.ds(r0, size), :]
        cs = cs_ref[t, pl.ds(r0, size), :]
        lane = lax.broadcasted_iota(jnp.int32, cs.shape, 1)
        out = None
        for j in range(nexp):
            hg = _dot(xs, wg_ref[j])
            hu = _dot(xs, wu_ref[j])
            w = jnp.sum(jnp.where(lane == N_GROUPS + e0 + j, cs, 0.0), axis=-1, keepdims=True)
            hh = (hg * jax.nn.sigmoid(hg) * hu * w).astype(BF16)
            d = _dot(hh, wd_ref[j])
            out = d if out is None else out + d
        acc_ref[t, pl.ds(r0, size), :] = acc_ref[t, pl.ds(r0, size), :] + out

    def tile_body(t, carry):
        i = pl.program_id(0) * tiles + t
        n = [cnt_ref[i * N_GROUPS + k] for k in range(N_GROUPS)]
        off = (jnp.where(g > 0, n[0], 0) + jnp.where(g > 1, n[1], 0) + jnp.where(g > 2, n[2], 0))
        ng = jnp.where(g == 0, n[0], jnp.where(g == 1, n[1], jnp.where(g == 2, n[2], n[3])))
        start = lax.shift_left(lax.shift_right_logical(off, 4), 4)
        need = off - start + ng
        nfull = jnp.maximum(lax.div(need - 1, MOE_SLAB), 0)
        rem_steps = lax.div(need - nfull * MOE_SLAB + (MOE_STEP - 1), MOE_STEP)

        def full_pass(s, c):
            ffn_pass(t, start + s * MOE_SLAB, MOE_SLAB)
            return c
        lax.fori_loop(0, nfull, full_pass, 0)

        for k in range(1, MOE_SLAB // MOE_STEP + 1):
            @pl.when(rem_steps == k)
            def _(k=k):
                ffn_pass(t, start + nfull * MOE_SLAB, k * MOE_STEP)
        return carry
    lax.fori_loop(0, tiles, tile_body, 0)

    @pl.when(step == pl.num_programs(1) - 1)
    def _():
        o_ref[...] = acc_ref[...].astype(BF16)


def _moe(xs, cs, cnt, wg, wu, wd):
    nt, rows, _ = xs.shape
    tpw = MOE_TILES_PER_WEIGHT if nt % MOE_TILES_PER_WEIGHT == 0 else 1
    nexp = MOE_EXPERTS_PER_STEP
    tmap = lambda s, e, c: (s, 0, 0)
    wmap = lambda s, e, c: (e, 0, 0)
    return pl.pallas_call(
        _moe_kernel,
        grid_spec=pltpu.PrefetchScalarGridSpec(
            num_scalar_prefetch=1,
            grid=(nt // tpw, N_EXPERTS // nexp),
            in_specs=[
                pl.BlockSpec((tpw, rows, D_MODEL), tmap),
                pl.BlockSpec((tpw, rows, LANES), tmap),
                pl.BlockSpec((nexp, D_MODEL, D_EXPERT), wmap),
                pl.BlockSpec((nexp, D_MODEL, D_EXPERT), wmap),
                pl.BlockSpec((nexp, D_EXPERT, D_MODEL), wmap),
            ],
            out_specs=pl.BlockSpec((tpw, rows, D_MODEL), tmap),
            scratch_shapes=[pltpu.VMEM((tpw, rows, D_MODEL), F32)],
        ),
        out_shape=jax.ShapeDtypeStruct((nt, rows, D_MODEL), BF16),
        compiler_params=_cparams(("parallel", "arbitrary"), 60),
        name="moe",
    )(cnt, xs, cs, wg, wu, wd)


def _unsort_kernel(os_ref, h_ref, ri_ref, o_ref):
    tm = h_ref.shape[0]
    want = lax.broadcasted_iota(jnp.int32, (tm, tm), 1).astype(F32)
    perm_t = jnp.where(ri_ref[:, 0:1] == want, 1.0, 0.0).astype(BF16)
    o_ref[...] = h_ref[...] + _dot(perm_t, os_ref[0, 0:tm, :])


def _unsort(os, h, ri):
    T = h.shape[0]
    tm = T_MOE
    rows = os.shape[1]
    row = lambda i: (i, 0)
    return pl.pallas_call(
        _unsort_kernel,
        grid=(T // tm,),
        in_specs=[
            pl.BlockSpec((1, rows, D_MODEL), lambda i: (i, 0, 0)),
            pl.BlockSpec((tm, D_MODEL), row),
            pl.BlockSpec((tm, LANES), row),
        ],
        out_specs=pl.BlockSpec((tm, D_MODEL), row),
        out_shape=jax.ShapeDtypeStruct((T, D_MODEL), F32),
        compiler_params=_cparams(("parallel",), 48),
        name="unsort",
    )(os, h, ri)


def _rope_tables_t(seq_len):
    pos = jnp.arange(seq_len, dtype=F32)
    inv = ROPE_THETA ** (-jnp.arange(0, HEAD_DIM, 2, dtype=F32) / HEAD_DIM)
    ang = pos[:, None] * inv[None, :]
    ang = jnp.concatenate([ang, ang], axis=-1)
    sign = jnp.where(jnp.arange(HEAD_DIM) < HEAD_DIM // 2, -1.0, 1.0).astype(F32)
    return jnp.cos(ang).T, (jnp.sin(ang) * sign[None, :]).T


def _lane_bcast(v):
    return jnp.broadcast_to(v.astype(F32)[:, None], (v.shape[0], LANES))


def _prep_layer(l, p):
    w_in = p["w_in"][l]
    cols = [w_in[:, _COLS[n][0]:_COLS[n][1]] for n in _PT_ORDER + _KT_ORDER]
    w_t = jnp.concatenate(cols, axis=1).T.astype(BF16)
    gains = jnp.stack([
        _lane_bcast(p["a_q_norm"][l] * (QK_SCALE * LOG2E)), _lane_bcast(p["a_k_norm"][l]),
        _lane_bcast(p["b_q_norm"][l] * QK_SCALE), _lane_bcast(p["b_k_norm"][l]),
        _lane_bcast(p["c_q_norm"][l] * QK_SCALE), _lane_bcast(p["c_k_norm"][l]),
    ])
    wr = jnp.zeros((D_MODEL, LANES), F32)
    wr = wr.at[:, :N_GROUPS].set(p["w_group"][l]).at[:, N_GROUPS:N_GROUPS + N_EXPERTS].set(p["w_router"][l])
    wr_hi = wr.astype(BF16)
    wr_lo = (wr - wr_hi.astype(F32)).astype(BF16)
    wr2 = jnp.concatenate([wr_hi, wr_lo], axis=1)
    br = jnp.zeros((1, LANES), F32)
    br = br.at[0, :N_GROUPS].set(p["b_group"][l]).at[0, N_GROUPS:N_GROUPS + N_EXPERTS].set(p["b_router"][l])
    bound = 1.02 * HEAD_DIM * QK_SCALE * jnp.max(jnp.abs(p["a_q_norm"][l])) * jnp.max(jnp.abs(p["a_k_norm"][l]))
    return dict(
        g1=p["norm1_g"][l].astype(F32)[None, :], w_t=w_t, gains=gains,
        lam=jnp.stack([p["a_lambda_q1"][l], p["a_lambda_k1"][l],
                       p["a_lambda_q2"][l], p["a_lambda_k2"][l]]).astype(F32),
        subg=_lane_bcast(p["a_subln"][l]),
        na_bias=_na_bias_table(p["b_rpb"][l]),
        sink=p["c_sink"][l].astype(F32),
        wo=p["w_out"][l].astype(BF16), g2=p["norm2_g"][l].astype(F32)[None, :],
        wr=wr2, br=br,
        wg=p["w_gate"][l].astype(BF16), wu=p["w_up"][l].astype(BF16), wd=p["w_down"][l].astype(BF16),
        fast_ok=bound <= FAST_SOFTMAX_BOUND,
    )


def _layer(x, lp, lambda_init, cos_t, sin_t):
    B, S, _ = x.shape
    pt, kc = _inproj(x, lp["g1"], lp["w_t"], lp["gains"], cos_t, sin_t)
    ya = lax.cond(
        lp["fast_ok"],
        lambda a, b: _diff_attn(a, b, lp["lam"], lp["subg"], lambda_init, False),
        lambda a, b: _diff_attn(a, b, lp["lam"], lp["subg"], lambda_init, True),
        pt, kc)
    yb = _na_attn(pt, kc, lp["na_bias"])
    yc = _win_attn(pt, kc, lp["sink"])
    T = B * S
    tri = jnp.tril(jnp.ones((T_MOE, T_MOE), BF16), -1)
    h, xs, cs, ri, cnt = _outproj(x.reshape(T, D_MODEL), ya.reshape(T, -1), yb.reshape(T, -1),
                                  yc.reshape(T, -1), lp["wo"], lp["g2"], lp["wr"], lp["br"], tri)
    counts = cnt[:, 0, :N_GROUPS].reshape(-1)
    os = _moe(xs, cs, counts, lp["wg"], lp["wu"], lp["wd"])
    y = _unsort(os, h, ri)
    return y.reshape(B, S, D_MODEL)


def kernel(x_prompt, x_sample, norm1_g, w_in, w_out, a_q_norm, a_k_norm, a_lambda_q1, a_lambda_k1,
           a_lambda_q2, a_lambda_k2, a_subln, b_q_norm, b_k_norm, b_rpb, c_q_norm, c_k_norm, c_sink,
           norm2_g, w_group, b_group, w_router, b_router, w_gate, w_up, w_down):
    p = dict(norm1_g=norm1_g, w_in=w_in, w_out=w_out, a_q_norm=a_q_norm, a_k_norm=a_k_norm,
             a_lambda_q1=a_lambda_q1, a_lambda_k1=a_lambda_k1, a_lambda_q2=a_lambda_q2,
             a_lambda_k2=a_lambda_k2, a_subln=a_subln, b_q_norm=b_q_norm, b_k_norm=b_k_norm,
             b_rpb=b_rpb, c_q_norm=c_q_norm, c_k_norm=c_k_norm, c_sink=c_sink, norm2_g=norm2_g,
             w_group=w_group, b_group=b_group, w_router=w_router, b_router=b_router,
             w_gate=w_gate, w_up=w_up, w_down=w_down)
    layers = [_prep_layer(l, p) for l in range(DEPTH)]

    def run(x):
        cos_t, sin_t = _rope_tables_t(x.shape[1])
        for l in range(DEPTH):
            lambda_init = 0.8 - 0.6 * math.exp(-0.3 * l)
            x = _layer(x, layers[l], lambda_init, cos_t, sin_t)
        return x

    return (run(x_prompt), run(x_sample))
```

```python
import functools
import math

import numpy as np
import jax
import jax.numpy as jnp
from jax import lax
from jax.experimental import pallas as pl
from jax.experimental.pallas import tpu as pltpu

F32 = jnp.float32
BF16 = jnp.bfloat16

D_MODEL = 1024
DEPTH = 2
HEAD_DIM = 64
GRID_W = 64
EPS = 1e-6
ROPE_THETA = 10000.0
A_HEADS = 4
A_VDIM = 2 * HEAD_DIM
B_HEADS = 4
NA_KH = 8
NA_KW = 16
C_HEADS = 4
C_KV_HEADS = 2
WINDOW = 128
N_GROUPS = 4
EXPERTS_PER_GROUP = 4
N_EXPERTS = 16
D_EXPERT = 512
QK_SCALE = HEAD_DIM ** -0.5
LOG2E = math.log2(math.e)

_COLS = dict(aq=(0, 512), ak=(512, 1024), av=(1024, 1536), bq=(1536, 1792), bk=(1792, 2048),
             bv=(2048, 2304), cq=(2304, 2560), ck=(2560, 2688), cv=(2688, 2816))
_PT_ORDER = ("aq", "av", "bq", "bv", "cq", "cv")
_KT_ORDER = ("ak", "bk", "ck")
PT_ROWS = 1920
KC_COLS = 896
IN_WIDTH = 2816

LANES = 128
NEG = -1e30

TM_PROJ = 1024
TQ_A = 1024
TK_A = 2048
UNROLL_A = 2
NA_Q = 1024
NA_PAIR_Q = 2 * GRID_W
NA_PAIR_ROWS = NA_KH + 2
NA_PAIR_K = NA_PAIR_ROWS * GRID_W
TQ_C = 256
TK_C = TQ_C + 2 * WINDOW
T_MOE = 1024
MOE_SLAB = 512
MOE_STEP = 64
MOE_PAD = 80
MOE_TILES_PER_WEIGHT = 2
MOE_EXPERTS_PER_STEP = 2
FAST_SOFTMAX_BOUND = 30.0


def _cparams(sem, vmem_mb):
    return pltpu.CompilerParams(dimension_semantics=sem, vmem_limit_bytes=vmem_mb * 1024 * 1024)


def _nt_dot(a, b):
    return lax.dot_general(a, b, (((1,), (1,)), ((), ())), preferred_element_type=F32)


def _dot(a, b):
    return jnp.dot(a, b, preferred_element_type=F32)


def _inproj_kernel(x_ref, g1_ref, w_ref, gains_ref, cos_ref, sin_ref, pt_ref, kc_ref):
    tm = x_ref.shape[1]
    x = x_ref[0]
    ms = jnp.mean(x * x, axis=-1, keepdims=True)
    n = (x * lax.rsqrt(ms + EPS) * g1_ref[...]).astype(BF16)
    cos = cos_ref[...]
    sin = sin_ref[...]
    reps = tm // LANES

    def project(r0, r1):
        return _nt_dot(w_ref[r0:r1, :], n)

    def head(chunk, r0, gi, rope):
        xh = chunk[r0:r0 + HEAD_DIM, :]
        ssq = jnp.sum(xh * xh, axis=0, keepdims=True)
        gain = pltpu.repeat(gains_ref[gi], reps, axis=1)
        xg = xh * lax.rsqrt(ssq * (1.0 / HEAD_DIM) + EPS) * gain
        if rope:
            swapped = jnp.concatenate([xg[HEAD_DIM // 2:], xg[:HEAD_DIM // 2]], axis=0)
            xg = xg * cos + swapped * sin
        return xg

    def put(r0, val):
        pt_ref[0, r0:r0 + val.shape[0], :] = val.astype(BF16)

    c = project(0, 512)
    for j in range(8):
        put(j * 64, head(c, j * 64, 0, True))
    put(512, project(512, 1024))
    c = project(1024, 1536)
    for j in range(4):
        put(1024 + j * 64, head(c, j * 64, 2, False))
    put(1280, c[256:512, :])
    c = project(1536, 1920)
    for j in range(4):
        put(1536 + j * 64, head(c, j * 64, 4, True))
    put(1792, c[256:384, :])
    def put_keys(chunk, j0, specs):
        for j, (gi, rope) in enumerate(specs):
            blk = jnp.concatenate([head(chunk, j * 128, gi, rope),
                                   head(chunk, j * 128 + 64, gi, rope)], axis=0)
            kc_ref[0, :, (j0 + j) * 128:(j0 + j + 1) * 128] = blk.T.astype(BF16)

    put_keys(project(PT_ROWS, PT_ROWS + 512), 0, [(1, True)] * 4)
    put_keys(project(PT_ROWS + 512, IN_WIDTH), 4, [(3, False)] * 2 + [(5, True)])


def _inproj(x, g1, w_t, gains, cos_t, sin_t):
    B, S, _ = x.shape
    tm = TM_PROJ
    return pl.pallas_call(
        _inproj_kernel,
        grid=(B, S // tm),
        in_specs=[
            pl.BlockSpec((1, tm, D_MODEL), lambda b, i: (b, i, 0)),
            pl.BlockSpec((1, D_MODEL), lambda b, i: (0, 0)),
            pl.BlockSpec((IN_WIDTH, D_MODEL), lambda b, i: (0, 0)),
            pl.BlockSpec((6, HEAD_DIM, LANES), lambda b, i: (0, 0, 0)),
            pl.BlockSpec((HEAD_DIM, tm), lambda b, i: (0, i)),
            pl.BlockSpec((HEAD_DIM, tm), lambda b, i: (0, i)),
        ],
        out_specs=[
            pl.BlockSpec((1, PT_ROWS, tm), lambda b, i: (b, 0, i)),
            pl.BlockSpec((1, tm, KC_COLS), lambda b, i: (b, i, 0)),
        ],
        out_shape=[
            jax.ShapeDtypeStruct((B, PT_ROWS, S), BF16),
            jax.ShapeDtypeStruct((B, S, KC_COLS), BF16),
        ],
        compiler_params=_cparams(("parallel", "parallel"), 56),
        name="inproj",
    )(x, g1, w_t, gains, cos_t, sin_t)


def _diff_attn_kernel(q_ref, k_ref, v_ref, lam_ref, subg_ref, o_ref, qz_ref, acc_ref, l_ref, m_ref,
                      *, tk, safe, lambda_init):
    tq = q_ref.shape[2]
    S = k_ref.shape[1]
    q = q_ref[0]
    z = jnp.zeros((HEAD_DIM, tq), BF16)
    qz_ref[:, :tq] = jnp.concatenate([q[:HEAD_DIM], z], axis=0)
    qz_ref[:, tq:] = jnp.concatenate([z, q[HEAD_DIM:]], axis=0)
    acc_ref[...] = jnp.zeros_like(acc_ref)
    l_ref[...] = jnp.zeros_like(l_ref)
    if safe:
        m_ref[...] = jnp.full_like(m_ref, NEG)

    def body(i, carry):
        off = pl.multiple_of(i * tk, tk)
        kc = k_ref[0, pl.ds(off, tk), :]
        vc = v_ref[0, :, pl.ds(off, tk)]
        s = _dot(kc, qz_ref[...])
        if safe:
            m_old = m_ref[...]
            m_new = jnp.maximum(m_old, jnp.max(s, axis=0, keepdims=True))
            alpha = jnp.exp2(m_old - m_new)
            p = jnp.exp2(s - m_new)
            m_ref[...] = m_new
            l_ref[...] = alpha * l_ref[...] + jnp.sum(p, axis=0, keepdims=True)
            acc_ref[...] = alpha * acc_ref[...] + _dot(vc, p.astype(BF16))
        else:
            p = jnp.exp2(s)
            l_ref[...] = l_ref[...] + jnp.sum(p, axis=0, keepdims=True)
            acc_ref[...] = acc_ref[...] + _dot(vc, p.astype(BF16))
        return carry

    lax.fori_loop(0, S // tk, body, 0, unroll=UNROLL_A)

    lp = lam_ref[...]
    lam = (jnp.exp(jnp.sum(lp[0:1] * lp[1:2], axis=-1, keepdims=True))
           - jnp.exp(jnp.sum(lp[2:3] * lp[3:4], axis=-1, keepdims=True)) + lambda_init)
    on = acc_ref[...] / l_ref[...]
    o = on[:, :tq] - lam * on[:, tq:]
    ms = jnp.mean(o * o, axis=0, keepdims=True)
    gain = pltpu.repeat(subg_ref[...], tq // LANES, axis=1)
    y = o * lax.rsqrt(ms + EPS) * gain * (1.0 - lambda_init)
    o_ref[0] = y.T.astype(BF16)


def _diff_attn(pt, kc, lam_params, subg, lambda_init, safe):
    B, _, S = pt.shape
    tq, tk = TQ_A, min(TK_A, S)
    kern = functools.partial(_diff_attn_kernel, tk=tk, safe=safe, lambda_init=lambda_init)
    return pl.pallas_call(
        kern,
        grid=(B, A_HEADS, S // tq),
        in_specs=[
            pl.BlockSpec((1, 128, tq), lambda b, h, i: (b, h, i)),
            pl.BlockSpec((1, S, 128), lambda b, h, i: (b, 0, h)),
            pl.BlockSpec((1, 128, S), lambda b, h, i: (b, 4 + h, 0)),
            pl.BlockSpec((4, HEAD_DIM), lambda b, h, i: (0, 0)),
            pl.BlockSpec((A_VDIM, LANES), lambda b, h, i: (0, 0)),
        ],
        out_specs=pl.BlockSpec((1, tq, 128), lambda b, h, i: (b, i, h)),
        out_shape=jax.ShapeDtypeStruct((B, S, A_HEADS * A_VDIM), BF16),
        scratch_shapes=[
            pltpu.VMEM((128, 2 * tq), BF16),
            pltpu.VMEM((A_VDIM, 2 * tq), F32),
            pltpu.VMEM((1, 2 * tq), F32),
            pltpu.VMEM((1, 2 * tq), F32),
        ],
        compiler_params=_cparams(("parallel", "parallel", "arbitrary"), 48),
        name="diff_attn_safe" if safe else "diff_attn",
    )(pt, kc, pt, lam_params, subg)


def _na_kernel(q_ref, k_ref, v_ref, bias_ref, o_ref, *, safe):
    S = k_ref.shape[1]
    rows = S // GRID_W
    r = pl.program_id(2)
    q = q_ref[0]
    z = jnp.zeros((HEAD_DIM, NA_PAIR_Q), BF16)
    npairs = NA_Q // NA_PAIR_Q
    scores, vals = [], []
    for u in range(npairs):
        mp = r * npairs + u
        srow = jnp.clip(2 * mp - NA_KH // 2, 0, rows - NA_PAIR_ROWS)
        start = pl.multiple_of(srow * GRID_W, 128)
        var = jnp.where(mp == 0, 0, jnp.where(mp == 1, 1, jnp.where(
            mp == rows // 2 - 2, 3, jnp.where(mp == rows // 2 - 1, 4, 2))))
        ks = k_ref[0, pl.ds(start, NA_PAIR_K), :]
        vals.append(v_ref[0, :, pl.ds(start, NA_PAIR_K)])
        qp = q[:, u * NA_PAIR_Q:(u + 1) * NA_PAIR_Q]
        qz = jnp.concatenate([jnp.concatenate([qp[:HEAD_DIM], z], axis=0),
                              jnp.concatenate([z, qp[HEAD_DIM:]], axis=0)], axis=1)
        scores.append(_dot(ks, qz) + bias_ref[0, var])
    probs, sums = [], []
    for s in scores:
        p = jnp.exp(s - jnp.max(s, axis=0, keepdims=True)) if safe else jnp.exp(s)
        sums.append(jnp.sum(p, axis=0, keepdims=True))
        probs.append(p.astype(BF16))
    outs = []
    for vs, p, l in zip(vals, probs, sums):
        o = _dot(vs, p) / l
        outs.append(jnp.concatenate([o[:HEAD_DIM, :NA_PAIR_Q], o[HEAD_DIM:, NA_PAIR_Q:]], axis=0))
    o_ref[0] = jnp.concatenate(outs, axis=1).T.astype(BF16)


def _na_attn(pt, kc, bias, safe):
    B, _, S = pt.shape
    nr = S // NA_Q
    return pl.pallas_call(
        functools.partial(_na_kernel, safe=safe),
        grid=(B, 2, nr),
        in_specs=[
            pl.BlockSpec((1, 128, NA_Q), lambda b, p, r: (b, 8 + p, r)),
            pl.BlockSpec((1, S, 128), lambda b, p, r: (b, 0, 4 + p)),
            pl.BlockSpec((1, 128, S), lambda b, p, r: (b, 10 + p, 0)),
            pl.BlockSpec((1, 5, NA_PAIR_K, 2 * NA_PAIR_Q), lambda b, p, r: (p, 0, 0, 0)),
        ],
        out_specs=pl.BlockSpec((1, NA_Q, 128), lambda b, p, r: (b, r, p)),
        out_shape=jax.ShapeDtypeStruct((B, S, B_HEADS * HEAD_DIM), BF16),
        compiler_params=_cparams(("parallel", "parallel", "arbitrary"), 48),
        name="na_attn_safe" if safe else "na_attn",
    )(pt, kc, pt, bias)


def _na_bias_index():
    rows = 32
    a = np.arange(2)[None, :]
    i = np.arange(NA_PAIR_ROWS)[:, None]
    ridx, rvalid = [], []
    for mp in (0, 1, 5, rows // 2 - 2, rows // 2 - 1):
        srow = np.clip(2 * mp - NA_KH // 2, 0, rows - NA_PAIR_ROWS)
        qrow = 2 * mp + a
        krow = srow + i
        rs = np.clip(qrow - NA_KH // 2, 0, rows - NA_KH)
        rvalid.append((krow >= rs) & (krow < rs + NA_KH))
        ridx.append(np.clip(krow - qrow + NA_KH - 1, 0, 2 * NA_KH - 2))
    ridx = np.stack(ridx)
    rvalid = np.stack(rvalid)
    qc = np.arange(GRID_W)[None, :]
    kcol = np.arange(GRID_W)[:, None]
    cs = np.clip(qc - NA_KW // 2, 0, GRID_W - NA_KW)
    cvalid = (kcol >= cs) & (kcol < cs + NA_KW)
    cidx = np.clip(kcol - qc + NA_KW - 1, 0, 2 * NA_KW - 2)
    valid = rvalid[:, :, None, :, None] & cvalid[None, None, :, None, :]
    col_onehot = (cidx.reshape(-1)[None, :] == np.arange(2 * NA_KW - 1)[:, None])
    return (ridx.astype(np.int32), col_onehot.astype(np.float32),
            valid.reshape(5, NA_PAIR_K, NA_PAIR_Q))


_NA_RIDX, _NA_COL_ONEHOT, _NA_VALID = _na_bias_index()


def _na_bias_table(rpb):
    rows = jnp.take(rpb.astype(F32), jnp.asarray(_NA_RIDX), axis=1)
    tab = jnp.einsum("hvkqc,cn->hvkqn", rows, jnp.asarray(_NA_COL_ONEHOT),
                     precision=lax.Precision.HIGHEST)
    tab = tab.reshape(B_HEADS, 5, NA_PAIR_ROWS, 2, GRID_W, GRID_W).transpose(0, 1, 2, 4, 3, 5)
    tab = tab.reshape(B_HEADS, 5, NA_PAIR_K, NA_PAIR_Q)
    tab = jnp.where(jnp.asarray(_NA_VALID)[None], tab, NEG)
    tab = tab.reshape(2, 2, 5, NA_PAIR_K, NA_PAIR_Q).transpose(0, 2, 3, 1, 4)
    return tab.reshape(2, 5, NA_PAIR_K, 2 * NA_PAIR_Q)


def _win_kernel(sink_ref, q_ref, k_ref, v_ref, o_ref, *, safe):
    S = k_ref.shape[1]
    i = pl.program_id(1)
    q0 = i * TQ_C
    start = pl.multiple_of(jnp.clip(q0 - WINDOW, 0, S - TK_C), 128)
    ks = k_ref[0, pl.ds(start, TK_C), :]
    vs = v_ref[0, :, pl.ds(start, TK_C)]
    kpos = start + lax.broadcasted_iota(jnp.int32, (TK_C, TQ_C), 0)
    qpos = q0 + lax.broadcasted_iota(jnp.int32, (TK_C, TQ_C), 1)
    mask = jnp.where(jnp.abs(qpos - kpos) <= WINDOW, 0.0, NEG).astype(F32)
    z = jnp.zeros((HEAD_DIM, TQ_C), BF16)
    gq = C_HEADS // C_KV_HEADS
    scores = []
    for hq in range(C_HEADS):
        qh = q_ref[0, hq * HEAD_DIM:(hq + 1) * HEAD_DIM, :]
        qz = jnp.concatenate([qh, z] if hq // gq == 0 else [z, qh], axis=0)
        scores.append(_dot(ks, qz) + mask)
    probs, dens = [], []
    for hq, s in enumerate(scores):
        sk = sink_ref[hq]
        if safe:
            m = jnp.maximum(jnp.max(s, axis=0, keepdims=True), sk)
            e = jnp.exp(s - m)
            dens.append(jnp.sum(e, axis=0, keepdims=True) + jnp.exp(sk - m))
        else:
            e = jnp.exp(s)
            dens.append(jnp.sum(e, axis=0, keepdims=True) + jnp.exp(sk))
        probs.append(e.astype(BF16))
    outs = []
    for hq, (e, den) in enumerate(zip(probs, dens)):
        kv = hq // gq
        outs.append(_dot(vs[kv * HEAD_DIM:(kv + 1) * HEAD_DIM], e) / den)
    o_ref[0] = jnp.concatenate(outs, axis=0).T.astype(BF16)


def _win_attn(pt, kc, sink, safe):
    B, _, S = pt.shape
    return pl.pallas_call(
        functools.partial(_win_kernel, safe=safe),
        grid=(B, S // TQ_C),
        in_specs=[
            pl.BlockSpec(memory_space=pltpu.SMEM),
            pl.BlockSpec((1, 256, TQ_C), lambda b, i: (b, 6, i)),
            pl.BlockSpec((1, S, 128), lambda b, i: (b, 0, 6)),
            pl.BlockSpec((1, 128, S), lambda b, i: (b, 14, 0)),
        ],
        out_specs=pl.BlockSpec((1, TQ_C, 256), lambda b, i: (b, i, 0)),
        out_shape=jax.ShapeDtypeStruct((B, S, C_HEADS * HEAD_DIM), BF16),
        compiler_params=_cparams(("parallel", "arbitrary"), 48),
        name="win_attn_safe" if safe else "win_attn",
    )(sink, pt, kc, pt)


def _outproj_kernel(x_ref, ya_ref, yb_ref, yc_ref, wo_ref, g2_ref, wr_ref, br_ref, tri_ref,
                    h_ref, xs_ref, cs_ref, ri_ref, cnt_ref):
    tm = x_ref.shape[0]
    h = (x_ref[...]
         + _dot(ya_ref[...], wo_ref[0:512, :])
         + _dot(yb_ref[...], wo_ref[512:768, :])
         + _dot(yc_ref[...], wo_ref[768:1024, :]))
    h_ref[...] = h
    ms = jnp.mean(h * h, axis=-1, keepdims=True)
    n2 = h * lax.rsqrt(ms + EPS) * g2_ref[...]
    n2_hi = n2.astype(BF16)
    n2_lo = (n2 - n2_hi.astype(F32)).astype(BF16)
    both = _dot(n2_hi, wr_ref[...])
    logits = both[:, :LANES] + both[:, LANES:] + _dot(n2_lo, wr_ref[:, :LANES]) + br_ref[...]
    lane = lax.broadcasted_iota(jnp.int32, logits.shape, 1).astype(F32)
    big = float(LANES)
    ninf = -jnp.inf

    def first_argmax(vals):
        vmax = jnp.max(vals, axis=-1, keepdims=True)
        idx = jnp.min(jnp.where(vals == vmax, lane, big), axis=-1, keepdims=True)
        return vmax, idx

    gl = jnp.where(lane < N_GROUPS, logits, ninf)
    gmax, gidx = first_argmax(gl)
    gp = 1.0 / jnp.sum(jnp.exp(gl - gmax), axis=-1, keepdims=True)
    lo = N_GROUPS + EXPERTS_PER_GROUP * gidx
    el = jnp.where((lane >= lo) & (lane < lo + EXPERTS_PER_GROUP), logits, ninf)
    v1, i1 = first_argmax(el)
    el2 = jnp.where(lane == i1, ninf, el)
    v2, i2 = first_argmax(el2)
    t = jnp.exp(v2 - v1)
    w1 = 1.0 / (1.0 + t)
    w2 = t / (1.0 + t)
    comb = jnp.where(lane == i1, gp * w1, jnp.where(lane == i2, gp * w2, 0.0))

    onehot = jnp.where(lane == gidx, 1.0, 0.0)
    before = _dot(tri_ref[...], onehot.astype(BF16))
    cnt = jnp.sum(onehot, axis=0, keepdims=True)
    pos = jnp.sum(jnp.where(lane < gidx, cnt, 0.0) + jnp.where(lane == gidx, before, 0.0),
                  axis=-1, keepdims=True)
    ri = jnp.where(lane == 0.0, pos, comb)
    ri_ref[...] = ri
    hi = ri.astype(BF16)
    r1 = ri - hi.astype(F32)
    mid = r1.astype(BF16)
    low = (r1 - mid.astype(F32)).astype(BF16)
    rs = jnp.concatenate([hi, mid, low], axis=1)
    pos_t = jnp.broadcast_to(pos, (tm, LANES)).T
    want = lax.broadcasted_iota(jnp.int32, (tm, tm), 0).astype(F32)
    perm = jnp.where(pos_t[0:1, :] == want, 1.0, 0.0).astype(BF16)
    xs_ref[0, 0:tm, :] = _dot(perm, n2_hi).astype(BF16)
    c3 = _dot(perm, rs)
    cs_ref[0, 0:tm, :] = c3[:, :LANES] + c3[:, LANES:2 * LANES] + c3[:, 2 * LANES:]
    xs_ref[0, tm:, :] = jnp.zeros((MOE_PAD, D_MODEL), BF16)
    cs_ref[0, tm:, :] = jnp.zeros((MOE_PAD, LANES), F32)
    cnt_ref[0] = jnp.broadcast_to(cnt, (8, LANES)).astype(jnp.int32)


def _outproj(x2, ya, yb, yc, wo, g2, wr, br, tri):
    T = x2.shape[0]
    tm = T_MOE
    nt = T // tm
    rows = tm + MOE_PAD
    row = lambda i: (i, 0)
    fixed = lambda i: (0, 0)
    return pl.pallas_call(
        _outproj_kernel,
        grid=(nt,),
        in_specs=[
            pl.BlockSpec((tm, D_MODEL), row),
            pl.BlockSpec((tm, 512), row),
            pl.BlockSpec((tm, 256), row),
            pl.BlockSpec((tm, 256), row),
            pl.BlockSpec((D_MODEL, D_MODEL), fixed),
            pl.BlockSpec((1, D_MODEL), fixed),
            pl.BlockSpec((D_MODEL, 2 * LANES), fixed),
            pl.BlockSpec((1, LANES), fixed),
            pl.BlockSpec((tm, tm), fixed),
        ],
        out_specs=[
            pl.BlockSpec((tm, D_MODEL), row),
            pl.BlockSpec((1, rows, D_MODEL), lambda i: (i, 0, 0)),
            pl.BlockSpec((1, rows, LANES), lambda i: (i, 0, 0)),
            pl.BlockSpec((tm, LANES), row),
            pl.BlockSpec((1, 8, LANES), lambda i: (i, 0, 0)),
        ],
        out_shape=[
            jax.ShapeDtypeStruct((T, D_MODEL), F32),
            jax.ShapeDtypeStruct((nt, rows, D_MODEL), BF16),
            jax.ShapeDtypeStruct((nt, rows, LANES), F32),
            jax.ShapeDtypeStruct((T, LANES), F32),
            jax.ShapeDtypeStruct((nt, 8, LANES), jnp.int32),
        ],
        compiler_params=_cparams(("parallel",), 56),
        name="outproj",
    )(x2, ya, yb, yc, wo, g2, wr, br, tri)


def _moe_kernel(cnt_ref, xs_ref, cs_ref, wg_ref, wu_ref, wd_ref, o_ref, acc_ref):
    tiles = xs_ref.shape[0]
    nexp = wg_ref.shape[0]
    step = pl.program_id(1)
    e0 = step * nexp
    g = lax.shift_right_logical(e0, 2)

    @pl.when(step == 0)
    def _():
        acc_ref[...] = jnp.zeros_like(acc_ref)

    def ffn_pass(t, r0, size):
        r0 = pl.multiple_of(r0, 16)
        xs = xs_ref[t, pl.ds(r0, size), :]
        cs = cs_ref[t, pl.ds(r0, size), :]
        lane = lax.broadcasted_iota(jnp.int32, cs.shape, 1)
        ups = [(_dot(xs, wg_ref[j]), _dot(xs, wu_ref[j])) for j in range(nexp)]
        hidden = []
        for j, (hg, hu) in enumerate(ups):
            w = jnp.sum(jnp.where(lane == N_GROUPS + e0 + j, cs, 0.0), axis=-1, keepdims=True)
            hidden.append((hg * jax.nn.sigmoid(hg) * hu * w).astype(BF16))
        out = _dot(hidden[0], wd_ref[0])
        for j in range(1, nexp):
            out = out + _dot(hidden[j], wd_ref[j])
        acc_ref[t, pl.ds(r0, size), :] = acc_ref[t, pl.ds(r0, size), :] + out

    def tile_body(t, carry):
        i = pl.program_id(0) * tiles + t
        n = [cnt_ref[i * N_GROUPS + k] for k in range(N_GROUPS)]
        off = (jnp.where(g > 0, n[0], 0) + jnp.where(g > 1, n[1], 0) + jnp.where(g > 2, n[2], 0))
        ng = jnp.where(g == 0, n[0], jnp.where(g == 1, n[1], jnp.where(g == 2, n[2], n[3])))
        start = lax.shift_left(lax.shift_right_logical(off, 4), 4)
        need = off - start + ng
        nfull = jnp.maximum(lax.div(need - 1, MOE_SLAB), 0)
        rem_steps = lax.div(need - nfull * MOE_SLAB + (MOE_STEP - 1), MOE_STEP)

        def full_pass(s, c):
            ffn_pass(t, start + s * MOE_SLAB, MOE_SLAB)
            return c
        lax.fori_loop(0, nfull, full_pass, 0)

        for k in range(1, MOE_SLAB // MOE_STEP + 1):
            @pl.when(rem_steps == k)
            def _(k=k):
                ffn_pass(t, start + nfull * MOE_SLAB, k * MOE_STEP)
        return carry
    lax.fori_loop(0, tiles, tile_body, 0)

    @pl.when(step == pl.num_programs(1) - 1)
    def _():
        o_ref[...] = acc_ref[...].astype(BF16)


def _moe(xs, cs, cnt, wg, wu, wd):
    nt, rows, _ = xs.shape
    tpw = MOE_TILES_PER_WEIGHT if nt % MOE_TILES_PER_WEIGHT == 0 else 1
    nexp = MOE_EXPERTS_PER_STEP
    tmap = lambda s, e, c: (s, 0, 0)
    wmap = lambda s, e, c: (e, 0, 0)
    return pl.pallas_call(
        _moe_kernel,
        grid_spec=pltpu.PrefetchScalarGridSpec(
            num_scalar_prefetch=1,
            grid=(nt // tpw, N_EXPERTS // nexp),
            in_specs=[
                pl.BlockSpec((tpw, rows, D_MODEL), tmap),
                pl.BlockSpec((tpw, rows, LANES), tmap),
                pl.BlockSpec((nexp, D_MODEL, D_EXPERT), wmap),
                pl.BlockSpec((nexp, D_MODEL, D_EXPERT), wmap),
                pl.BlockSpec((nexp, D_EXPERT, D_MODEL), wmap),
            ],
            out_specs=pl.BlockSpec((tpw, rows, D_MODEL), tmap),
            scratch_shapes=[pltpu.VMEM((tpw, rows, D_MODEL), F32)],
        ),
        out_shape=jax.ShapeDtypeStruct((nt, rows, D_MODEL), BF16),
        compiler_params=_cparams(("parallel", "arbitrary"), 60),
        name="moe",
    )(cnt, xs, cs, wg, wu, wd)


def _unsort_kernel(os_ref, h_ref, ri_ref, o_ref):
    tm = h_ref.shape[0]
    want = lax.broadcasted_iota(jnp.int32, (tm, tm), 1).astype(F32)
    perm_t = jnp.where(ri_ref[:, 0:1] == want, 1.0, 0.0).astype(BF16)
    o_ref[...] = h_ref[...] + _dot(perm_t, os_ref[0, 0:tm, :])


def _unsort(os, h, ri):
    T = h.shape[0]
    tm = T_MOE
    rows = os.shape[1]
    row = lambda i: (i, 0)
    return pl.pallas_call(
        _unsort_kernel,
        grid=(T // tm,),
        in_specs=[
            pl.BlockSpec((1, rows, D_MODEL), lambda i: (i, 0, 0)),
            pl.BlockSpec((tm, D_MODEL), row),
            pl.BlockSpec((tm, LANES), row),
        ],
        out_specs=pl.BlockSpec((tm, D_MODEL), row),
        out_shape=jax.ShapeDtypeStruct((T, D_MODEL), F32),
        compiler_params=_cparams(("parallel",), 48),
        name="unsort",
    )(os, h, ri)


def _rope_tables_t(seq_len):
    pos = jnp.arange(seq_len, dtype=F32)
    inv = ROPE_THETA ** (-jnp.arange(0, HEAD_DIM, 2, dtype=F32) / HEAD_DIM)
    ang = pos[:, None] * inv[None, :]
    ang = jnp.concatenate([ang, ang], axis=-1)
    sign = jnp.where(jnp.arange(HEAD_DIM) < HEAD_DIM // 2, -1.0, 1.0).astype(F32)
    return jnp.cos(ang).T, (jnp.sin(ang) * sign[None, :]).T


def _lane_bcast(v):
    return jnp.broadcast_to(v.astype(F32)[:, None], (v.shape[0], LANES))


def _prep_layer(l, p):
    w_in = p["w_in"][l]
    cols = [w_in[:, _COLS[n][0]:_COLS[n][1]] for n in _PT_ORDER + _KT_ORDER]
    w_t = jnp.concatenate(cols, axis=1).T.astype(BF16)
    gains = jnp.stack([
        _lane_bcast(p["a_q_norm"][l] * (QK_SCALE * LOG2E)), _lane_bcast(p["a_k_norm"][l]),
        _lane_bcast(p["b_q_norm"][l] * QK_SCALE), _lane_bcast(p["b_k_norm"][l]),
        _lane_bcast(p["c_q_norm"][l] * QK_SCALE), _lane_bcast(p["c_k_norm"][l]),
    ])
    wr = jnp.zeros((D_MODEL, LANES), F32)
    wr = wr.at[:, :N_GROUPS].set(p["w_group"][l]).at[:, N_GROUPS:N_GROUPS + N_EXPERTS].set(p["w_router"][l])
    wr_hi = wr.astype(BF16)
    wr_lo = (wr - wr_hi.astype(F32)).astype(BF16)
    wr2 = jnp.concatenate([wr_hi, wr_lo], axis=1)
    br = jnp.zeros((1, LANES), F32)
    br = br.at[0, :N_GROUPS].set(p["b_group"][l]).at[0, N_GROUPS:N_GROUPS + N_EXPERTS].set(p["b_router"][l])
    def score_bound(gq, gk):
        return 1.02 * HEAD_DIM * QK_SCALE * jnp.max(jnp.abs(gq)) * jnp.max(jnp.abs(gk))

    bound = score_bound(p["a_q_norm"][l], p["a_k_norm"][l])
    bound_b = score_bound(p["b_q_norm"][l], p["b_k_norm"][l]) + jnp.max(jnp.abs(p["b_rpb"][l]))
    bound_c = jnp.maximum(score_bound(p["c_q_norm"][l], p["c_k_norm"][l]), jnp.max(jnp.abs(p["c_sink"][l])))
    return dict(
        fast_b=bound_b <= FAST_SOFTMAX_BOUND, fast_c=bound_c <= FAST_SOFTMAX_BOUND,
        g1=p["norm1_g"][l].astype(F32)[None, :], w_t=w_t, gains=gains,
        lam=jnp.stack([p["a_lambda_q1"][l], p["a_lambda_k1"][l],
                       p["a_lambda_q2"][l], p["a_lambda_k2"][l]]).astype(F32),
        subg=_lane_bcast(p["a_subln"][l]),
        na_bias=_na_bias_table(p["b_rpb"][l]),
        sink=p["c_sink"][l].astype(F32),
        wo=p["w_out"][l].astype(BF16), g2=p["norm2_g"][l].astype(F32)[None, :],
        wr=wr2, br=br,
        wg=p["w_gate"][l].astype(BF16), wu=p["w_up"][l].astype(BF16), wd=p["w_down"][l].astype(BF16),
        fast_ok=bound <= FAST_SOFTMAX_BOUND,
    )


def _layer(x, lp, lambda_init, cos_t, sin_t):
    B, S, _ = x.shape
    pt, kc = _inproj(x, lp["g1"], lp["w_t"], lp["gains"], cos_t, sin_t)
    ya = lax.cond(
        lp["fast_ok"],
        lambda a, b: _diff_attn(a, b, lp["lam"], lp["subg"], lambda_init, False),
        lambda a, b: _diff_attn(a, b, lp["lam"], lp["subg"], lambda_init, True),
        pt, kc)
    yb = lax.cond(lp["fast_b"],
                  lambda a, b: _na_attn(a, b, lp["na_bias"], False),
                  lambda a, b: _na_attn(a, b, lp["na_bias"], True), pt, kc)
    yc = lax.cond(lp["fast_c"],
                  lambda a, b: _win_attn(a, b, lp["sink"], False),
                  lambda a, b: _win_attn(a, b, lp["sink"], True), pt, kc)
    T = B * S
    tri = jnp.tril(jnp.ones((T_MOE, T_MOE), BF16), -1)
    h, xs, cs, ri, cnt = _outproj(x.reshape(T, D_MODEL), ya.reshape(T, -1), yb.reshape(T, -1),
                                  yc.reshape(T, -1), lp["wo"], lp["g2"], lp["wr"], lp["br"], tri)
    counts = cnt[:, 0, :N_GROUPS].reshape(-1)
    os = _moe(xs, cs, counts, lp["wg"], lp["wu"], lp["wd"])
    y = _unsort(os, h, ri)
    return y.reshape(B, S, D_MODEL)


def kernel(x_prompt, x_sample, norm1_g, w_in, w_out, a_q_norm, a_k_norm, a_lambda_q1, a_lambda_k1,
           a_lambda_q2, a_lambda_k2, a_subln, b_q_norm, b_k_norm, b_rpb, c_q_norm, c_k_norm, c_sink,
           norm2_g, w_group, b_group, w_router, b_router, w_gate, w_up, w_down):
    p = dict(norm1_g=norm1_g, w_in=w_in, w_out=w_out, a_q_norm=a_q_norm, a_k_norm=a_k_norm,
             a_lambda_q1=a_lambda_q1, a_lambda_k1=a_lambda_k1, a_lambda_q2=a_lambda_q2,
             a_lambda_k2=a_lambda_k2, a_subln=a_subln, b_q_norm=b_q_norm, b_k_norm=b_k_norm,
             b_rpb=b_rpb, c_q_norm=c_q_norm, c_k_norm=c_k_norm, c_sink=c_sink, norm2_g=norm2_g,
             w_group=w_group, b_group=b_group, w_router=w_router, b_router=b_router,
             w_gate=w_gate, w_up=w_up, w_down=w_down)
    layers = [_prep_layer(l, p) for l in range(DEPTH)]

    def run(x):
        cos_t, sin_t = _rope_tables_t(x.shape[1])
        for l in range(DEPTH):
            lambda_init = 0.8 - 0.6 * math.exp(-0.3 * l)
            x = _layer(x, layers[l], lambda_init, cos_t, sin_t)
        return x

    return (run(x_prompt), run(x_sample))
```

```python
import functools
import math

import numpy as np
import jax
import jax.numpy as jnp
from jax import lax
from jax.experimental import pallas as pl
from jax.experimental.pallas import tpu as pltpu

F32 = jnp.float32
BF16 = jnp.bfloat16

D_MODEL = 1024
DEPTH = 2
HEAD_DIM = 64
GRID_W = 64
EPS = 1e-6
ROPE_THETA = 10000.0
A_HEADS = 4
A_VDIM = 2 * HEAD_DIM
B_HEADS = 4
NA_KH = 8
NA_KW = 16
C_HEADS = 4
C_KV_HEADS = 2
WINDOW = 128
N_GROUPS = 4
EXPERTS_PER_GROUP = 4
N_EXPERTS = 16
D_EXPERT = 512
QK_SCALE = HEAD_DIM ** -0.5
LOG2E = math.log2(math.e)

_COLS = dict(aq=(0, 512), ak=(512, 1024), av=(1024, 1536), bq=(1536, 1792), bk=(1792, 2048),
             bv=(2048, 2304), cq=(2304, 2560), ck=(2560, 2688), cv=(2688, 2816))
_PT_ORDER = ("aq", "av", "bq", "bv", "cq", "cv")
_KT_ORDER = ("ak", "bk", "ck")
PT_ROWS = 1920
KC_COLS = 896
IN_WIDTH = 2816

LANES = 128
NEG = -1e30

TM_PROJ = 1024
TQ_A = 1024
TK_A = 2048
UNROLL_A = 2
NA_Q = 2048
NA_PAIR_Q = 2 * GRID_W
NA_PAIR_ROWS = NA_KH + 2
NA_PAIR_K = NA_PAIR_ROWS * GRID_W
TQ_C = 256
TK_C = TQ_C + 2 * WINDOW
T_MOE = 1024
MOE_SLAB = 512
MOE_STEP = 64
MOE_PAD = 80
MOE_TILES_PER_WEIGHT = 2
MOE_EXPERTS_PER_STEP = 2
FAST_SOFTMAX_BOUND = 30.0


def _cparams(sem, vmem_mb):
    return pltpu.CompilerParams(dimension_semantics=sem, vmem_limit_bytes=vmem_mb * 1024 * 1024)


def _nt_dot(a, b):
    return lax.dot_general(a, b, (((1,), (1,)), ((), ())), preferred_element_type=F32)


def _dot(a, b):
    return jnp.dot(a, b, preferred_element_type=F32)


def _inproj_kernel(x_ref, g1_ref, w_ref, gains_ref, cos_ref, sin_ref, pt_ref, kc_ref):
    tm = x_ref.shape[1]
    x = x_ref[0]
    ms = jnp.mean(x * x, axis=-1, keepdims=True)
    n = (x * lax.rsqrt(ms + EPS) * g1_ref[...]).astype(BF16)
    cos = cos_ref[...]
    sin = sin_ref[...]
    reps = tm // LANES

    def project(r0, r1):
        return _nt_dot(w_ref[r0:r1, :], n)

    def head(chunk, r0, gi, rope):
        xh = chunk[r0:r0 + HEAD_DIM, :]
        ssq = jnp.sum(xh * xh, axis=0, keepdims=True)
        gain = pltpu.repeat(gains_ref[gi], reps, axis=1)
        xg = xh * lax.rsqrt(ssq * (1.0 / HEAD_DIM) + EPS) * gain
        if rope:
            swapped = jnp.concatenate([xg[HEAD_DIM // 2:], xg[:HEAD_DIM // 2]], axis=0)
            xg = xg * cos + swapped * sin
        return xg

    def put(r0, val):
        pt_ref[0, r0:r0 + val.shape[0], :] = val.astype(BF16)

    c = project(0, 512)
    for j in range(8):
        put(j * 64, head(c, j * 64, 0, True))
    put(512, project(512, 1024))
    c = project(1024, 1536)
    for j in range(4):
        put(1024 + j * 64, head(c, j * 64, 2, False))
    put(1280, c[256:512, :])
    c = project(1536, 1920)
    for j in range(4):
        put(1536 + j * 64, head(c, j * 64, 4, True))
    put(1792, c[256:384, :])
    def put_keys(chunk, j0, specs):
        for j, (gi, rope) in enumerate(specs):
            blk = jnp.concatenate([head(chunk, j * 128, gi, rope),
                                   head(chunk, j * 128 + 64, gi, rope)], axis=0)
            kc_ref[0, :, (j0 + j) * 128:(j0 + j + 1) * 128] = blk.T.astype(BF16)

    put_keys(project(PT_ROWS, PT_ROWS + 512), 0, [(1, True)] * 4)
    put_keys(project(PT_ROWS + 512, IN_WIDTH), 4, [(3, False)] * 2 + [(5, True)])


def _inproj(x, g1, w_t, gains, cos_t, sin_t):
    B, S, _ = x.shape
    tm = TM_PROJ
    return pl.pallas_call(
        _inproj_kernel,
        grid=(B, S // tm),
        in_specs=[
            pl.BlockSpec((1, tm, D_MODEL), lambda b, i: (b, i, 0)),
            pl.BlockSpec((1, D_MODEL), lambda b, i: (0, 0)),
            pl.BlockSpec((IN_WIDTH, D_MODEL), lambda b, i: (0, 0)),
            pl.BlockSpec((6, HEAD_DIM, LANES), lambda b, i: (0, 0, 0)),
            pl.BlockSpec((HEAD_DIM, tm), lambda b, i: (0, i)),
            pl.BlockSpec((HEAD_DIM, tm), lambda b, i: (0, i)),
        ],
        out_specs=[
            pl.BlockSpec((1, PT_ROWS, tm), lambda b, i: (b, 0, i)),
            pl.BlockSpec((1, tm, KC_COLS), lambda b, i: (b, i, 0)),
        ],
        out_shape=[
            jax.ShapeDtypeStruct((B, PT_ROWS, S), BF16),
            jax.ShapeDtypeStruct((B, S, KC_COLS), BF16),
        ],
        compiler_params=_cparams(("parallel", "parallel"), 56),
        name="inproj",
    )(x, g1, w_t, gains, cos_t, sin_t)


def _diff_attn_kernel(q_ref, k_ref, v_ref, lam_ref, subg_ref, o_ref, qz_ref, acc_ref, l_ref, m_ref,
                      *, tk, safe, lambda_init):
    tq = q_ref.shape[2]
    S = k_ref.shape[1]
    q = q_ref[0]
    z = jnp.zeros((HEAD_DIM, tq), BF16)
    qz_ref[:, :tq] = jnp.concatenate([q[:HEAD_DIM], z], axis=0)
    qz_ref[:, tq:] = jnp.concatenate([z, q[HEAD_DIM:]], axis=0)
    acc_ref[...] = jnp.zeros_like(acc_ref)
    l_ref[...] = jnp.zeros_like(l_ref)
    if safe:
        m_ref[...] = jnp.full_like(m_ref, NEG)

    def body(i, carry):
        off = pl.multiple_of(i * tk, tk)
        kc = k_ref[0, pl.ds(off, tk), :]
        vc = v_ref[0, :, pl.ds(off, tk)]
        s = _dot(kc, qz_ref[...])
        if safe:
            m_old = m_ref[...]
            m_new = jnp.maximum(m_old, jnp.max(s, axis=0, keepdims=True))
            alpha = jnp.exp2(m_old - m_new)
            p = jnp.exp2(s - m_new)
            m_ref[...] = m_new
            l_ref[...] = alpha * l_ref[...] + jnp.sum(p, axis=0, keepdims=True)
            acc_ref[...] = alpha * acc_ref[...] + _dot(vc, p.astype(BF16))
        else:
            p = jnp.exp2(s)
            l_ref[...] = l_ref[...] + jnp.sum(p, axis=0, keepdims=True)
            acc_ref[...] = acc_ref[...] + _dot(vc, p.astype(BF16))
        return carry

    lax.fori_loop(0, S // tk, body, 0, unroll=UNROLL_A)

    lp = lam_ref[...]
    lam = (jnp.exp(jnp.sum(lp[0:1] * lp[1:2], axis=-1, keepdims=True))
           - jnp.exp(jnp.sum(lp[2:3] * lp[3:4], axis=-1, keepdims=True)) + lambda_init)
    on = acc_ref[...] / l_ref[...]
    o = on[:, :tq] - lam * on[:, tq:]
    ms = jnp.mean(o * o, axis=0, keepdims=True)
    gain = pltpu.repeat(subg_ref[...], tq // LANES, axis=1)
    y = o * lax.rsqrt(ms + EPS) * gain * (1.0 - lambda_init)
    o_ref[0] = y.T.astype(BF16)


def _diff_attn(pt, kc, lam_params, subg, lambda_init, safe):
    B, _, S = pt.shape
    tq, tk = TQ_A, min(TK_A, S)
    kern = functools.partial(_diff_attn_kernel, tk=tk, safe=safe, lambda_init=lambda_init)
    return pl.pallas_call(
        kern,
        grid=(B, A_HEADS, S // tq),
        in_specs=[
            pl.BlockSpec((1, 128, tq), lambda b, h, i: (b, h, i)),
            pl.BlockSpec((1, S, 128), lambda b, h, i: (b, 0, h)),
            pl.BlockSpec((1, 128, S), lambda b, h, i: (b, 4 + h, 0)),
            pl.BlockSpec((4, HEAD_DIM), lambda b, h, i: (0, 0)),
            pl.BlockSpec((A_VDIM, LANES), lambda b, h, i: (0, 0)),
        ],
        out_specs=pl.BlockSpec((1, tq, 128), lambda b, h, i: (b, i, h)),
        out_shape=jax.ShapeDtypeStruct((B, S, A_HEADS * A_VDIM), BF16),
        scratch_shapes=[
            pltpu.VMEM((128, 2 * tq), BF16),
            pltpu.VMEM((A_VDIM, 2 * tq), F32),
            pltpu.VMEM((1, 2 * tq), F32),
            pltpu.VMEM((1, 2 * tq), F32),
        ],
        compiler_params=_cparams(("parallel", "parallel", "arbitrary"), 48),
        name="diff_attn_safe" if safe else "diff_attn",
    )(pt, kc, pt, lam_params, subg)


def _na_kernel(q_ref, k_ref, v_ref, bias_ref, o_ref, *, safe):
    S = k_ref.shape[1]
    rows = S // GRID_W
    r = pl.program_id(2)
    q = q_ref[0]
    z = jnp.zeros((HEAD_DIM, NA_PAIR_Q), BF16)
    npairs = q_ref.shape[2] // NA_PAIR_Q
    scores, vals = [], []
    for u in range(npairs):
        mp = r * npairs + u
        srow = jnp.clip(2 * mp - NA_KH // 2, 0, rows - NA_PAIR_ROWS)
        start = pl.multiple_of(srow * GRID_W, 128)
        var = jnp.where(mp == 0, 0, jnp.where(mp == 1, 1, jnp.where(
            mp == rows // 2 - 2, 3, jnp.where(mp == rows // 2 - 1, 4, 2))))
        ks = k_ref[0, pl.ds(start, NA_PAIR_K), :]
        vals.append(v_ref[0, :, pl.ds(start, NA_PAIR_K)])
        qp = q[:, u * NA_PAIR_Q:(u + 1) * NA_PAIR_Q]
        qz = jnp.concatenate([jnp.concatenate([qp[:HEAD_DIM], z], axis=0),
                              jnp.concatenate([z, qp[HEAD_DIM:]], axis=0)], axis=1)
        scores.append(_dot(ks, qz) + bias_ref[0, var])
    probs, sums = [], []
    for s in scores:
        p = jnp.exp(s - jnp.max(s, axis=0, keepdims=True)) if safe else jnp.exp(s)
        sums.append(jnp.sum(p, axis=0, keepdims=True))
        probs.append(p.astype(BF16))
    outs = []
    for vs, p, l in zip(vals, probs, sums):
        o = _dot(vs, p) / l
        outs.append(jnp.concatenate([o[:HEAD_DIM, :NA_PAIR_Q], o[HEAD_DIM:, NA_PAIR_Q:]], axis=0))
    o_ref[0] = jnp.concatenate(outs, axis=1).T.astype(BF16)


def _na_attn(pt, kc, bias, safe):
    B, _, S = pt.shape
    nq = min(NA_Q, S)
    nr = S // nq
    return pl.pallas_call(
        functools.partial(_na_kernel, safe=safe),
        grid=(B, 2, nr),
        in_specs=[
            pl.BlockSpec((1, 128, nq), lambda b, p, r: (b, 8 + p, r)),
            pl.BlockSpec((1, S, 128), lambda b, p, r: (b, 0, 4 + p)),
            pl.BlockSpec((1, 128, S), lambda b, p, r: (b, 10 + p, 0)),
            pl.BlockSpec((1, 5, NA_PAIR_K, 2 * NA_PAIR_Q), lambda b, p, r: (p, 0, 0, 0)),
        ],
        out_specs=pl.BlockSpec((1, nq, 128), lambda b, p, r: (b, r, p)),
        out_shape=jax.ShapeDtypeStruct((B, S, B_HEADS * HEAD_DIM), BF16),
        compiler_params=_cparams(("parallel", "parallel", "arbitrary"), 48),
        name="na_attn_safe" if safe else "na_attn",
    )(pt, kc, pt, bias)


def _na_bias_index():
    rows = 32
    a = np.arange(2)[None, :]
    i = np.arange(NA_PAIR_ROWS)[:, None]
    ridx, rvalid = [], []
    for mp in (0, 1, 5, rows // 2 - 2, rows // 2 - 1):
        srow = np.clip(2 * mp - NA_KH // 2, 0, rows - NA_PAIR_ROWS)
        qrow = 2 * mp + a
        krow = srow + i
        rs = np.clip(qrow - NA_KH // 2, 0, rows - NA_KH)
        rvalid.append((krow >= rs) & (krow < rs + NA_KH))
        ridx.append(np.clip(krow - qrow + NA_KH - 1, 0, 2 * NA_KH - 2))
    ridx = np.stack(ridx)
    rvalid = np.stack(rvalid)
    qc = np.arange(GRID_W)[None, :]
    kcol = np.arange(GRID_W)[:, None]
    cs = np.clip(qc - NA_KW // 2, 0, GRID_W - NA_KW)
    cvalid = (kcol >= cs) & (kcol < cs + NA_KW)
    cidx = np.clip(kcol - qc + NA_KW - 1, 0, 2 * NA_KW - 2)
    valid = rvalid[:, :, None, :, None] & cvalid[None, None, :, None, :]
    col_onehot = (cidx.reshape(-1)[None, :] == np.arange(2 * NA_KW - 1)[:, None])
    return (ridx.astype(np.int32), col_onehot.astype(np.float32),
            valid.reshape(5, NA_PAIR_K, NA_PAIR_Q))


_NA_RIDX, _NA_COL_ONEHOT, _NA_VALID = _na_bias_index()


def _na_bias_table(rpb):
    rows = jnp.take(rpb.astype(F32), jnp.asarray(_NA_RIDX), axis=1)
    tab = jnp.einsum("hvkqc,cn->hvkqn", rows, jnp.asarray(_NA_COL_ONEHOT),
                     precision=lax.Precision.HIGHEST)
    tab = tab.reshape(B_HEADS, 5, NA_PAIR_ROWS, 2, GRID_W, GRID_W).transpose(0, 1, 2, 4, 3, 5)
    tab = tab.reshape(B_HEADS, 5, NA_PAIR_K, NA_PAIR_Q)
    tab = jnp.where(jnp.asarray(_NA_VALID)[None], tab, NEG)
    tab = tab.reshape(2, 2, 5, NA_PAIR_K, NA_PAIR_Q).transpose(0, 2, 3, 1, 4)
    return tab.reshape(2, 5, NA_PAIR_K, 2 * NA_PAIR_Q)


def _win_kernel(sink_ref, q_ref, k_ref, v_ref, o_ref, *, safe):
    S = k_ref.shape[1]
    i = pl.program_id(1)
    q0 = i * TQ_C
    start = pl.multiple_of(jnp.clip(q0 - WINDOW, 0, S - TK_C), 128)
    ks = k_ref[0, pl.ds(start, TK_C), :]
    vs = v_ref[0, :, pl.ds(start, TK_C)]
    kpos = start + lax.broadcasted_iota(jnp.int32, (TK_C, TQ_C), 0)
    qpos = q0 + lax.broadcasted_iota(jnp.int32, (TK_C, TQ_C), 1)
    mask = jnp.where(jnp.abs(qpos - kpos) <= WINDOW, 0.0, NEG).astype(F32)
    z = jnp.zeros((HEAD_DIM, TQ_C), BF16)
    gq = C_HEADS // C_KV_HEADS
    scores = []
    for hq in range(C_HEADS):
        qh = q_ref[0, hq * HEAD_DIM:(hq + 1) * HEAD_DIM, :]
        qz = jnp.concatenate([qh, z] if hq // gq == 0 else [z, qh], axis=0)
        scores.append(_dot(ks, qz) + mask)
    probs, dens = [], []
    for hq, s in enumerate(scores):
        sk = sink_ref[hq]
        if safe:
            m = jnp.maximum(jnp.max(s, axis=0, keepdims=True), sk)
            e = jnp.exp(s - m)
            dens.append(jnp.sum(e, axis=0, keepdims=True) + jnp.exp(sk - m))
        else:
            e = jnp.exp(s)
            dens.append(jnp.sum(e, axis=0, keepdims=True) + jnp.exp(sk))
        probs.append(e.astype(BF16))
    outs = []
    for hq, (e, den) in enumerate(zip(probs, dens)):
        kv = hq // gq
        outs.append(_dot(vs[kv * HEAD_DIM:(kv + 1) * HEAD_DIM], e) / den)
    o_ref[0] = jnp.concatenate(outs, axis=0).T.astype(BF16)


def _win_attn(pt, kc, sink, safe):
    B, _, S = pt.shape
    return pl.pallas_call(
        functools.partial(_win_kernel, safe=safe),
        grid=(B, S // TQ_C),
        in_specs=[
            pl.BlockSpec(memory_space=pltpu.SMEM),
            pl.BlockSpec((1, 256, TQ_C), lambda b, i: (b, 6, i)),
            pl.BlockSpec((1, S, 128), lambda b, i: (b, 0, 6)),
            pl.BlockSpec((1, 128, S), lambda b, i: (b, 14, 0)),
        ],
        out_specs=pl.BlockSpec((1, TQ_C, 256), lambda b, i: (b, i, 0)),
        out_shape=jax.ShapeDtypeStruct((B, S, C_HEADS * HEAD_DIM), BF16),
        compiler_params=_cparams(("parallel", "arbitrary"), 48),
        name="win_attn_safe" if safe else "win_attn",
    )(sink, pt, kc, pt)


def _outproj_kernel(x_ref, ya_ref, yb_ref, yc_ref, wo_ref, g2_ref, wr_ref, br_ref, tri_ref,
                    h_ref, xs_ref, cs_ref, ri_ref, cnt_ref):
    tm = x_ref.shape[0]
    h = (x_ref[...]
         + _dot(ya_ref[...], wo_ref[0:512, :])
         + _dot(yb_ref[...], wo_ref[512:768, :])
         + _dot(yc_ref[...], wo_ref[768:1024, :]))
    h_ref[...] = h
    ms = jnp.mean(h * h, axis=-1, keepdims=True)
    n2 = h * lax.rsqrt(ms + EPS) * g2_ref[...]
    n2_hi = n2.astype(BF16)
    n2_lo = (n2 - n2_hi.astype(F32)).astype(BF16)
    both = _dot(n2_hi, wr_ref[...])
    logits = both[:, :LANES] + both[:, LANES:] + _dot(n2_lo, wr_ref[:, :LANES]) + br_ref[...]
    lane = lax.broadcasted_iota(jnp.int32, logits.shape, 1).astype(F32)
    big = float(LANES)
    ninf = -jnp.inf

    def first_argmax(vals):
        vmax = jnp.max(vals, axis=-1, keepdims=True)
        idx = jnp.min(jnp.where(vals == vmax, lane, big), axis=-1, keepdims=True)
        return vmax, idx

    gl = jnp.where(lane < N_GROUPS, logits, ninf)
    gmax, gidx = first_argmax(gl)
    gp = 1.0 / jnp.sum(jnp.exp(gl - gmax), axis=-1, keepdims=True)
    lo = N_GROUPS + EXPERTS_PER_GROUP * gidx
    el = jnp.where((lane >= lo) & (lane < lo + EXPERTS_PER_GROUP), logits, ninf)
    v1, i1 = first_argmax(el)
    el2 = jnp.where(lane == i1, ninf, el)
    v2, i2 = first_argmax(el2)
    t = jnp.exp(v2 - v1)
    w1 = 1.0 / (1.0 + t)
    w2 = t / (1.0 + t)
    comb = jnp.where(lane == i1, gp * w1, jnp.where(lane == i2, gp * w2, 0.0))

    onehot = jnp.where(lane == gidx, 1.0, 0.0)
    before = _dot(tri_ref[...], onehot.astype(BF16))
    cnt = jnp.sum(onehot, axis=0, keepdims=True)
    pos = jnp.sum(jnp.where(lane < gidx, cnt, 0.0) + jnp.where(lane == gidx, before, 0.0),
                  axis=-1, keepdims=True)
    ri = jnp.where(lane == 0.0, pos, comb)
    ri_ref[...] = ri
    hi = ri.astype(BF16)
    r1 = ri - hi.astype(F32)
    mid = r1.astype(BF16)
    low = (r1 - mid.astype(F32)).astype(BF16)
    rs = jnp.concatenate([hi, mid, low], axis=1)
    pos_t = jnp.broadcast_to(pos, (tm, LANES)).T
    want = lax.broadcasted_iota(jnp.int32, (tm, tm), 0).astype(F32)
    perm = jnp.where(pos_t[0:1, :] == want, 1.0, 0.0).astype(BF16)
    xs_ref[0, 0:tm, :] = _dot(perm, n2_hi).astype(BF16)
    c3 = _dot(perm, rs)
    cs_ref[0, 0:tm, :] = c3[:, :LANES] + c3[:, LANES:2 * LANES] + c3[:, 2 * LANES:]
    xs_ref[0, tm:, :] = jnp.zeros((MOE_PAD, D_MODEL), BF16)
    cs_ref[0, tm:, :] = jnp.zeros((MOE_PAD, LANES), F32)
    cnt_ref[0] = jnp.broadcast_to(cnt, (8, LANES)).astype(jnp.int32)


def _outproj(x2, ya, yb, yc, wo, g2, wr, br, tri):
    T = x2.shape[0]
    tm = T_MOE
    nt = T // tm
    rows = tm + MOE_PAD
    row = lambda i: (i, 0)
    fixed = lambda i: (0, 0)
    return pl.pallas_call(
        _outproj_kernel,
        grid=(nt,),
        in_specs=[
            pl.BlockSpec((tm, D_MODEL), row),
            pl.BlockSpec((tm, 512), row),
            pl.BlockSpec((tm, 256), row),
            pl.BlockSpec((tm, 256), row),
            pl.BlockSpec((D_MODEL, D_MODEL), fixed),
            pl.BlockSpec((1, D_MODEL), fixed),
            pl.BlockSpec((D_MODEL, 2 * LANES), fixed),
            pl.BlockSpec((1, LANES), fixed),
            pl.BlockSpec((tm, tm), fixed),
        ],
        out_specs=[
            pl.BlockSpec((tm, D_MODEL), row),
            pl.BlockSpec((1, rows, D_MODEL), lambda i: (i, 0, 0)),
            pl.BlockSpec((1, rows, LANES), lambda i: (i, 0, 0)),
            pl.BlockSpec((tm, LANES), row),
            pl.BlockSpec((1, 8, LANES), lambda i: (i, 0, 0)),
        ],
        out_shape=[
            jax.ShapeDtypeStruct((T, D_MODEL), F32),
            jax.ShapeDtypeStruct((nt, rows, D_MODEL), BF16),
            jax.ShapeDtypeStruct((nt, rows, LANES), F32),
            jax.ShapeDtypeStruct((T, LANES), F32),
            jax.ShapeDtypeStruct((nt, 8, LANES), jnp.int32),
        ],
        compiler_params=_cparams(("parallel",), 56),
        name="outproj",
    )(x2, ya, yb, yc, wo, g2, wr, br, tri)


def _moe_kernel(cnt_ref, xs_ref, cs_ref, wg_ref, wu_ref, wd_ref, o_ref, acc_ref):
    tiles = xs_ref.shape[0]
    nexp = wg_ref.shape[0]
    step = pl.program_id(1)
    e0 = step * nexp
    g = lax.shift_right_logical(e0, 2)

    @pl.when(step == 0)
    def _():
        acc_ref[...] = jnp.zeros_like(acc_ref)

    def ffn_pass(t, r0, size):
        r0 = pl.multiple_of(r0, 16)
        xs = xs_ref[t, pl.ds(r0, size), :]
        cs = cs_ref[t, pl.ds(r0, size), :]
        lane = lax.broadcasted_iota(jnp.int32, cs.shape, 1)
        ups = [(_dot(xs, wg_ref[j]), _dot(xs, wu_ref[j])) for j in range(nexp)]
        hidden = []
        for j, (hg, hu) in enumerate(ups):
            w = jnp.sum(jnp.where(lane == N_GROUPS + e0 + j, cs, 0.0), axis=-1, keepdims=True)
            hidden.append((hg * jax.nn.sigmoid(hg) * hu * w).astype(BF16))
        out = _dot(hidden[0], wd_ref[0])
        for j in range(1, nexp):
            out = out + _dot(hidden[j], wd_ref[j])
        acc_ref[t, pl.ds(r0, size), :] = acc_ref[t, pl.ds(r0, size), :] + out

    def tile_body(t, carry):
        i = pl.program_id(0) * tiles + t
        n = [cnt_ref[i * N_GROUPS + k] for k in range(N_GROUPS)]
        off = (jnp.where(g > 0, n[0], 0) + jnp.where(g > 1, n[1], 0) + jnp.where(g > 2, n[2], 0))
        ng = jnp.where(g == 0, n[0], jnp.where(g == 1, n[1], jnp.where(g == 2, n[2], n[3])))
        start = lax.shift_left(lax.shift_right_logical(off, 4), 4)
        need = off - start + ng
        nfull = jnp.maximum(lax.div(need - 1, MOE_SLAB), 0)
        rem_steps = lax.div(need - nfull * MOE_SLAB + (MOE_STEP - 1), MOE_STEP)

        def full_pass(s, c):
            ffn_pass(t, start + s * MOE_SLAB, MOE_SLAB)
            return c
        lax.fori_loop(0, nfull, full_pass, 0)

        for k in range(1, MOE_SLAB // MOE_STEP + 1):
            @pl.when(rem_steps == k)
            def _(k=k):
                ffn_pass(t, start + nfull * MOE_SLAB, k * MOE_STEP)
        return carry
    lax.fori_loop(0, tiles, tile_body, 0)

    @pl.when(step == pl.num_programs(1) - 1)
    def _():
        o_ref[...] = acc_ref[...].astype(BF16)


def _moe(xs, cs, cnt, wg, wu, wd):
    nt, rows, _ = xs.shape
    tpw = MOE_TILES_PER_WEIGHT if nt % MOE_TILES_PER_WEIGHT == 0 else 1
    nexp = MOE_EXPERTS_PER_STEP
    tmap = lambda s, e, c: (s, 0, 0)
    wmap = lambda s, e, c: (e, 0, 0)
    return pl.pallas_call(
        _moe_kernel,
        grid_spec=pltpu.PrefetchScalarGridSpec(
            num_scalar_prefetch=1,
            grid=(nt // tpw, N_EXPERTS // nexp),
            in_specs=[
                pl.BlockSpec((tpw, rows, D_MODEL), tmap),
                pl.BlockSpec((tpw, rows, LANES), tmap),
                pl.BlockSpec((nexp, D_MODEL, D_EXPERT), wmap),
                pl.BlockSpec((nexp, D_MODEL, D_EXPERT), wmap),
                pl.BlockSpec((nexp, D_EXPERT, D_MODEL), wmap),
            ],
            out_specs=pl.BlockSpec((tpw, rows, D_MODEL), tmap),
            scratch_shapes=[pltpu.VMEM((tpw, rows, D_MODEL), F32)],
        ),
        out_shape=jax.ShapeDtypeStruct((nt, rows, D_MODEL), BF16),
        compiler_params=_cparams(("parallel", "arbitrary"), 60),
        name="moe",
    )(cnt, xs, cs, wg, wu, wd)


def _unsort_kernel(os_ref, h_ref, ri_ref, o_ref):
    tm = h_ref.shape[0]
    want = lax.broadcasted_iota(jnp.int32, (tm, tm), 1).astype(F32)
    perm_t = jnp.where(ri_ref[:, 0:1] == want, 1.0, 0.0).astype(BF16)
    o_ref[...] = h_ref[...] + _dot(perm_t, os_ref[0, 0:tm, :])


def _unsort(os, h, ri):
    T = h.shape[0]
    tm = T_MOE
    rows = os.shape[1]
    row = lambda i: (i, 0)
    return pl.pallas_call(
        _unsort_kernel,
        grid=(T // tm,),
        in_specs=[
            pl.BlockSpec((1, rows, D_MODEL), lambda i: (i, 0, 0)),
            pl.BlockSpec((tm, D_MODEL), row),
            pl.BlockSpec((tm, LANES), row),
        ],
        out_specs=pl.BlockSpec((tm, D_MODEL), row),
        out_shape=jax.ShapeDtypeStruct((T, D_MODEL), F32),
        compiler_params=_cparams(("parallel",), 48),
        name="unsort",
    )(os, h, ri)


def _rope_tables_t(seq_len):
    pos = jnp.arange(seq_len, dtype=F32)
    inv = ROPE_THETA ** (-jnp.arange(0, HEAD_DIM, 2, dtype=F32) / HEAD_DIM)
    ang = pos[:, None] * inv[None, :]
    ang = jnp.concatenate([ang, ang], axis=-1)
    sign = jnp.where(jnp.arange(HEAD_DIM) < HEAD_DIM // 2, -1.0, 1.0).astype(F32)
    return jnp.cos(ang).T, (jnp.sin(ang) * sign[None, :]).T


def _lane_bcast(v):
    return jnp.broadcast_to(v.astype(F32)[:, None], (v.shape[0], LANES))


def _prep_layer(l, p):
    w_in = p["w_in"][l]
    cols = [w_in[:, _COLS[n][0]:_COLS[n][1]] for n in _PT_ORDER + _KT_ORDER]
    w_t = jnp.concatenate(cols, axis=1).T.astype(BF16)
    gains = jnp.stack([
        _lane_bcast(p["a_q_norm"][l] * (QK_SCALE * LOG2E)), _lane_bcast(p["a_k_norm"][l]),
        _lane_bcast(p["b_q_norm"][l] * QK_SCALE), _lane_bcast(p["b_k_norm"][l]),
        _lane_bcast(p["c_q_norm"][l] * QK_SCALE), _lane_bcast(p["c_k_norm"][l]),
    ])
    wr = jnp.zeros((D_MODEL, LANES), F32)
    wr = wr.at[:, :N_GROUPS].set(p["w_group"][l]).at[:, N_GROUPS:N_GROUPS + N_EXPERTS].set(p["w_router"][l])
    wr_hi = wr.astype(BF16)
    wr_lo = (wr - wr_hi.astype(F32)).astype(BF16)
    wr2 = jnp.concatenate([wr_hi, wr_lo], axis=1)
    br = jnp.zeros((1, LANES), F32)
    br = br.at[0, :N_GROUPS].set(p["b_group"][l]).at[0, N_GROUPS:N_GROUPS + N_EXPERTS].set(p["b_router"][l])
    def score_bound(gq, gk):
        return 1.02 * HEAD_DIM * QK_SCALE * jnp.max(jnp.abs(gq)) * jnp.max(jnp.abs(gk))

    bound = score_bound(p["a_q_norm"][l], p["a_k_norm"][l])
    bound_b = score_bound(p["b_q_norm"][l], p["b_k_norm"][l]) + jnp.max(jnp.abs(p["b_rpb"][l]))
    bound_c = jnp.maximum(score_bound(p["c_q_norm"][l], p["c_k_norm"][l]), jnp.max(jnp.abs(p["c_sink"][l])))
    return dict(
        fast_b=bound_b <= FAST_SOFTMAX_BOUND, fast_c=bound_c <= FAST_SOFTMAX_BOUND,
        g1=p["norm1_g"][l].astype(F32)[None, :], w_t=w_t, gains=gains,
        lam=jnp.stack([p["a_lambda_q1"][l], p["a_lambda_k1"][l],
                       p["a_lambda_q2"][l], p["a_lambda_k2"][l]]).astype(F32),
        subg=_lane_bcast(p["a_subln"][l]),
        na_bias=_na_bias_table(p["b_rpb"][l]),
        sink=p["c_sink"][l].astype(F32),
        wo=p["w_out"][l].astype(BF16), g2=p["norm2_g"][l].astype(F32)[None, :],
        wr=wr2, br=br,
        wg=p["w_gate"][l].astype(BF16), wu=p["w_up"][l].astype(BF16), wd=p["w_down"][l].astype(BF16),
        fast_ok=bound <= FAST_SOFTMAX_BOUND,
    )


def _layer(x, lp, lambda_init, cos_t, sin_t):
    B, S, _ = x.shape
    pt, kc = _inproj(x, lp["g1"], lp["w_t"], lp["gains"], cos_t, sin_t)
    ya = lax.cond(
        lp["fast_ok"],
        lambda a, b: _diff_attn(a, b, lp["lam"], lp["subg"], lambda_init, False),
        lambda a, b: _diff_attn(a, b, lp["lam"], lp["subg"], lambda_init, True),
        pt, kc)
    yb = lax.cond(lp["fast_b"],
                  lambda a, b: _na_attn(a, b, lp["na_bias"], False),
                  lambda a, b: _na_attn(a, b, lp["na_bias"], True), pt, kc)
    yc = lax.cond(lp["fast_c"],
                  lambda a, b: _win_attn(a, b, lp["sink"], False),
                  lambda a, b: _win_attn(a, b, lp["sink"], True), pt, kc)
    T = B * S
    tri = jnp.tril(jnp.ones((T_MOE, T_MOE), BF16), -1)
    h, xs, cs, ri, cnt = _outproj(x.reshape(T, D_MODEL), ya.reshape(T, -1), yb.reshape(T, -1),
                                  yc.reshape(T, -1), lp["wo"], lp["g2"], lp["wr"], lp["br"], tri)
    counts = cnt[:, 0, :N_GROUPS].reshape(-1)
    os = _moe(xs, cs, counts, lp["wg"], lp["wu"], lp["wd"])
    y = _unsort(os, h, ri)
    return y.reshape(B, S, D_MODEL)


def kernel(x_prompt, x_sample, norm1_g, w_in, w_out, a_q_norm, a_k_norm, a_lambda_q1, a_lambda_k1,
           a_lambda_q2, a_lambda_k2, a_subln, b_q_norm, b_k_norm, b_rpb, c_q_norm, c_k_norm, c_sink,
           norm2_g, w_group, b_group, w_router, b_router, w_gate, w_up, w_down):
    p = dict(norm1_g=norm1_g, w_in=w_in, w_out=w_out, a_q_norm=a_q_norm, a_k_norm=a_k_norm,
             a_lambda_q1=a_lambda_q1, a_lambda_k1=a_lambda_k1, a_lambda_q2=a_lambda_q2,
             a_lambda_k2=a_lambda_k2, a_subln=a_subln, b_q_norm=b_q_norm, b_k_norm=b_k_norm,
             b_rpb=b_rpb, c_q_norm=c_q_norm, c_k_norm=c_k_norm, c_sink=c_sink, norm2_g=norm2_g,
             w_group=w_group, b_group=b_group, w_router=w_router, b_router=b_router,
             w_gate=w_gate, w_up=w_up, w_down=w_down)
    layers = [_prep_layer(l, p) for l in range(DEPTH)]

    def run(x):
        cos_t, sin_t = _rope_tables_t(x.shape[1])
        for l in range(DEPTH):
            lambda_init = 0.8 - 0.6 * math.exp(-0.3 * l)
            x = _layer(x, layers[l], lambda_init, cos_t, sin_t)
        return x

    return (run(x_prompt), run(x_sample))
```

```python
import functools
import math

import numpy as np
import jax
import jax.numpy as jnp
from jax import lax
from jax.experimental import pallas as pl
from jax.experimental.pallas import tpu as pltpu

F32 = jnp.float32
BF16 = jnp.bfloat16

D_MODEL = 1024
DEPTH = 2
HEAD_DIM = 64
GRID_W = 64
EPS = 1e-6
ROPE_THETA = 10000.0
A_HEADS = 4
A_VDIM = 2 * HEAD_DIM
B_HEADS = 4
NA_KH = 8
NA_KW = 16
C_HEADS = 4
C_KV_HEADS = 2
WINDOW = 128
N_GROUPS = 4
EXPERTS_PER_GROUP = 4
N_EXPERTS = 16
D_EXPERT = 512
QK_SCALE = HEAD_DIM ** -0.5
LOG2E = math.log2(math.e)

_COLS = dict(aq=(0, 512), ak=(512, 1024), av=(1024, 1536), bq=(1536, 1792), bk=(1792, 2048),
             bv=(2048, 2304), cq=(2304, 2560), ck=(2560, 2688), cv=(2688, 2816))
_PT_ORDER = ("aq", "av", "bq", "bv", "cq", "cv")
_KT_ORDER = ("ak", "bk", "ck")
PT_ROWS = 1920
KC_COLS = 896
IN_WIDTH = 2816

LANES = 128
NEG = -1e30

TM_PROJ = 1024
TQ_A = 1024
TK_A = 2048
UNROLL_A = 2
NA_Q = 2048
NA_PAIR_Q = 2 * GRID_W
NA_PAIR_ROWS = NA_KH + 2
NA_PAIR_K = NA_PAIR_ROWS * GRID_W
TQ_C = 256
WIN_Q = 1024
TK_C = TQ_C + 2 * WINDOW
T_MOE = 1024
MOE_SLAB = 768
MOE_STEP = 64
MOE_PAD = 80
MOE_TILES_PER_WEIGHT = 2
MOE_EXPERTS_PER_STEP = 2
FAST_SOFTMAX_BOUND = 30.0


def _cparams(sem, vmem_mb):
    return pltpu.CompilerParams(dimension_semantics=sem, vmem_limit_bytes=vmem_mb * 1024 * 1024)


def _nt_dot(a, b):
    return lax.dot_general(a, b, (((1,), (1,)), ((), ())), preferred_element_type=F32)


def _dot(a, b):
    return jnp.dot(a, b, preferred_element_type=F32)


def _inproj_kernel(x_ref, g1_ref, w_ref, gains_ref, cos_ref, sin_ref, pt_ref, kc_ref):
    tm = x_ref.shape[1]
    x = x_ref[0]
    ms = jnp.mean(x * x, axis=-1, keepdims=True)
    n = (x * lax.rsqrt(ms + EPS) * g1_ref[...]).astype(BF16)
    cos = cos_ref[...]
    sin = sin_ref[...]
    reps = tm // LANES

    def project(r0, r1):
        return _nt_dot(w_ref[r0:r1, :], n)

    def head(chunk, r0, gi, rope):
        xh = chunk[r0:r0 + HEAD_DIM, :]
        ssq = jnp.sum(xh * xh, axis=0, keepdims=True)
        gain = pltpu.repeat(gains_ref[gi], reps, axis=1)
        xg = xh * lax.rsqrt(ssq * (1.0 / HEAD_DIM) + EPS) * gain
        if rope:
            swapped = jnp.concatenate([xg[HEAD_DIM // 2:], xg[:HEAD_DIM // 2]], axis=0)
            xg = xg * cos + swapped * sin
        return xg

    def put(r0, val):
        pt_ref[0, r0:r0 + val.shape[0], :] = val.astype(BF16)

    c = project(0, 512)
    for j in range(8):
        put(j * 64, head(c, j * 64, 0, True))
    put(512, project(512, 1024))
    c = project(1024, 1536)
    for j in range(4):
        put(1024 + j * 64, head(c, j * 64, 2, False))
    put(1280, c[256:512, :])
    c = project(1536, 1920)
    for j in range(4):
        put(1536 + j * 64, head(c, j * 64, 4, True))
    put(1792, c[256:384, :])
    def put_keys(chunk, j0, specs):
        for j, (gi, rope) in enumerate(specs):
            blk = jnp.concatenate([head(chunk, j * 128, gi, rope),
                                   head(chunk, j * 128 + 64, gi, rope)], axis=0)
            kc_ref[0, :, (j0 + j) * 128:(j0 + j + 1) * 128] = blk.T.astype(BF16)

    put_keys(project(PT_ROWS, PT_ROWS + 512), 0, [(1, True)] * 4)
    put_keys(project(PT_ROWS + 512, IN_WIDTH), 4, [(3, False)] * 2 + [(5, True)])


def _inproj(x, g1, w_t, gains, cos_t, sin_t):
    B, S, _ = x.shape
    tm = TM_PROJ
    return pl.pallas_call(
        _inproj_kernel,
        grid=(B, S // tm),
        in_specs=[
            pl.BlockSpec((1, tm, D_MODEL), lambda b, i: (b, i, 0)),
            pl.BlockSpec((1, D_MODEL), lambda b, i: (0, 0)),
            pl.BlockSpec((IN_WIDTH, D_MODEL), lambda b, i: (0, 0)),
            pl.BlockSpec((6, HEAD_DIM, LANES), lambda b, i: (0, 0, 0)),
            pl.BlockSpec((HEAD_DIM, tm), lambda b, i: (0, i)),
            pl.BlockSpec((HEAD_DIM, tm), lambda b, i: (0, i)),
        ],
        out_specs=[
            pl.BlockSpec((1, PT_ROWS, tm), lambda b, i: (b, 0, i)),
            pl.BlockSpec((1, tm, KC_COLS), lambda b, i: (b, i, 0)),
        ],
        out_shape=[
            jax.ShapeDtypeStruct((B, PT_ROWS, S), BF16),
            jax.ShapeDtypeStruct((B, S, KC_COLS), BF16),
        ],
        compiler_params=_cparams(("parallel", "parallel"), 56),
        name="inproj",
    )(x, g1, w_t, gains, cos_t, sin_t)


def _diff_attn_kernel(q_ref, k_ref, v_ref, lam_ref, subg_ref, o_ref, qz_ref, acc_ref, l_ref, m_ref,
                      *, tk, safe, lambda_init):
    tq = q_ref.shape[2]
    S = k_ref.shape[1]
    q = q_ref[0]
    z = jnp.zeros((HEAD_DIM, tq), BF16)
    qz_ref[:, :tq] = jnp.concatenate([q[:HEAD_DIM], z], axis=0)
    qz_ref[:, tq:] = jnp.concatenate([z, q[HEAD_DIM:]], axis=0)
    acc_ref[...] = jnp.zeros_like(acc_ref)
    l_ref[...] = jnp.zeros_like(l_ref)
    if safe:
        m_ref[...] = jnp.full_like(m_ref, NEG)

    def body(i, carry):
        off = pl.multiple_of(i * tk, tk)
        kc = k_ref[0, pl.ds(off, tk), :]
        vc = v_ref[0, :, pl.ds(off, tk)]
        s = _dot(kc, qz_ref[...])
        if safe:
            m_old = m_ref[...]
            m_new = jnp.maximum(m_old, jnp.max(s, axis=0, keepdims=True))
            alpha = jnp.exp2(m_old - m_new)
            p = jnp.exp2(s - m_new)
            m_ref[...] = m_new
            l_ref[...] = alpha * l_ref[...] + jnp.sum(p, axis=0, keepdims=True)
            acc_ref[...] = alpha * acc_ref[...] + _dot(vc, p.astype(BF16))
        else:
            p = jnp.exp2(s)
            l_ref[...] = l_ref[...] + jnp.sum(p, axis=0, keepdims=True)
            acc_ref[...] = acc_ref[...] + _dot(vc, p.astype(BF16))
        return carry

    lax.fori_loop(0, S // tk, body, 0, unroll=UNROLL_A)

    lp = lam_ref[...]
    lam = (jnp.exp(jnp.sum(lp[0:1] * lp[1:2], axis=-1, keepdims=True))
           - jnp.exp(jnp.sum(lp[2:3] * lp[3:4], axis=-1, keepdims=True)) + lambda_init)
    on = acc_ref[...] / l_ref[...]
    o = on[:, :tq] - lam * on[:, tq:]
    ms = jnp.mean(o * o, axis=0, keepdims=True)
    gain = pltpu.repeat(subg_ref[...], tq // LANES, axis=1)
    y = o * lax.rsqrt(ms + EPS) * gain * (1.0 - lambda_init)
    o_ref[0] = y.T.astype(BF16)


def _diff_attn(pt, kc, lam_params, subg, lambda_init, safe):
    B, _, S = pt.shape
    tq, tk = TQ_A, min(TK_A, S)
    kern = functools.partial(_diff_attn_kernel, tk=tk, safe=safe, lambda_init=lambda_init)
    return pl.pallas_call(
        kern,
        grid=(B, A_HEADS, S // tq),
        in_specs=[
            pl.BlockSpec((1, 128, tq), lambda b, h, i: (b, h, i)),
            pl.BlockSpec((1, S, 128), lambda b, h, i: (b, 0, h)),
            pl.BlockSpec((1, 128, S), lambda b, h, i: (b, 4 + h, 0)),
            pl.BlockSpec((4, HEAD_DIM), lambda b, h, i: (0, 0)),
            pl.BlockSpec((A_VDIM, LANES), lambda b, h, i: (0, 0)),
        ],
        out_specs=pl.BlockSpec((1, tq, 128), lambda b, h, i: (b, i, h)),
        out_shape=jax.ShapeDtypeStruct((B, S, A_HEADS * A_VDIM), BF16),
        scratch_shapes=[
            pltpu.VMEM((128, 2 * tq), BF16),
            pltpu.VMEM((A_VDIM, 2 * tq), F32),
            pltpu.VMEM((1, 2 * tq), F32),
            pltpu.VMEM((1, 2 * tq), F32),
        ],
        compiler_params=_cparams(("parallel", "parallel", "arbitrary"), 48),
        name="diff_attn_safe" if safe else "diff_attn",
    )(pt, kc, pt, lam_params, subg)


def _na_kernel(q_ref, k_ref, v_ref, bias_ref, o_ref, *, safe):
    S = k_ref.shape[1]
    rows = S // GRID_W
    r = pl.program_id(2)
    q = q_ref[0]
    z = jnp.zeros((HEAD_DIM, NA_PAIR_Q), BF16)
    npairs = q_ref.shape[2] // NA_PAIR_Q
    scores, vals = [], []
    for u in range(npairs):
        mp = r * npairs + u
        srow = jnp.clip(2 * mp - NA_KH // 2, 0, rows - NA_PAIR_ROWS)
        start = pl.multiple_of(srow * GRID_W, 128)
        var = jnp.where(mp == 0, 0, jnp.where(mp == 1, 1, jnp.where(
            mp == rows // 2 - 2, 3, jnp.where(mp == rows // 2 - 1, 4, 2))))
        ks = k_ref[0, pl.ds(start, NA_PAIR_K), :]
        vals.append(v_ref[0, :, pl.ds(start, NA_PAIR_K)])
        qp = q[:, u * NA_PAIR_Q:(u + 1) * NA_PAIR_Q]
        qz = jnp.concatenate([jnp.concatenate([qp[:HEAD_DIM], z], axis=0),
                              jnp.concatenate([z, qp[HEAD_DIM:]], axis=0)], axis=1)
        scores.append(_dot(ks, qz) + bias_ref[0, var])
    probs, sums = [], []
    for s in scores:
        p = jnp.exp(s - jnp.max(s, axis=0, keepdims=True)) if safe else jnp.exp(s)
        sums.append(jnp.sum(p, axis=0, keepdims=True))
        probs.append(p.astype(BF16))
    outs = []
    for vs, p, l in zip(vals, probs, sums):
        o = _dot(vs, p) / l
        outs.append(jnp.concatenate([o[:HEAD_DIM, :NA_PAIR_Q], o[HEAD_DIM:, NA_PAIR_Q:]], axis=0))
    o_ref[0] = jnp.concatenate(outs, axis=1).T.astype(BF16)


def _na_attn(pt, kc, bias, safe):
    B, _, S = pt.shape
    nq = min(NA_Q, S)
    nr = S // nq
    return pl.pallas_call(
        functools.partial(_na_kernel, safe=safe),
        grid=(B, 2, nr),
        in_specs=[
            pl.BlockSpec((1, 128, nq), lambda b, p, r: (b, 8 + p, r)),
            pl.BlockSpec((1, S, 128), lambda b, p, r: (b, 0, 4 + p)),
            pl.BlockSpec((1, 128, S), lambda b, p, r: (b, 10 + p, 0)),
            pl.BlockSpec((1, 5, NA_PAIR_K, 2 * NA_PAIR_Q), lambda b, p, r: (p, 0, 0, 0)),
        ],
        out_specs=pl.BlockSpec((1, nq, 128), lambda b, p, r: (b, r, p)),
        out_shape=jax.ShapeDtypeStruct((B, S, B_HEADS * HEAD_DIM), BF16),
        compiler_params=_cparams(("parallel", "parallel", "arbitrary"), 48),
        name="na_attn_safe" if safe else "na_attn",
    )(pt, kc, pt, bias)


def _na_bias_index():
    rows = 32
    a = np.arange(2)[None, :]
    i = np.arange(NA_PAIR_ROWS)[:, None]
    ridx, rvalid = [], []
    for mp in (0, 1, 5, rows // 2 - 2, rows // 2 - 1):
        srow = np.clip(2 * mp - NA_KH // 2, 0, rows - NA_PAIR_ROWS)
        qrow = 2 * mp + a
        krow = srow + i
        rs = np.clip(qrow - NA_KH // 2, 0, rows - NA_KH)
        rvalid.append((krow >= rs) & (krow < rs + NA_KH))
        ridx.append(np.clip(krow - qrow + NA_KH - 1, 0, 2 * NA_KH - 2))
    ridx = np.stack(ridx)
    rvalid = np.stack(rvalid)
    qc = np.arange(GRID_W)[None, :]
    kcol = np.arange(GRID_W)[:, None]
    cs = np.clip(qc - NA_KW // 2, 0, GRID_W - NA_KW)
    cvalid = (kcol >= cs) & (kcol < cs + NA_KW)
    cidx = np.clip(kcol - qc + NA_KW - 1, 0, 2 * NA_KW - 2)
    valid = rvalid[:, :, None, :, None] & cvalid[None, None, :, None, :]
    col_onehot = (cidx.reshape(-1)[None, :] == np.arange(2 * NA_KW - 1)[:, None])
    return (ridx.astype(np.int32), col_onehot.astype(np.float32),
            valid.reshape(5, NA_PAIR_K, NA_PAIR_Q))


_NA_RIDX, _NA_COL_ONEHOT, _NA_VALID = _na_bias_index()


def _na_bias_table(rpb):
    rows = jnp.take(rpb.astype(F32), jnp.asarray(_NA_RIDX), axis=1)
    tab = jnp.einsum("hvkqc,cn->hvkqn", rows, jnp.asarray(_NA_COL_ONEHOT),
                     precision=lax.Precision.HIGHEST)
    tab = tab.reshape(B_HEADS, 5, NA_PAIR_ROWS, 2, GRID_W, GRID_W).transpose(0, 1, 2, 4, 3, 5)
    tab = tab.reshape(B_HEADS, 5, NA_PAIR_K, NA_PAIR_Q)
    tab = jnp.where(jnp.asarray(_NA_VALID)[None], tab, NEG)
    tab = tab.reshape(2, 2, 5, NA_PAIR_K, NA_PAIR_Q).transpose(0, 2, 3, 1, 4)
    return tab.reshape(2, 5, NA_PAIR_K, 2 * NA_PAIR_Q)


def _win_kernel(sink_ref, q_ref, k_ref, v_ref, o_ref, *, safe):
    S = k_ref.shape[1]
    nsub = q_ref.shape[2] // TQ_C
    z = jnp.zeros((HEAD_DIM, TQ_C), BF16)
    gq = C_HEADS // C_KV_HEADS
    scores, vals = [], []
    for u in range(nsub):
        q0 = (pl.program_id(1) * nsub + u) * TQ_C
        start = pl.multiple_of(jnp.clip(q0 - WINDOW, 0, S - TK_C), 128)
        ks = k_ref[0, pl.ds(start, TK_C), :]
        vals.append(v_ref[0, :, pl.ds(start, TK_C)])
        kpos = start + lax.broadcasted_iota(jnp.int32, (TK_C, TQ_C), 0)
        qpos = q0 + lax.broadcasted_iota(jnp.int32, (TK_C, TQ_C), 1)
        mask = jnp.where(jnp.abs(qpos - kpos) <= WINDOW, 0.0, NEG).astype(F32)
        for hq in range(C_HEADS):
            qh = q_ref[0, hq * HEAD_DIM:(hq + 1) * HEAD_DIM, u * TQ_C:(u + 1) * TQ_C]
            qz = jnp.concatenate([qh, z] if hq // gq == 0 else [z, qh], axis=0)
            scores.append(_dot(ks, qz) + mask)
    probs, dens = [], []
    for idx, s in enumerate(scores):
        sk = sink_ref[idx % C_HEADS]
        if safe:
            m = jnp.maximum(jnp.max(s, axis=0, keepdims=True), sk)
            e = jnp.exp(s - m)
            dens.append(jnp.sum(e, axis=0, keepdims=True) + jnp.exp(sk - m))
        else:
            e = jnp.exp(s)
            dens.append(jnp.sum(e, axis=0, keepdims=True) + jnp.exp(sk))
        probs.append(e.astype(BF16))
    blocks = []
    for u in range(nsub):
        outs = []
        for hq in range(C_HEADS):
            kv = hq // gq
            e, den = probs[u * C_HEADS + hq], dens[u * C_HEADS + hq]
            outs.append(_dot(vals[u][kv * HEAD_DIM:(kv + 1) * HEAD_DIM], e) / den)
        blocks.append(jnp.concatenate(outs, axis=0))
    o_ref[0] = jnp.concatenate(blocks, axis=1).T.astype(BF16)


def _win_attn(pt, kc, sink, safe):
    B, _, S = pt.shape
    nq = min(WIN_Q, S)
    return pl.pallas_call(
        functools.partial(_win_kernel, safe=safe),
        grid=(B, S // nq),
        in_specs=[
            pl.BlockSpec(memory_space=pltpu.SMEM),
            pl.BlockSpec((1, 256, nq), lambda b, i: (b, 6, i)),
            pl.BlockSpec((1, S, 128), lambda b, i: (b, 0, 6)),
            pl.BlockSpec((1, 128, S), lambda b, i: (b, 14, 0)),
        ],
        out_specs=pl.BlockSpec((1, nq, 256), lambda b, i: (b, i, 0)),
        out_shape=jax.ShapeDtypeStruct((B, S, C_HEADS * HEAD_DIM), BF16),
        compiler_params=_cparams(("parallel", "arbitrary"), 48),
        name="win_attn_safe" if safe else "win_attn",
    )(sink, pt, kc, pt)


def _outproj_kernel(x_ref, ya_ref, yb_ref, yc_ref, wo_ref, g2_ref, wr_ref, br_ref, tri_ref,
                    h_ref, xs_ref, cs_ref, ri_ref, cnt_ref):
    tm = x_ref.shape[0]
    h = (x_ref[...]
         + _dot(ya_ref[...], wo_ref[0:512, :])
         + _dot(yb_ref[...], wo_ref[512:768, :])
         + _dot(yc_ref[...], wo_ref[768:1024, :]))
    h_ref[...] = h
    ms = jnp.mean(h * h, axis=-1, keepdims=True)
    n2 = h * lax.rsqrt(ms + EPS) * g2_ref[...]
    n2_hi = n2.astype(BF16)
    n2_lo = (n2 - n2_hi.astype(F32)).astype(BF16)
    both = _dot(n2_hi, wr_ref[...])
    logits = both[:, :LANES] + both[:, LANES:] + _dot(n2_lo, wr_ref[:, :LANES]) + br_ref[...]
    lane = lax.broadcasted_iota(jnp.int32, logits.shape, 1).astype(F32)
    big = float(LANES)
    ninf = -jnp.inf

    def first_argmax(vals):
        vmax = jnp.max(vals, axis=-1, keepdims=True)
        idx = jnp.min(jnp.where(vals == vmax, lane, big), axis=-1, keepdims=True)
        return vmax, idx

    gl = jnp.where(lane < N_GROUPS, logits, ninf)
    gmax, gidx = first_argmax(gl)
    gp = 1.0 / jnp.sum(jnp.exp(gl - gmax), axis=-1, keepdims=True)
    lo = N_GROUPS + EXPERTS_PER_GROUP * gidx
    el = jnp.where((lane >= lo) & (lane < lo + EXPERTS_PER_GROUP), logits, ninf)
    v1, i1 = first_argmax(el)
    el2 = jnp.where(lane == i1, ninf, el)
    v2, i2 = first_argmax(el2)
    t = jnp.exp(v2 - v1)
    w1 = 1.0 / (1.0 + t)
    w2 = t / (1.0 + t)
    comb = jnp.where(lane == i1, gp * w1, jnp.where(lane == i2, gp * w2, 0.0))

    onehot = jnp.where(lane == gidx, 1.0, 0.0)
    before = _dot(tri_ref[...], onehot.astype(BF16))
    cnt = jnp.sum(onehot, axis=0, keepdims=True)
    pos = jnp.sum(jnp.where(lane < gidx, cnt, 0.0) + jnp.where(lane == gidx, before, 0.0),
                  axis=-1, keepdims=True)
    ri = jnp.where(lane == 0.0, pos, comb)
    ri_ref[...] = ri
    hi = ri.astype(BF16)
    r1 = ri - hi.astype(F32)
    mid = r1.astype(BF16)
    low = (r1 - mid.astype(F32)).astype(BF16)
    rs = jnp.concatenate([hi, mid, low], axis=1)
    pos_t = jnp.broadcast_to(pos, (tm, LANES)).T
    want = lax.broadcasted_iota(jnp.int32, (tm, tm), 0).astype(F32)
    perm = jnp.where(pos_t[0:1, :] == want, 1.0, 0.0).astype(BF16)
    xs_ref[0, 0:tm, :] = _dot(perm, n2_hi).astype(BF16)
    c3 = _dot(perm, rs)
    cs_ref[0, 0:tm, :] = c3[:, :LANES] + c3[:, LANES:2 * LANES] + c3[:, 2 * LANES:]
    xs_ref[0, tm:, :] = jnp.zeros((MOE_PAD, D_MODEL), BF16)
    cs_ref[0, tm:, :] = jnp.zeros((MOE_PAD, LANES), F32)
    cnt_ref[0] = jnp.broadcast_to(cnt, (8, LANES)).astype(jnp.int32)


def _outproj(x2, ya, yb, yc, wo, g2, wr, br, tri):
    T = x2.shape[0]
    tm = T_MOE
    nt = T // tm
    rows = tm + MOE_PAD
    row = lambda i: (i, 0)
    fixed = lambda i: (0, 0)
    return pl.pallas_call(
        _outproj_kernel,
        grid=(nt,),
        in_specs=[
            pl.BlockSpec((tm, D_MODEL), row),
            pl.BlockSpec((tm, 512), row),
            pl.BlockSpec((tm, 256), row),
            pl.BlockSpec((tm, 256), row),
            pl.BlockSpec((D_MODEL, D_MODEL), fixed),
            pl.BlockSpec((1, D_MODEL), fixed),
            pl.BlockSpec((D_MODEL, 2 * LANES), fixed),
            pl.BlockSpec((1, LANES), fixed),
            pl.BlockSpec((tm, tm), fixed),
        ],
        out_specs=[
            pl.BlockSpec((tm, D_MODEL), row),
            pl.BlockSpec((1, rows, D_MODEL), lambda i: (i, 0, 0)),
            pl.BlockSpec((1, rows, LANES), lambda i: (i, 0, 0)),
            pl.BlockSpec((tm, LANES), row),
            pl.BlockSpec((1, 8, LANES), lambda i: (i, 0, 0)),
        ],
        out_shape=[
            jax.ShapeDtypeStruct((T, D_MODEL), F32),
            jax.ShapeDtypeStruct((nt, rows, D_MODEL), BF16),
            jax.ShapeDtypeStruct((nt, rows, LANES), F32),
            jax.ShapeDtypeStruct((T, LANES), F32),
            jax.ShapeDtypeStruct((nt, 8, LANES), jnp.int32),
        ],
        compiler_params=_cparams(("parallel",), 56),
        name="outproj",
    )(x2, ya, yb, yc, wo, g2, wr, br, tri)


def _moe_kernel(cnt_ref, xs_ref, cs_ref, wg_ref, wu_ref, wd_ref, o_ref, acc_ref):
    tiles = xs_ref.shape[0]
    nexp = wg_ref.shape[0]
    step = pl.program_id(1)
    e0 = step * nexp
    g = lax.shift_right_logical(e0, 2)

    @pl.when(step == 0)
    def _():
        acc_ref[...] = jnp.zeros_like(acc_ref)

    def ffn_pass(t, r0, size):
        r0 = pl.multiple_of(r0, 16)
        xs = xs_ref[t, pl.ds(r0, size), :]
        cs = cs_ref[t, pl.ds(r0, size), :]
        lane = lax.broadcasted_iota(jnp.int32, cs.shape, 1)
        ups = [(_dot(xs, wg_ref[j]), _dot(xs, wu_ref[j])) for j in range(nexp)]
        hidden = []
        for j, (hg, hu) in enumerate(ups):
            w = jnp.sum(jnp.where(lane == N_GROUPS + e0 + j, cs, 0.0), axis=-1, keepdims=True)
            hidden.append((hg * jax.nn.sigmoid(hg) * hu * w).astype(BF16))
        out = _dot(hidden[0], wd_ref[0])
        for j in range(1, nexp):
            out = out + _dot(hidden[j], wd_ref[j])
        acc_ref[t, pl.ds(r0, size), :] = acc_ref[t, pl.ds(r0, size), :] + out

    def tile_body(t, carry):
        i = pl.program_id(0) * tiles + t
        n = [cnt_ref[i * N_GROUPS + k] for k in range(N_GROUPS)]
        off = (jnp.where(g > 0, n[0], 0) + jnp.where(g > 1, n[1], 0) + jnp.where(g > 2, n[2], 0))
        ng = jnp.where(g == 0, n[0], jnp.where(g == 1, n[1], jnp.where(g == 2, n[2], n[3])))
        start = lax.shift_left(lax.shift_right_logical(off, 4), 4)
        need = off - start + ng
        nfull = jnp.maximum(lax.div(need - 1, MOE_SLAB), 0)
        rem_steps = lax.div(need - nfull * MOE_SLAB + (MOE_STEP - 1), MOE_STEP)

        def full_pass(s, c):
            ffn_pass(t, start + s * MOE_SLAB, MOE_SLAB)
            return c
        lax.fori_loop(0, nfull, full_pass, 0)

        for k in range(1, MOE_SLAB // MOE_STEP + 1):
            @pl.when(rem_steps == k)
            def _(k=k):
                ffn_pass(t, start + nfull * MOE_SLAB, k * MOE_STEP)
        return carry
    lax.fori_loop(0, tiles, tile_body, 0)

    @pl.when(step == pl.num_programs(1) - 1)
    def _():
        o_ref[...] = acc_ref[...].astype(BF16)


def _moe(xs, cs, cnt, wg, wu, wd):
    nt, rows, _ = xs.shape
    tpw = MOE_TILES_PER_WEIGHT if nt % MOE_TILES_PER_WEIGHT == 0 else 1
    nexp = MOE_EXPERTS_PER_STEP
    tmap = lambda s, e, c: (s, 0, 0)
    wmap = lambda s, e, c: (e, 0, 0)
    return pl.pallas_call(
        _moe_kernel,
        grid_spec=pltpu.PrefetchScalarGridSpec(
            num_scalar_prefetch=1,
            grid=(nt // tpw, N_EXPERTS // nexp),
            in_specs=[
                pl.BlockSpec((tpw, rows, D_MODEL), tmap),
                pl.BlockSpec((tpw, rows, LANES), tmap),
                pl.BlockSpec((nexp, D_MODEL, D_EXPERT), wmap),
                pl.BlockSpec((nexp, D_MODEL, D_EXPERT), wmap),
                pl.BlockSpec((nexp, D_EXPERT, D_MODEL), wmap),
            ],
            out_specs=pl.BlockSpec((tpw, rows, D_MODEL), tmap),
            scratch_shapes=[pltpu.VMEM((tpw, rows, D_MODEL), F32)],
        ),
        out_shape=jax.ShapeDtypeStruct((nt, rows, D_MODEL), BF16),
        compiler_params=_cparams(("parallel", "arbitrary"), 60),
        name="moe",
    )(cnt, xs, cs, wg, wu, wd)


def _unsort_kernel(os_ref, h_ref, ri_ref, o_ref):
    tm = h_ref.shape[0]
    want = lax.broadcasted_iota(jnp.int32, (tm, tm), 1).astype(F32)
    perm_t = jnp.where(ri_ref[:, 0:1] == want, 1.0, 0.0).astype(BF16)
    o_ref[...] = h_ref[...] + _dot(perm_t, os_ref[0, 0:tm, :])


def _unsort(os, h, ri):
    T = h.shape[0]
    tm = T_MOE
    rows = os.shape[1]
    row = lambda i: (i, 0)
    return pl.pallas_call(
        _unsort_kernel,
        grid=(T // tm,),
        in_specs=[
            pl.BlockSpec((1, rows, D_MODEL), lambda i: (i, 0, 0)),
            pl.BlockSpec((tm, D_MODEL), row),
            pl.BlockSpec((tm, LANES), row),
        ],
        out_specs=pl.BlockSpec((tm, D_MODEL), row),
        out_shape=jax.ShapeDtypeStruct((T, D_MODEL), F32),
        compiler_params=_cparams(("parallel",), 48),
        name="unsort",
    )(os, h, ri)


def _rope_tables_t(seq_len):
    pos = jnp.arange(seq_len, dtype=F32)
    inv = ROPE_THETA ** (-jnp.arange(0, HEAD_DIM, 2, dtype=F32) / HEAD_DIM)
    ang = pos[:, None] * inv[None, :]
    ang = jnp.concatenate([ang, ang], axis=-1)
    sign = jnp.where(jnp.arange(HEAD_DIM) < HEAD_DIM // 2, -1.0, 1.0).astype(F32)
    return jnp.cos(ang).T, (jnp.sin(ang) * sign[None, :]).T


def _lane_bcast(v):
    return jnp.broadcast_to(v.astype(F32)[:, None], (v.shape[0], LANES))


def _prep_layer(l, p):
    w_in = p["w_in"][l]
    cols = [w_in[:, _COLS[n][0]:_COLS[n][1]] for n in _PT_ORDER + _KT_ORDER]
    w_t = jnp.concatenate(cols, axis=1).T.astype(BF16)
    gains = jnp.stack([
        _lane_bcast(p["a_q_norm"][l] * (QK_SCALE * LOG2E)), _lane_bcast(p["a_k_norm"][l]),
        _lane_bcast(p["b_q_norm"][l] * QK_SCALE), _lane_bcast(p["b_k_norm"][l]),
        _lane_bcast(p["c_q_norm"][l] * QK_SCALE), _lane_bcast(p["c_k_norm"][l]),
    ])
    wr = jnp.zeros((D_MODEL, LANES), F32)
    wr = wr.at[:, :N_GROUPS].set(p["w_group"][l]).at[:, N_GROUPS:N_GROUPS + N_EXPERTS].set(p["w_router"][l])
    wr_hi = wr.astype(BF16)
    wr_lo = (wr - wr_hi.astype(F32)).astype(BF16)
    wr2 = jnp.concatenate([wr_hi, wr_lo], axis=1)
    br = jnp.zeros((1, LANES), F32)
    br = br.at[0, :N_GROUPS].set(p["b_group"][l]).at[0, N_GROUPS:N_GROUPS + N_EXPERTS].set(p["b_router"][l])
    def score_bound(gq, gk):
        return 1.02 * HEAD_DIM * QK_SCALE * jnp.max(jnp.abs(gq)) * jnp.max(jnp.abs(gk))

    bound = score_bound(p["a_q_norm"][l], p["a_k_norm"][l])
    bound_b = score_bound(p["b_q_norm"][l], p["b_k_norm"][l]) + jnp.max(jnp.abs(p["b_rpb"][l]))
    bound_c = jnp.maximum(score_bound(p["c_q_norm"][l], p["c_k_norm"][l]), jnp.max(jnp.abs(p["c_sink"][l])))
    return dict(
        fast_b=bound_b <= FAST_SOFTMAX_BOUND, fast_c=bound_c <= FAST_SOFTMAX_BOUND,
        g1=p["norm1_g"][l].astype(F32)[None, :], w_t=w_t, gains=gains,
        lam=jnp.stack([p["a_lambda_q1"][l], p["a_lambda_k1"][l],
                       p["a_lambda_q2"][l], p["a_lambda_k2"][l]]).astype(F32),
        subg=_lane_bcast(p["a_subln"][l]),
        na_bias=_na_bias_table(p["b_rpb"][l]),
        sink=p["c_sink"][l].astype(F32),
        wo=p["w_out"][l].astype(BF16), g2=p["norm2_g"][l].astype(F32)[None, :],
        wr=wr2, br=br,
        wg=p["w_gate"][l].astype(BF16), wu=p["w_up"][l].astype(BF16), wd=p["w_down"][l].astype(BF16),
        fast_ok=bound <= FAST_SOFTMAX_BOUND,
    )


def _layer(x, lp, lambda_init, cos_t, sin_t):
    B, S, _ = x.shape
    pt, kc = _inproj(x, lp["g1"], lp["w_t"], lp["gains"], cos_t, sin_t)
    ya = lax.cond(
        lp["fast_ok"],
        lambda a, b: _diff_attn(a, b, lp["lam"], lp["subg"], lambda_init, False),
        lambda a, b: _diff_attn(a, b, lp["lam"], lp["subg"], lambda_init, True),
        pt, kc)
    yb = lax.cond(lp["fast_b"],
                  lambda a, b: _na_attn(a, b, lp["na_bias"], False),
                  lambda a, b: _na_attn(a, b, lp["na_bias"], True), pt, kc)
    yc = lax.cond(lp["fast_c"],
                  lambda a, b: _win_attn(a, b, lp["sink"], False),
                  lambda a, b: _win_attn(a, b, lp["sink"], True), pt, kc)
    T = B * S
    tri = jnp.tril(jnp.ones((T_MOE, T_MOE), BF16), -1)
    h, xs, cs, ri, cnt = _outproj(x.reshape(T, D_MODEL), ya.reshape(T, -1), yb.reshape(T, -1),
                                  yc.reshape(T, -1), lp["wo"], lp["g2"], lp["wr"], lp["br"], tri)
    counts = cnt[:, 0, :N_GROUPS].reshape(-1)
    os = _moe(xs, cs, counts, lp["wg"], lp["wu"], lp["wd"])
    y = _unsort(os, h, ri)
    return y.reshape(B, S, D_MODEL)


def kernel(x_prompt, x_sample, norm1_g, w_in, w_out, a_q_norm, a_k_norm, a_lambda_q1, a_lambda_k1,
           a_lambda_q2, a_lambda_k2, a_subln, b_q_norm, b_k_norm, b_rpb, c_q_norm, c_k_norm, c_sink,
           norm2_g, w_group, b_group, w_router, b_router, w_gate, w_up, w_down):
    p = dict(norm1_g=norm1_g, w_in=w_in, w_out=w_out, a_q_norm=a_q_norm, a_k_norm=a_k_norm,
             a_lambda_q1=a_lambda_q1, a_lambda_k1=a_lambda_k1, a_lambda_q2=a_lambda_q2,
             a_lambda_k2=a_lambda_k2, a_subln=a_subln, b_q_norm=b_q_norm, b_k_norm=b_k_norm,
             b_rpb=b_rpb, c_q_norm=c_q_norm, c_k_norm=c_k_norm, c_sink=c_sink, norm2_g=norm2_g,
             w_group=w_group, b_group=b_group, w_router=w_router, b_router=b_router,
             w_gate=w_gate, w_up=w_up, w_down=w_down)
    layers = [_prep_layer(l, p) for l in range(DEPTH)]

    def run(x):
        cos_t, sin_t = _rope_tables_t(x.shape[1])
        for l in range(DEPTH):
            lambda_init = 0.8 - 0.6 * math.exp(-0.3 * l)
            x = _layer(x, layers[l], lambda_init, cos_t, sin_t)
        return x

    return (run(x_prompt), run(x_sample))
```

```python
import functools
import math

import numpy as np
import jax
import jax.numpy as jnp
from jax import lax
from jax.experimental import pallas as pl
from jax.experimental.pallas import tpu as pltpu

F32 = jnp.float32
BF16 = jnp.bfloat16

D_MODEL = 1024
DEPTH = 2
HEAD_DIM = 64
GRID_W = 64
EPS = 1e-6
ROPE_THETA = 10000.0
A_HEADS = 4
A_VDIM = 2 * HEAD_DIM
B_HEADS = 4
NA_KH = 8
NA_KW = 16
C_HEADS = 4
C_KV_HEADS = 2
WINDOW = 128
N_GROUPS = 4
EXPERTS_PER_GROUP = 4
N_EXPERTS = 16
D_EXPERT = 512
QK_SCALE = HEAD_DIM ** -0.5
LOG2E = math.log2(math.e)

_COLS = dict(aq=(0, 512), ak=(512, 1024), av=(1024, 1536), bq=(1536, 1792), bk=(1792, 2048),
             bv=(2048, 2304), cq=(2304, 2560), ck=(2560, 2688), cv=(2688, 2816))
_PT_ORDER = ("aq", "av", "bq", "bv", "cq", "cv")
_KT_ORDER = ("ak", "bk", "ck")
PT_ROWS = 1920
KC_COLS = 896
IN_WIDTH = 2816

LANES = 128
NEG = -1e30

TM_PROJ = 1024
TQ_A = 1024
TK_A = 2048
UNROLL_A = 2
NA_Q = 2048
NA_PAIR_Q = 2 * GRID_W
NA_PAIR_ROWS = NA_KH + 2
NA_PAIR_K = NA_PAIR_ROWS * GRID_W
TQ_C = 256
WIN_Q = 1024
TK_C = TQ_C + 2 * WINDOW
T_MOE = 1024
MOE_SLAB = 512
MOE_STEP = 64
MOE_PAD = 80
MOE_TILES_PER_WEIGHT = 2
MOE_EXPERTS_PER_STEP = 2
FAST_SOFTMAX_BOUND = 30.0


def _cparams(sem, vmem_mb):
    return pltpu.CompilerParams(dimension_semantics=sem, vmem_limit_bytes=vmem_mb * 1024 * 1024)


def _nt_dot(a, b):
    return lax.dot_general(a, b, (((1,), (1,)), ((), ())), preferred_element_type=F32)


def _dot(a, b):
    return jnp.dot(a, b, preferred_element_type=F32)


def _inproj_kernel(x_ref, g1_ref, w_ref, gains_ref, cos_ref, sin_ref, pt_ref, kc_ref):
    tm = x_ref.shape[1]
    x = x_ref[0]
    ms = jnp.mean(x * x, axis=-1, keepdims=True)
    n = (x * lax.rsqrt(ms + EPS) * g1_ref[...]).astype(BF16)
    cos = cos_ref[...]
    sin = sin_ref[...]
    reps = tm // LANES

    def project(r0, r1):
        return _nt_dot(w_ref[r0:r1, :], n)

    def head(chunk, r0, gi, rope):
        xh = chunk[r0:r0 + HEAD_DIM, :]
        ssq = jnp.sum(xh * xh, axis=0, keepdims=True)
        gain = pltpu.repeat(gains_ref[gi], reps, axis=1)
        xg = xh * lax.rsqrt(ssq * (1.0 / HEAD_DIM) + EPS) * gain
        if rope:
            swapped = jnp.concatenate([xg[HEAD_DIM // 2:], xg[:HEAD_DIM // 2]], axis=0)
            xg = xg * cos + swapped * sin
        return xg

    def put(r0, val):
        pt_ref[0, r0:r0 + val.shape[0], :] = val.astype(BF16)

    c = project(0, 512)
    for j in range(8):
        put(j * 64, head(c, j * 64, 0, True))
    put(512, project(512, 1024))
    c = project(1024, 1536)
    for j in range(4):
        put(1024 + j * 64, head(c, j * 64, 2, False))
    put(1280, c[256:512, :])
    c = project(1536, 1920)
    for j in range(4):
        put(1536 + j * 64, head(c, j * 64, 4, True))
    put(1792, c[256:384, :])
    def put_keys(chunk, j0, specs):
        for j, (gi, rope) in enumerate(specs):
            blk = jnp.concatenate([head(chunk, j * 128, gi, rope),
                                   head(chunk, j * 128 + 64, gi, rope)], axis=0)
            kc_ref[0, :, (j0 + j) * 128:(j0 + j + 1) * 128] = blk.T.astype(BF16)

    put_keys(project(PT_ROWS, PT_ROWS + 512), 0, [(1, True)] * 4)
    put_keys(project(PT_ROWS + 512, IN_WIDTH), 4, [(3, False)] * 2 + [(5, True)])


def _inproj(x, g1, w_t, gains, cos_t, sin_t):
    B, S, _ = x.shape
    tm = TM_PROJ
    return pl.pallas_call(
        _inproj_kernel,
        grid=(B, S // tm),
        in_specs=[
            pl.BlockSpec((1, tm, D_MODEL), lambda b, i: (b, i, 0)),
            pl.BlockSpec((1, D_MODEL), lambda b, i: (0, 0)),
            pl.BlockSpec((IN_WIDTH, D_MODEL), lambda b, i: (0, 0)),
            pl.BlockSpec((6, HEAD_DIM, LANES), lambda b, i: (0, 0, 0)),
            pl.BlockSpec((HEAD_DIM, tm), lambda b, i: (0, i)),
            pl.BlockSpec((HEAD_DIM, tm), lambda b, i: (0, i)),
        ],
        out_specs=[
            pl.BlockSpec((1, PT_ROWS, tm), lambda b, i: (b, 0, i)),
            pl.BlockSpec((1, tm, KC_COLS), lambda b, i: (b, i, 0)),
        ],
        out_shape=[
            jax.ShapeDtypeStruct((B, PT_ROWS, S), BF16),
            jax.ShapeDtypeStruct((B, S, KC_COLS), BF16),
        ],
        compiler_params=_cparams(("parallel", "parallel"), 56),
        name="inproj",
    )(x, g1, w_t, gains, cos_t, sin_t)


def _diff_attn_kernel(q_ref, k_ref, v_ref, lam_ref, subg_ref, o_ref, qz_ref, acc_ref, l_ref, m_ref,
                      *, tk, safe, lambda_init):
    tq = q_ref.shape[2]
    S = k_ref.shape[1]
    q = q_ref[0]
    z = jnp.zeros((HEAD_DIM, tq), BF16)
    qz_ref[:, :tq] = jnp.concatenate([q[:HEAD_DIM], z], axis=0)
    qz_ref[:, tq:] = jnp.concatenate([z, q[HEAD_DIM:]], axis=0)
    acc_ref[...] = jnp.zeros_like(acc_ref)
    l_ref[...] = jnp.zeros_like(l_ref)
    if safe:
        m_ref[...] = jnp.full_like(m_ref, NEG)

    def body(i, carry):
        off = pl.multiple_of(i * tk, tk)
        kc = k_ref[0, pl.ds(off, tk), :]
        vc = v_ref[0, :, pl.ds(off, tk)]
        s = _dot(kc, qz_ref[...])
        if safe:
            m_old = m_ref[...]
            m_new = jnp.maximum(m_old, jnp.max(s, axis=0, keepdims=True))
            alpha = jnp.exp2(m_old - m_new)
            p = jnp.exp2(s - m_new)
            m_ref[...] = m_new
            l_ref[...] = alpha * l_ref[...] + jnp.sum(p, axis=0, keepdims=True)
            acc_ref[...] = alpha * acc_ref[...] + _dot(vc, p.astype(BF16))
        else:
            p = jnp.exp2(s)
            l_ref[...] = l_ref[...] + jnp.sum(p, axis=0, keepdims=True)
            acc_ref[...] = acc_ref[...] + _dot(vc, p.astype(BF16))
        return carry

    lax.fori_loop(0, S // tk, body, 0, unroll=UNROLL_A)

    lp = lam_ref[...]
    lam = (jnp.exp(jnp.sum(lp[0:1] * lp[1:2], axis=-1, keepdims=True))
           - jnp.exp(jnp.sum(lp[2:3] * lp[3:4], axis=-1, keepdims=True)) + lambda_init)
    on = acc_ref[...] / l_ref[...]
    o = on[:, :tq] - lam * on[:, tq:]
    ms = jnp.mean(o * o, axis=0, keepdims=True)
    gain = pltpu.repeat(subg_ref[...], tq // LANES, axis=1)
    y = o * lax.rsqrt(ms + EPS) * gain * (1.0 - lambda_init)
    o_ref[0] = y.T.astype(BF16)


def _diff_attn(pt, kc, lam_params, subg, lambda_init, safe):
    B, _, S = pt.shape
    tq, tk = TQ_A, min(TK_A, S)
    kern = functools.partial(_diff_attn_kernel, tk=tk, safe=safe, lambda_init=lambda_init)
    return pl.pallas_call(
        kern,
        grid=(B, A_HEADS, S // tq),
        in_specs=[
            pl.BlockSpec((1, 128, tq), lambda b, h, i: (b, h, i)),
            pl.BlockSpec((1, S, 128), lambda b, h, i: (b, 0, h)),
            pl.BlockSpec((1, 128, S), lambda b, h, i: (b, 4 + h, 0)),
            pl.BlockSpec((4, HEAD_DIM), lambda b, h, i: (0, 0)),
            pl.BlockSpec((A_VDIM, LANES), lambda b, h, i: (0, 0)),
        ],
        out_specs=pl.BlockSpec((1, tq, 128), lambda b, h, i: (b, i, h)),
        out_shape=jax.ShapeDtypeStruct((B, S, A_HEADS * A_VDIM), BF16),
        scratch_shapes=[
            pltpu.VMEM((128, 2 * tq), BF16),
            pltpu.VMEM((A_VDIM, 2 * tq), F32),
            pltpu.VMEM((1, 2 * tq), F32),
            pltpu.VMEM((1, 2 * tq), F32),
        ],
        compiler_params=_cparams(("parallel", "parallel", "arbitrary"), 48),
        name="diff_attn_safe" if safe else "diff_attn",
    )(pt, kc, pt, lam_params, subg)


def _na_kernel(q_ref, k_ref, v_ref, bias_ref, o_ref, *, safe):
    S = k_ref.shape[1]
    rows = S // GRID_W
    r = pl.program_id(2)
    q = q_ref[0]
    z = jnp.zeros((HEAD_DIM, NA_PAIR_Q), BF16)
    npairs = q_ref.shape[2] // NA_PAIR_Q
    scores, vals = [], []
    for u in range(npairs):
        mp = r * npairs + u
        srow = jnp.clip(2 * mp - NA_KH // 2, 0, rows - NA_PAIR_ROWS)
        start = pl.multiple_of(srow * GRID_W, 128)
        var = jnp.where(mp == 0, 0, jnp.where(mp == 1, 1, jnp.where(
            mp == rows // 2 - 2, 3, jnp.where(mp == rows // 2 - 1, 4, 2))))
        ks = k_ref[0, pl.ds(start, NA_PAIR_K), :]
        vals.append(v_ref[0, :, pl.ds(start, NA_PAIR_K)])
        qp = q[:, u * NA_PAIR_Q:(u + 1) * NA_PAIR_Q]
        qz = jnp.concatenate([jnp.concatenate([qp[:HEAD_DIM], z], axis=0),
                              jnp.concatenate([z, qp[HEAD_DIM:]], axis=0)], axis=1)
        scores.append(_dot(ks, qz) + bias_ref[0, var])
    probs, sums = [], []
    for s in scores:
        p = jnp.exp(s - jnp.max(s, axis=0, keepdims=True)) if safe else jnp.exp(s)
        sums.append(jnp.sum(p, axis=0, keepdims=True))
        probs.append(p.astype(BF16))
    outs = []
    for vs, p, l in zip(vals, probs, sums):
        o = _dot(vs, p) / l
        outs.append(jnp.concatenate([o[:HEAD_DIM, :NA_PAIR_Q], o[HEAD_DIM:, NA_PAIR_Q:]], axis=0))
    o_ref[0] = jnp.concatenate(outs, axis=1).T.astype(BF16)


def _na_attn(pt, kc, bias, safe):
    B, _, S = pt.shape
    nq = min(NA_Q, S)
    nr = S // nq
    return pl.pallas_call(
        functools.partial(_na_kernel, safe=safe),
        grid=(B, 2, nr),
        in_specs=[
            pl.BlockSpec((1, 128, nq), lambda b, p, r: (b, 8 + p, r)),
            pl.BlockSpec((1, S, 128), lambda b, p, r: (b, 0, 4 + p)),
            pl.BlockSpec((1, 128, S), lambda b, p, r: (b, 10 + p, 0)),
            pl.BlockSpec((1, 5, NA_PAIR_K, 2 * NA_PAIR_Q), lambda b, p, r: (p, 0, 0, 0)),
        ],
        out_specs=pl.BlockSpec((1, nq, 128), lambda b, p, r: (b, r, p)),
        out_shape=jax.ShapeDtypeStruct((B, S, B_HEADS * HEAD_DIM), BF16),
        compiler_params=_cparams(("parallel", "parallel", "arbitrary"), 48),
        name="na_attn_safe" if safe else "na_attn",
    )(pt, kc, pt, bias)


def _na_bias_index():
    rows = 32
    a = np.arange(2)[None, :]
    i = np.arange(NA_PAIR_ROWS)[:, None]
    ridx, rvalid = [], []
    for mp in (0, 1, 5, rows // 2 - 2, rows // 2 - 1):
        srow = np.clip(2 * mp - NA_KH // 2, 0, rows - NA_PAIR_ROWS)
        qrow = 2 * mp + a
        krow = srow + i
        rs = np.clip(qrow - NA_KH // 2, 0, rows - NA_KH)
        rvalid.append((krow >= rs) & (krow < rs + NA_KH))
        ridx.append(np.clip(krow - qrow + NA_KH - 1, 0, 2 * NA_KH - 2))
    ridx = np.stack(ridx)
    rvalid = np.stack(rvalid)
    qc = np.arange(GRID_W)[None, :]
    kcol = np.arange(GRID_W)[:, None]
    cs = np.clip(qc - NA_KW // 2, 0, GRID_W - NA_KW)
    cvalid = (kcol >= cs) & (kcol < cs + NA_KW)
    cidx = np.clip(kcol - qc + NA_KW - 1, 0, 2 * NA_KW - 2)
    valid = rvalid[:, :, None, :, None] & cvalid[None, None, :, None, :]
    col_onehot = (cidx.reshape(-1)[None, :] == np.arange(2 * NA_KW - 1)[:, None])
    return (ridx.astype(np.int32), col_onehot.astype(np.float32),
            valid.reshape(5, NA_PAIR_K, NA_PAIR_Q))


_NA_RIDX, _NA_COL_ONEHOT, _NA_VALID = _na_bias_index()


def _na_bias_table(rpb):
    rows = jnp.take(rpb.astype(F32), jnp.asarray(_NA_RIDX), axis=1)
    tab = jnp.einsum("hvkqc,cn->hvkqn", rows, jnp.asarray(_NA_COL_ONEHOT),
                     precision=lax.Precision.HIGHEST)
    tab = tab.reshape(B_HEADS, 5, NA_PAIR_ROWS, 2, GRID_W, GRID_W).transpose(0, 1, 2, 4, 3, 5)
    tab = tab.reshape(B_HEADS, 5, NA_PAIR_K, NA_PAIR_Q)
    tab = jnp.where(jnp.asarray(_NA_VALID)[None], tab, NEG)
    tab = tab.reshape(2, 2, 5, NA_PAIR_K, NA_PAIR_Q).transpose(0, 2, 3, 1, 4)
    return tab.reshape(2, 5, NA_PAIR_K, 2 * NA_PAIR_Q)


def _win_kernel(sink_ref, q_ref, k_ref, v_ref, o_ref, *, safe):
    S = k_ref.shape[1]
    nsub = q_ref.shape[2] // TQ_C
    z = jnp.zeros((HEAD_DIM, TQ_C), BF16)
    gq = C_HEADS // C_KV_HEADS
    scores, vals = [], []
    for u in range(nsub):
        q0 = (pl.program_id(1) * nsub + u) * TQ_C
        start = pl.multiple_of(jnp.clip(q0 - WINDOW, 0, S - TK_C), 128)
        ks = k_ref[0, pl.ds(start, TK_C), :]
        vals.append(v_ref[0, :, pl.ds(start, TK_C)])
        kpos = start + lax.broadcasted_iota(jnp.int32, (TK_C, TQ_C), 0)
        qpos = q0 + lax.broadcasted_iota(jnp.int32, (TK_C, TQ_C), 1)
        mask = jnp.where(jnp.abs(qpos - kpos) <= WINDOW, 0.0, NEG).astype(F32)
        for hq in range(C_HEADS):
            qh = q_ref[0, hq * HEAD_DIM:(hq + 1) * HEAD_DIM, u * TQ_C:(u + 1) * TQ_C]
            qz = jnp.concatenate([qh, z] if hq // gq == 0 else [z, qh], axis=0)
            scores.append(_dot(ks, qz) + mask)
    probs, dens = [], []
    for idx, s in enumerate(scores):
        sk = sink_ref[idx % C_HEADS]
        if safe:
            m = jnp.maximum(jnp.max(s, axis=0, keepdims=True), sk)
            e = jnp.exp(s - m)
            dens.append(jnp.sum(e, axis=0, keepdims=True) + jnp.exp(sk - m))
        else:
            e = jnp.exp(s)
            dens.append(jnp.sum(e, axis=0, keepdims=True) + jnp.exp(sk))
        probs.append(e.astype(BF16))
    blocks = []
    for u in range(nsub):
        outs = []
        for hq in range(C_HEADS):
            kv = hq // gq
            e, den = probs[u * C_HEADS + hq], dens[u * C_HEADS + hq]
            outs.append(_dot(vals[u][kv * HEAD_DIM:(kv + 1) * HEAD_DIM], e) / den)
        blocks.append(jnp.concatenate(outs, axis=0))
    o_ref[0] = jnp.concatenate(blocks, axis=1).T.astype(BF16)


def _win_attn(pt, kc, sink, safe):
    B, _, S = pt.shape
    nq = min(WIN_Q, S)
    return pl.pallas_call(
        functools.partial(_win_kernel, safe=safe),
        grid=(B, S // nq),
        in_specs=[
            pl.BlockSpec(memory_space=pltpu.SMEM),
            pl.BlockSpec((1, 256, nq), lambda b, i: (b, 6, i)),
            pl.BlockSpec((1, S, 128), lambda b, i: (b, 0, 6)),
            pl.BlockSpec((1, 128, S), lambda b, i: (b, 14, 0)),
        ],
        out_specs=pl.BlockSpec((1, nq, 256), lambda b, i: (b, i, 0)),
        out_shape=jax.ShapeDtypeStruct((B, S, C_HEADS * HEAD_DIM), BF16),
        compiler_params=_cparams(("parallel", "arbitrary"), 48),
        name="win_attn_safe" if safe else "win_attn",
    )(sink, pt, kc, pt)


def _outproj_kernel(x_ref, ya_ref, yb_ref, yc_ref, wo_ref, g2_ref, wr_ref, br_ref, tri_ref,
                    h_ref, xs_ref, cs_ref, ri_ref, cnt_ref):
    tm = x_ref.shape[0]
    h = (x_ref[...]
         + _dot(ya_ref[...], wo_ref[0:512, :])
         + _dot(yb_ref[...], wo_ref[512:768, :])
         + _dot(yc_ref[...], wo_ref[768:1024, :]))
    h_ref[...] = h
    ms = jnp.mean(h * h, axis=-1, keepdims=True)
    n2 = h * lax.rsqrt(ms + EPS) * g2_ref[...]
    n2_hi = n2.astype(BF16)
    n2_lo = (n2 - n2_hi.astype(F32)).astype(BF16)
    both = _dot(n2_hi, wr_ref[...])
    logits = both[:, :LANES] + both[:, LANES:] + _dot(n2_lo, wr_ref[:, :LANES]) + br_ref[...]
    lane = lax.broadcasted_iota(jnp.int32, logits.shape, 1).astype(F32)
    big = float(LANES)
    ninf = -jnp.inf

    def first_argmax(vals):
        vmax = jnp.max(vals, axis=-1, keepdims=True)
        idx = jnp.min(jnp.where(vals == vmax, lane, big), axis=-1, keepdims=True)
        return vmax, idx

    gl = jnp.where(lane < N_GROUPS, logits, ninf)
    gmax, gidx = first_argmax(gl)
    gp = 1.0 / jnp.sum(jnp.exp(gl - gmax), axis=-1, keepdims=True)
    lo = N_GROUPS + EXPERTS_PER_GROUP * gidx
    el = jnp.where((lane >= lo) & (lane < lo + EXPERTS_PER_GROUP), logits, ninf)
    v1, i1 = first_argmax(el)
    el2 = jnp.where(lane == i1, ninf, el)
    v2, i2 = first_argmax(el2)
    t = jnp.exp(v2 - v1)
    w1 = 1.0 / (1.0 + t)
    w2 = t / (1.0 + t)
    comb = jnp.where(lane == i1, gp * w1, jnp.where(lane == i2, gp * w2, 0.0))

    onehot = jnp.where(lane == gidx, 1.0, 0.0)
    before = _dot(tri_ref[...], onehot.astype(BF16))
    cnt = jnp.sum(onehot, axis=0, keepdims=True)
    pos = jnp.sum(jnp.where(lane < gidx, cnt, 0.0) + jnp.where(lane == gidx, before, 0.0),
                  axis=-1, keepdims=True)
    ri = jnp.where(lane == 0.0, pos, comb)
    ri_ref[...] = ri
    hi = ri.astype(BF16)
    r1 = ri - hi.astype(F32)
    mid = r1.astype(BF16)
    low = (r1 - mid.astype(F32)).astype(BF16)
    rs = jnp.concatenate([hi, mid, low], axis=1)
    pos_t = jnp.broadcast_to(pos, (tm, LANES)).T
    want = lax.broadcasted_iota(jnp.int32, (tm, tm), 0).astype(F32)
    perm = jnp.where(pos_t[0:1, :] == want, 1.0, 0.0).astype(BF16)
    xs_ref[0, 0:tm, :] = _dot(perm, n2_hi).astype(BF16)
    c3 = _dot(perm, rs)
    cs_ref[0, 0:tm, :] = c3[:, :LANES] + c3[:, LANES:2 * LANES] + c3[:, 2 * LANES:]
    xs_ref[0, tm:, :] = jnp.zeros((MOE_PAD, D_MODEL), BF16)
    cs_ref[0, tm:, :] = jnp.zeros((MOE_PAD, LANES), F32)
    cnt_ref[0] = jnp.broadcast_to(cnt, (8, LANES)).astype(jnp.int32)


def _outproj(x2, ya, yb, yc, wo, g2, wr, br, tri):
    T = x2.shape[0]
    tm = T_MOE
    nt = T // tm
    rows = tm + MOE_PAD
    row = lambda i: (i, 0)
    fixed = lambda i: (0, 0)
    return pl.pallas_call(
        _outproj_kernel,
        grid=(nt,),
        in_specs=[
            pl.BlockSpec((tm, D_MODEL), row),
            pl.BlockSpec((tm, 512), row),
            pl.BlockSpec((tm, 256), row),
            pl.BlockSpec((tm, 256), row),
            pl.BlockSpec((D_MODEL, D_MODEL), fixed),
            pl.BlockSpec((1, D_MODEL), fixed),
            pl.BlockSpec((D_MODEL, 2 * LANES), fixed),
            pl.BlockSpec((1, LANES), fixed),
            pl.BlockSpec((tm, tm), fixed),
        ],
        out_specs=[
            pl.BlockSpec((tm, D_MODEL), row),
            pl.BlockSpec((1, rows, D_MODEL), lambda i: (i, 0, 0)),
            pl.BlockSpec((1, rows, LANES), lambda i: (i, 0, 0)),
            pl.BlockSpec((tm, LANES), row),
            pl.BlockSpec((1, 8, LANES), lambda i: (i, 0, 0)),
        ],
        out_shape=[
            jax.ShapeDtypeStruct((T, D_MODEL), F32),
            jax.ShapeDtypeStruct((nt, rows, D_MODEL), BF16),
            jax.ShapeDtypeStruct((nt, rows, LANES), F32),
            jax.ShapeDtypeStruct((T, LANES), F32),
            jax.ShapeDtypeStruct((nt, 8, LANES), jnp.int32),
        ],
        compiler_params=_cparams(("parallel",), 56),
        name="outproj",
    )(x2, ya, yb, yc, wo, g2, wr, br, tri)


def _moe_kernel(cnt_ref, xs_ref, cs_ref, wg_ref, wu_ref, wd_ref, o_ref, acc_ref):
    tiles = xs_ref.shape[0]
    nexp = wg_ref.shape[0]
    step = pl.program_id(1)
    e0 = step * nexp
    g = lax.shift_right_logical(e0, 2)

    @pl.when(step == 0)
    def _():
        acc_ref[...] = jnp.zeros_like(acc_ref)

    def ffn_pass(t, r0, size):
        r0 = pl.multiple_of(r0, 16)
        xs = xs_ref[t, pl.ds(r0, size), :]
        cs = cs_ref[t, pl.ds(r0, size), :]
        lane = lax.broadcasted_iota(jnp.int32, cs.shape, 1)
        ups = [(_dot(xs, wg_ref[j]), _dot(xs, wu_ref[j])) for j in range(nexp)]
        hidden = []
        for j, (hg, hu) in enumerate(ups):
            w = jnp.sum(jnp.where(lane == N_GROUPS + e0 + j, cs, 0.0), axis=-1, keepdims=True)
            hidden.append((hg * jax.nn.sigmoid(hg) * hu * w).astype(BF16))
        out = _dot(hidden[0], wd_ref[0])
        for j in range(1, nexp):
            out = out + _dot(hidden[j], wd_ref[j])
        acc_ref[t, pl.ds(r0, size), :] = acc_ref[t, pl.ds(r0, size), :] + out

    def tile_body(t, carry):
        i = pl.program_id(0) * tiles + t
        n = [cnt_ref[i * N_GROUPS + k] for k in range(N_GROUPS)]
        off = (jnp.where(g > 0, n[0], 0) + jnp.where(g > 1, n[1], 0) + jnp.where(g > 2, n[2], 0))
        ng = jnp.where(g == 0, n[0], jnp.where(g == 1, n[1], jnp.where(g == 2, n[2], n[3])))
        start = lax.shift_left(lax.shift_right_logical(off, 4), 4)
        need = off - start + ng
        nfull = jnp.maximum(lax.div(need - 1, MOE_SLAB), 0)
        rem_steps = lax.div(need - nfull * MOE_SLAB + (MOE_STEP - 1), MOE_STEP)

        def full_pass(s, c):
            ffn_pass(t, start + s * MOE_SLAB, MOE_SLAB)
            return c
        lax.fori_loop(0, nfull, full_pass, 0)

        for k in range(1, MOE_SLAB // MOE_STEP + 1):
            @pl.when(rem_steps == k)
            def _(k=k):
                ffn_pass(t, start + nfull * MOE_SLAB, k * MOE_STEP)
        return carry
    lax.fori_loop(0, tiles, tile_body, 0)

    @pl.when(step == pl.num_programs(1) - 1)
    def _():
        o_ref[...] = acc_ref[...].astype(BF16)


def _moe(xs, cs, cnt, wg, wu, wd):
    nt, rows, _ = xs.shape
    tpw = MOE_TILES_PER_WEIGHT if nt % MOE_TILES_PER_WEIGHT == 0 else 1
    nexp = MOE_EXPERTS_PER_STEP
    tmap = lambda s, e, c: (s, 0, 0)
    wmap = lambda s, e, c: (e, 0, 0)
    return pl.pallas_call(
        _moe_kernel,
        grid_spec=pltpu.PrefetchScalarGridSpec(
            num_scalar_prefetch=1,
            grid=(nt // tpw, N_EXPERTS // nexp),
            in_specs=[
                pl.BlockSpec((tpw, rows, D_MODEL), tmap),
                pl.BlockSpec((tpw, rows, LANES), tmap),
                pl.BlockSpec((nexp, D_MODEL, D_EXPERT), wmap),
                pl.BlockSpec((nexp, D_MODEL, D_EXPERT), wmap),
                pl.BlockSpec((nexp, D_EXPERT, D_MODEL), wmap),
            ],
            out_specs=pl.BlockSpec((tpw, rows, D_MODEL), tmap),
            scratch_shapes=[pltpu.VMEM((tpw, rows, D_MODEL), F32)],
        ),
        out_shape=jax.ShapeDtypeStruct((nt, rows, D_MODEL), BF16),
        compiler_params=_cparams(("parallel", "arbitrary"), 60),
        name="moe",
    )(cnt, xs, cs, wg, wu, wd)


def _unsort_kernel(os_ref, h_ref, ri_ref, o_ref):
    tm = h_ref.shape[0]
    want = lax.broadcasted_iota(jnp.int32, (tm, tm), 1).astype(F32)
    perm_t = jnp.where(ri_ref[:, 0:1] == want, 1.0, 0.0).astype(BF16)
    o_ref[...] = h_ref[...] + _dot(perm_t, os_ref[0, 0:tm, :])


def _unsort(os, h, ri):
    T = h.shape[0]
    tm = T_MOE
    rows = os.shape[1]
    row = lambda i: (i, 0)
    return pl.pallas_call(
        _unsort_kernel,
        grid=(T // tm,),
        in_specs=[
            pl.BlockSpec((1, rows, D_MODEL), lambda i: (i, 0, 0)),
            pl.BlockSpec((tm, D_MODEL), row),
            pl.BlockSpec((tm, LANES), row),
        ],
        out_specs=pl.BlockSpec((tm, D_MODEL), row),
        out_shape=jax.ShapeDtypeStruct((T, D_MODEL), F32),
        compiler_params=_cparams(("parallel",), 48),
        name="unsort",
    )(os, h, ri)


def _rope_tables_t(seq_len):
    pos = jnp.arange(seq_len, dtype=F32)
    inv = ROPE_THETA ** (-jnp.arange(0, HEAD_DIM, 2, dtype=F32) / HEAD_DIM)
    ang = pos[:, None] * inv[None, :]
    ang = jnp.concatenate([ang, ang], axis=-1)
    sign = jnp.where(jnp.arange(HEAD_DIM) < HEAD_DIM // 2, -1.0, 1.0).astype(F32)
    return jnp.cos(ang).T, (jnp.sin(ang) * sign[None, :]).T


def _lane_bcast(v):
    return jnp.broadcast_to(v.astype(F32)[:, None], (v.shape[0], LANES))


def _prep_layer(l, p):
    w_in = p["w_in"][l]
    cols = [w_in[:, _COLS[n][0]:_COLS[n][1]] for n in _PT_ORDER + _KT_ORDER]
    w_t = jnp.concatenate(cols, axis=1).T.astype(BF16)
    gains = jnp.stack([
        _lane_bcast(p["a_q_norm"][l] * (QK_SCALE * LOG2E)), _lane_bcast(p["a_k_norm"][l]),
        _lane_bcast(p["b_q_norm"][l] * QK_SCALE), _lane_bcast(p["b_k_norm"][l]),
        _lane_bcast(p["c_q_norm"][l] * QK_SCALE), _lane_bcast(p["c_k_norm"][l]),
    ])
    wr = jnp.zeros((D_MODEL, LANES), F32)
    wr = wr.at[:, :N_GROUPS].set(p["w_group"][l]).at[:, N_GROUPS:N_GROUPS + N_EXPERTS].set(p["w_router"][l])
    wr_hi = wr.astype(BF16)
    wr_lo = (wr - wr_hi.astype(F32)).astype(BF16)
    wr2 = jnp.concatenate([wr_hi, wr_lo], axis=1)
    br = jnp.zeros((1, LANES), F32)
    br = br.at[0, :N_GROUPS].set(p["b_group"][l]).at[0, N_GROUPS:N_GROUPS + N_EXPERTS].set(p["b_router"][l])
    def score_bound(gq, gk):
        return 1.02 * HEAD_DIM * QK_SCALE * jnp.max(jnp.abs(gq)) * jnp.max(jnp.abs(gk))

    bound = score_bound(p["a_q_norm"][l], p["a_k_norm"][l])
    bound_b = score_bound(p["b_q_norm"][l], p["b_k_norm"][l]) + jnp.max(jnp.abs(p["b_rpb"][l]))
    bound_c = jnp.maximum(score_bound(p["c_q_norm"][l], p["c_k_norm"][l]), jnp.max(jnp.abs(p["c_sink"][l])))
    return dict(
        fast_b=bound_b <= FAST_SOFTMAX_BOUND, fast_c=bound_c <= FAST_SOFTMAX_BOUND,
        g1=p["norm1_g"][l].astype(F32)[None, :], w_t=w_t, gains=gains,
        lam=jnp.stack([p["a_lambda_q1"][l], p["a_lambda_k1"][l],
                       p["a_lambda_q2"][l], p["a_lambda_k2"][l]]).astype(F32),
        subg=_lane_bcast(p["a_subln"][l]),
        na_bias=_na_bias_table(p["b_rpb"][l]),
        sink=p["c_sink"][l].astype(F32),
        wo=p["w_out"][l].astype(BF16), g2=p["norm2_g"][l].astype(F32)[None, :],
        wr=wr2, br=br,
        wg=p["w_gate"][l].astype(BF16), wu=p["w_up"][l].astype(BF16), wd=p["w_down"][l].astype(BF16),
        fast_ok=bound <= FAST_SOFTMAX_BOUND,
    )


def _layer(x, lp, lambda_init, cos_t, sin_t):
    B, S, _ = x.shape
    pt, kc = _inproj(x, lp["g1"], lp["w_t"], lp["gains"], cos_t, sin_t)
    ya = lax.cond(
        lp["fast_ok"],
        lambda a, b: _diff_attn(a, b, lp["lam"], lp["subg"], lambda_init, False),
        lambda a, b: _diff_attn(a, b, lp["lam"], lp["subg"], lambda_init, True),
        pt, kc)
    yb = lax.cond(lp["fast_b"],
                  lambda a, b: _na_attn(a, b, lp["na_bias"], False),
                  lambda a, b: _na_attn(a, b, lp["na_bias"], True), pt, kc)
    yc = lax.cond(lp["fast_c"],
                  lambda a, b: _win_attn(a, b, lp["sink"], False),
                  lambda a, b: _win_attn(a, b, lp["sink"], True), pt, kc)
    T = B * S
    tri = jnp.tril(jnp.ones((T_MOE, T_MOE), BF16), -1)
    h, xs, cs, ri, cnt = _outproj(x.reshape(T, D_MODEL), ya.reshape(T, -1), yb.reshape(T, -1),
                                  yc.reshape(T, -1), lp["wo"], lp["g2"], lp["wr"], lp["br"], tri)
    counts = cnt[:, 0, :N_GROUPS].reshape(-1)
    os = _moe(xs, cs, counts, lp["wg"], lp["wu"], lp["wd"])
    y = _unsort(os, h, ri)
    return y.reshape(B, S, D_MODEL)


def kernel(x_prompt, x_sample, norm1_g, w_in, w_out, a_q_norm, a_k_norm, a_lambda_q1, a_lambda_k1,
           a_lambda_q2, a_lambda_k2, a_subln, b_q_norm, b_k_norm, b_rpb, c_q_norm, c_k_norm, c_sink,
           norm2_g, w_group, b_group, w_router, b_router, w_gate, w_up, w_down):
    p = dict(norm1_g=norm1_g, w_in=w_in, w_out=w_out, a_q_norm=a_q_norm, a_k_norm=a_k_norm,
             a_lambda_q1=a_lambda_q1, a_lambda_k1=a_lambda_k1, a_lambda_q2=a_lambda_q2,
             a_lambda_k2=a_lambda_k2, a_subln=a_subln, b_q_norm=b_q_norm, b_k_norm=b_k_norm,
             b_rpb=b_rpb, c_q_norm=c_q_norm, c_k_norm=c_k_norm, c_sink=c_sink, norm2_g=norm2_g,
             w_group=w_group, b_group=b_group, w_router=w_router, b_router=b_router,
             w_gate=w_gate, w_up=w_up, w_down=w_down)
    layers = [_prep_layer(l, p) for l in range(DEPTH)]

    def run(x):
        cos_t, sin_t = _rope_tables_t(x.shape[1])
        for l in range(DEPTH):
            lambda_init = 0.8 - 0.6 * math.exp(-0.3 * l)
            x = _layer(x, layers[l], lambda_init, cos_t, sin_t)
        return x

    return (run(x_prompt), run(x_sample))
```
